```python
import math, functools
import jax
import jax.numpy as jnp
from jax import lax
import numpy as np

D_MODEL = 2048
BATCH = 4
SEQ = 4096
DEPTH = 1
DEC_BATCH = 32
DEC_SEQ = 4
PAST_LEN = 16384
PAGE_SIZE = 128

D_HEAD = 128
H_A = D_MODEL // 256
H_B = D_MODEL // 256
A_W = H_A * D_HEAD
B_W = H_B * D_HEAD
MIX_W = A_W + B_W
CONV_A = 4
GDN_CHUNK = 64
Q_BLOCK = 128
D_FF = ((8 * D_MODEL // 3 + 127) // 128) * 128
FFN_CONV = 3
EPS = 1e-6
SPLITS = [A_W, 2 * A_W, 3 * A_W, 4 * A_W, 4 * A_W + H_A, 4 * A_W + 2 * H_A,
          4 * A_W + 2 * H_A + B_W, 4 * A_W + 2 * H_A + 2 * B_W,
          4 * A_W + 2 * H_A + 3 * B_W, 4 * A_W + 2 * H_A + 4 * B_W]
N_IN = 4 * A_W + 2 * H_A + 4 * B_W + H_B

kernel_name = 'hymba_gdn_fox_convffn_step'


def rms_norm(x, g):
    xf = x.astype(jnp.float32)
    y = xf * lax.rsqrt(jnp.mean(xf * xf, axis=-1, keepdims=True) + EPS)
    return (y * g.astype(jnp.float32)).astype(x.dtype)


def l2_norm(x):
    xf = x.astype(jnp.float32)
    return xf * lax.rsqrt(jnp.sum(xf * xf, axis=-1, keepdims=True) + EPS)


def heads(x):
    return x.reshape(x.shape[:-1] + (-1, D_HEAD))


def causal_dwconv(x, buf, w):
    width = w.shape[0]
    T = x.shape[1]
    xp = jnp.concatenate([buf.astype(x.dtype), x], axis=1)
    y = sum(w[i] * xp[:, i:i + T] for i in range(width))
    return y, xp[:, T:]


def adaln(c, w_ada, b_ada):
    mod = jnp.einsum('bd,de->be', jax.nn.silu(c), w_ada) + b_ada
    return jnp.split(mod[:, None, :], 6, axis=-1)


def gdn_chunked(q, k, v, g, beta, s0):
    B, T, H, Dk = q.shape
    C = GDN_CHUNK
    pad = (-T) % C
    n = (T + pad) // C

    def blocks(a):
        a = jnp.pad(a.astype(jnp.float32), [(0, 0), (0, pad)] + [(0, 0)] * (a.ndim - 2))
        a = a.reshape((B, n, C) + a.shape[2:])
        return jnp.swapaxes(jnp.moveaxis(a, 1, 0), 2, 3)

    incl = jnp.tril(jnp.ones((C, C), dtype=bool))
    strict = jnp.tril(jnp.ones((C, C), dtype=bool), -1)
    eye = jnp.eye(C, dtype=jnp.float32)

    def chunk_step(S, inp):
        qc, kc, vc, gc, bc = inp
        gcum = jnp.cumsum(gc, axis=-1)
        diff = jnp.where(incl, gcum[..., :, None] - gcum[..., None, :], 0.0)
        decay = jnp.where(incl, jnp.exp(diff), 0.0)
        kb = kc * bc[..., None]
        vb = vc * bc[..., None]
        lower = jnp.where(strict, jnp.einsum('bhik,bhjk->bhij', kb, kc) * decay, 0.0)
        a_mat = eye + lower
        u = lax.linalg.triangular_solve(a_mat, vb, left_side=True, lower=True, unit_diagonal=True)
        w = lax.linalg.triangular_solve(a_mat, kb * jnp.exp(gcum)[..., None], left_side=True,
                                        lower=True, unit_diagonal=True)
        v_new = u - jnp.einsum('bhck,bhkv->bhcv', w, S)
        qk = jnp.where(incl, jnp.einsum('bhik,bhjk->bhij', qc, kc) * decay, 0.0)
        o = (jnp.einsum('bhck,bhkv->bhcv', qc * jnp.exp(gcum)[..., None], S)
             + jnp.einsum('bhij,bhjv->bhiv', qk, v_new))
        g_last = gcum[..., -1]
        S = (S * jnp.exp(g_last)[..., None, None]
             + jnp.einsum('bhck,bhcv->bhkv', kc * jnp.exp(g_last[..., None] - gcum)[..., None], v_new))
        return S, o

    S, o = lax.scan(chunk_step, s0.astype(jnp.float32),
                    (blocks(q), blocks(k), blocks(v), blocks(g), blocks(beta)))
    o = jnp.moveaxis(jnp.swapaxes(o, 2, 3), 0, 1).reshape(B, n * C, H, -1)[:, :T]
    return o, S


def fox_prompt(q, k, v, logf):
    B, T, H, D = q.shape
    f32 = jnp.float32
    scale = D ** -0.5
    cum = jnp.swapaxes(jnp.cumsum(logf, axis=1), 1, 2)
    n_blk = T // Q_BLOCK
    q_blocks = jnp.moveaxis(q.reshape(B, n_blk, Q_BLOCK, H, D), 1, 0)
    cum_blocks = jnp.moveaxis(cum.reshape(B, H, n_blk, Q_BLOCK), 2, 0)
    k_pos = jnp.arange(T)
    v32 = v.astype(f32)

    def block(args):
        idx, qi, ci = args
        s = (jnp.einsum('bqhd,bkhd->bhqk', qi, k, preferred_element_type=f32) * scale
             + ci[..., :, None] - cum[:, :, None, :])
        q_pos = idx * Q_BLOCK + jnp.arange(Q_BLOCK)
        s = jnp.where(k_pos[None, :] <= q_pos[:, None], s, -jnp.inf)
        p = jax.nn.softmax(s, axis=-1)
        return jnp.einsum('bhqk,bkhd->bqhd', p, v32)

    o = lax.map(block, (jnp.arange(n_blk), q_blocks, cum_blocks))
    return jnp.moveaxis(o, 0, 1).reshape(B, T, H, D).astype(q.dtype)


def fox_sample(q, k, v, logf, cache_k, cache_v, cache_logf, page_table, layer):
    Bd, T, H, D = q.shape
    n_pages = page_table.shape[1]
    f32 = jnp.float32
    scale = D ** -0.5
    cn = jnp.swapaxes(jnp.cumsum(logf, axis=1), 1, 2)
    pos = jnp.arange(T)
    causal = pos[None, :] <= pos[:, None]
    s_new = (jnp.einsum('bqhd,bkhd->bhqk', q, k, preferred_element_type=f32) * scale
             + cn[..., :, None] - cn[..., None, :])
    s_new = jnp.where(causal, s_new, -jnp.inf)
    m0 = jnp.max(s_new, axis=-1)
    p0 = jnp.exp(s_new - m0[..., None])
    l0 = jnp.sum(p0, axis=-1)
    acc0 = jnp.einsum('bhqk,bkhd->bhqd', p0, v.astype(f32))
    past_logf = cache_logf[layer][page_table].astype(f32).reshape(Bd, n_pages * PAGE_SIZE, H)
    suffix = lax.cumsum(past_logf, axis=1, reverse=True) - past_logf
    suffix = jnp.moveaxis(suffix.reshape(Bd, n_pages, PAGE_SIZE, H), 1, 0)

    def page_step(carry, inp):
        m, l, acc = carry
        pidx, r = inp
        kp = cache_k[layer][pidx]
        vp = cache_v[layer][pidx]
        s = (jnp.einsum('bqhd,bkhd->bhqk', q, kp, preferred_element_type=f32) * scale
             + cn[..., :, None] + jnp.swapaxes(r, 1, 2)[:, :, None, :])
        m_new = jnp.maximum(m, jnp.max(s, axis=-1))
        corr = jnp.exp(m - m_new)
        p = jnp.exp(s - m_new[..., None])
        l = l * corr + jnp.sum(p, axis=-1)
        acc = acc * corr[..., None] + jnp.einsum('bhqk,bkhd->bhqd', p, vp.astype(f32))
        return (m_new, l, acc), None

    (m, l, acc), _ = lax.scan(page_step, (m0, l0, acc0), (page_table.T, suffix))
    o = acc / l[..., None]
    return jnp.swapaxes(o, 1, 2).astype(q.dtype)


def trunk_layer(x, c, conv_buf, gdn_s0, ffn_buf, fox_attend,
                w_ada, b_ada, g_pre_mix, g_post_mix, g_pre_ffn, g_post_ffn, w_in, w_conv_qkv,
                a_log, dt_bias, g_gdn_norm, q_norm, k_norm, b_forget, w_out, w_up, w_conv_ffn,
                b_conv_ffn, w_down):
    B, T, _ = x.shape
    f32 = jnp.float32
    sh_m, sc_m, gt_m, sh_f, sc_f, gt_f = adaln(c, w_ada, b_ada)
    h = rms_norm(x, g_pre_mix) * (1.0 + sc_m) + sh_m
    proj = jnp.einsum('btd,dn->btn', h, w_in)
    qa, ka, va, za, ba, aa, qb, kb, vb, gb, fb = jnp.split(proj, SPLITS, axis=-1)
    qkv, conv_new = causal_dwconv(jnp.concatenate([qa, ka, va], axis=-1), conv_buf, w_conv_qkv)
    qa, ka, va = jnp.split(jax.nn.silu(qkv), 3, axis=-1)
    q_a = l2_norm(heads(qa)) * D_HEAD ** -0.5
    k_a = l2_norm(heads(ka))
    beta = jax.nn.sigmoid(ba.astype(f32))
    g = -jnp.exp(a_log.astype(f32)) * jax.nn.softplus(aa.astype(f32) + dt_bias.astype(f32))
    o_a, gdn_new = gdn_chunked(q_a, k_a, heads(va), g, beta, gdn_s0)
    o_a = (rms_norm(o_a, g_gdn_norm) * jax.nn.silu(heads(za).astype(f32))).reshape(B, T, A_W).astype(x.dtype)
    q_b = rms_norm(heads(qb), q_norm)
    k_b = rms_norm(heads(kb), k_norm)
    v_b = heads(vb)
    logf = jax.nn.log_sigmoid(fb.astype(f32) + b_forget.astype(f32))
    o_b = fox_attend(q_b, k_b, v_b, logf).reshape(B, T, B_W) * jax.nn.sigmoid(gb)
    mix = jnp.einsum('btm,md->btd', jnp.concatenate([o_a, o_b], axis=-1), w_out)
    x = x + gt_m * rms_norm(mix, g_post_mix)
    h = rms_norm(x, g_pre_ffn) * (1.0 + sc_f) + sh_f
    up, ffn_new = causal_dwconv(jnp.einsum('btd,df->btf', h, w_up), ffn_buf, w_conv_ffn)
    gate, val = jnp.split(up + b_conv_ffn, 2, axis=-1)
    y = jnp.einsum('btf,fd->btd', jax.nn.silu(gate) * val, w_down)
    x = x + gt_f * rms_norm(y, g_post_ffn)
    return x, k_b, v_b, logf, gdn_new, conv_new, ffn_new


def setup_inputs(seed: int = 0) -> dict:
    key = jax.random.key(seed)
    ks = jax.random.split(key, 32)
    f32 = jnp.float32
    n_pages = PAST_LEN // PAGE_SIZE
    used = DEC_BATCH * n_pages
    n_pool = used + max(1, used // 4)

    def nrm(k, shape, s):
        return jax.random.normal(k, shape, f32) * s

    def gain(k, shape):
        return 1.0 + 0.1 * jax.random.normal(k, shape, f32)

    x_prompt = nrm(ks[0], (BATCH, SEQ, D_MODEL), 1.0)
    x_sample = nrm(ks[1], (DEC_BATCH, DEC_SEQ, D_MODEL), 1.0)
    cache_k = nrm(ks[2], (DEPTH, n_pool, PAGE_SIZE, H_B, D_HEAD), 1.0)
    cache_v = nrm(ks[3], (DEPTH, n_pool, PAGE_SIZE, H_B, D_HEAD), 1.0)
    cache_logf = jax.nn.log_sigmoid(jax.random.uniform(ks[4], (DEPTH, n_pool, PAGE_SIZE, H_B), f32, 6.0, 10.0))
    state_gdn = nrm(ks[5], (DEPTH, DEC_BATCH, H_A, D_HEAD, D_HEAD), D_HEAD ** -0.5)
    state_conv_qkv = nrm(ks[6], (DEPTH, DEC_BATCH, CONV_A - 1, 3 * A_W), 1.0)
    state_ffn_conv = nrm(ks[7], (DEPTH, DEC_BATCH, FFN_CONV - 1, 2 * D_FF), 1.0)
    page_table = jax.random.permutation(ks[8], n_pool)[:used].reshape(DEC_BATCH, n_pages).astype(jnp.int32)
    c_prompt = nrm(ks[9], (BATCH, D_MODEL), 1.0)
    c_sample = nrm(ks[10], (DEC_BATCH, D_MODEL), 1.0)
    w_ada = nrm(ks[11], (DEPTH, D_MODEL, 6 * D_MODEL), 0.5 * D_MODEL ** -0.5)
    b_ada = nrm(ks[12], (DEPTH, 6 * D_MODEL), 0.1)
    g_pre_mix = gain(ks[13], (DEPTH, D_MODEL))
    g_post_mix = gain(ks[14], (DEPTH, D_MODEL))
    g_pre_ffn = gain(ks[15], (DEPTH, D_MODEL))
    g_post_ffn = gain(ks[16], (DEPTH, D_MODEL))
    w_in = nrm(ks[17], (DEPTH, D_MODEL, N_IN), D_MODEL ** -0.5)
    w_conv_qkv = nrm(ks[18], (DEPTH, CONV_A, 3 * A_W), CONV_A ** -0.5)
    a_log = jnp.log(jax.random.uniform(ks[19], (DEPTH, H_A), f32, 1.0, 16.0))
    dt = jnp.exp(jax.random.uniform(ks[20], (DEPTH, H_A), f32, math.log(1e-3), math.log(1e-1)))
    dt_bias = dt + jnp.log(-jnp.expm1(-dt))
    g_gdn_norm = gain(ks[21], (DEPTH, D_HEAD))
    q_norm = gain(ks[22], (DEPTH, D_HEAD))
    k_norm = gain(ks[23], (DEPTH, D_HEAD))
    b_forget = jax.random.uniform(ks[24], (DEPTH, H_B), f32, 1.0, 5.0)
    w_out = nrm(ks[25], (DEPTH, MIX_W, D_MODEL), MIX_W ** -0.5)
    w_up = nrm(ks[26], (DEPTH, D_MODEL, 2 * D_FF), D_MODEL ** -0.5)
    w_conv_ffn = nrm(ks[27], (DEPTH, FFN_CONV, 2 * D_FF), FFN_CONV ** -0.5)
    b_conv_ffn = nrm(ks[28], (DEPTH, 2 * D_FF), 0.02)
    w_down = nrm(ks[29], (DEPTH, D_FF, D_MODEL), D_FF ** -0.5)
    return {'x_prompt': x_prompt, 'x_sample': x_sample, 'cache_k': cache_k, 'cache_v': cache_v,
            'cache_logf': cache_logf, 'state_gdn': state_gdn, 'state_conv_qkv': state_conv_qkv,
            'state_ffn_conv': state_ffn_conv, 'page_table': page_table, 'c_prompt': c_prompt,
            'c_sample': c_sample, 'w_ada': w_ada, 'b_ada': b_ada, 'g_pre_mix': g_pre_mix,
            'g_post_mix': g_post_mix, 'g_pre_ffn': g_pre_ffn, 'g_post_ffn': g_post_ffn, 'w_in': w_in,
            'w_conv_qkv': w_conv_qkv, 'a_log': a_log, 'dt_bias': dt_bias, 'g_gdn_norm': g_gdn_norm,
            'q_norm': q_norm, 'k_norm': k_norm, 'b_forget': b_forget, 'w_out': w_out, 'w_up': w_up,
            'w_conv_ffn': w_conv_ffn, 'b_conv_ffn': b_conv_ffn, 'w_down': w_down}


def reference(x_prompt, x_sample, cache_k, cache_v, cache_logf, state_gdn, state_conv_qkv, state_ffn_conv,
              page_table, c_prompt, c_sample, w_ada, b_ada, g_pre_mix, g_post_mix, g_pre_ffn, g_post_ffn,
              w_in, w_conv_qkv, a_log, dt_bias, g_gdn_norm, q_norm, k_norm, b_forget, w_out, w_up,
              w_conv_ffn, b_conv_ffn, w_down):
    B, T = x_prompt.shape[0], x_prompt.shape[1]
    y_prompt, y_sample = x_prompt, x_sample
    rows_p, rows_s = [], []
    for layer in range(DEPTH):
        lw = (w_ada[layer], b_ada[layer], g_pre_mix[layer], g_post_mix[layer], g_pre_ffn[layer],
              g_post_ffn[layer], w_in[layer], w_conv_qkv[layer], a_log[layer], dt_bias[layer],
              g_gdn_norm[layer], q_norm[layer], k_norm[layer], b_forget[layer], w_out[layer],
              w_up[layer], w_conv_ffn[layer], b_conv_ffn[layer], w_down[layer])
        zeros_conv = jnp.zeros((B, CONV_A - 1, 3 * A_W), x_prompt.dtype)
        zeros_s0 = jnp.zeros((B, H_A, D_HEAD, D_HEAD), jnp.float32)
        zeros_ffn = jnp.zeros((B, FFN_CONV - 1, 2 * D_FF), x_prompt.dtype)
        y_prompt, *st_p = trunk_layer(y_prompt, c_prompt, zeros_conv, zeros_s0, zeros_ffn, fox_prompt, *lw)
        fox_paged = functools.partial(fox_sample, cache_k=cache_k, cache_v=cache_v, cache_logf=cache_logf,
                                      page_table=page_table, layer=layer)
        y_sample, *st_s = trunk_layer(y_sample, c_sample, state_conv_qkv[layer], state_gdn[layer],
                                      state_ffn_conv[layer], fox_paged, *lw)
        rows_p.append(st_p)
        rows_s.append(st_s)
    n_pg = T // PAGE_SIZE
    k_prompt = jnp.stack([r[0].reshape(B, n_pg, PAGE_SIZE, H_B, D_HEAD) for r in rows_p])
    v_prompt = jnp.stack([r[1].reshape(B, n_pg, PAGE_SIZE, H_B, D_HEAD) for r in rows_p])
    logf_prompt = jnp.stack([r[2].reshape(B, n_pg, PAGE_SIZE, H_B) for r in rows_p])
    gdn_prompt = jnp.stack([r[3] for r in rows_p])
    conv_qkv_prompt = jnp.stack([r[4] for r in rows_p])
    ffn_conv_prompt = jnp.stack([r[5] for r in rows_p])
    k_sample = jnp.stack([r[0] for r in rows_s])
    v_sample = jnp.stack([r[1] for r in rows_s])
    logf_sample = jnp.stack([r[2] for r in rows_s])
    gdn_sample = jnp.stack([r[3].astype(state_gdn.dtype) for r in rows_s])
    conv_qkv_sample = jnp.stack([r[4] for r in rows_s])
    ffn_conv_sample = jnp.stack([r[5] for r in rows_s])
    return (y_prompt, y_sample, k_prompt, v_prompt, logf_prompt, gdn_prompt, conv_qkv_prompt, ffn_conv_prompt,
            k_sample, v_sample, logf_sample, gdn_sample, conv_qkv_sample, ffn_conv_sample)
```

```python
import functools

import jax
import jax.numpy as jnp
from jax import lax
from jax.experimental import pallas as pl
from jax.experimental.pallas import tpu as pltpu

EPS = 1e-6
F32 = jnp.float32
BF16 = jnp.bfloat16
HI = lax.Precision.HIGHEST
NEG = -1e30
GDN_CHUNK = 64
GROUP = 8
V7X_VMEM_LIMIT = 56 * 1024 * 1024

NN = (((1,), (0,)), ((), ()))
NT = (((1,), (1,)), ((), ()))
TN = (((0,), (0,)), ((), ()))


def _mm(a, b, dims=NN, precision=None):
    return lax.dot_general(a, b, dims, precision=precision, preferred_element_type=F32)


def _pick(n, target, mult):
    best = None
    for d in range(mult, min(n, target) + 1, mult):
        if n % d == 0:
            best = d
    return best if best is not None else n


def _params(sem):
    return pltpu.CompilerParams(dimension_semantics=sem, vmem_limit_bytes=V7X_VMEM_LIMIT)


def _sigmoid(x):
    return 1.0 / (1.0 + jnp.exp(-x))


def _softplus(x):
    return jnp.maximum(x, 0.0) + jnp.log(1.0 + jnp.exp(-jnp.abs(x)))


def _tri(n, kind):
    r = lax.broadcasted_iota(jnp.int32, (n, n), 0)
    c = lax.broadcasted_iota(jnp.int32, (n, n), 1)
    if kind == "lower_incl":
        return (c <= r).astype(F32)
    if kind == "upper_incl":
        return (r <= c).astype(F32)
    raise ValueError(kind)


def _ada_kernel(c_ref, w_ref, b_ref, o_ref):
    c = c_ref[...]
    o_ref[...] = _mm(c * _sigmoid(c), w_ref[...], precision=HI) + b_ref[...]


def _ada(c_all, w_ada, b_ada):
    m, d = c_all.shape
    n = w_ada.shape[1]
    tn = _pick(n, 1024, 128)
    return pl.pallas_call(
        _ada_kernel,
        grid=(n // tn,),
        in_specs=[pl.BlockSpec((m, d), lambda j: (0, 0)),
                  pl.BlockSpec((d, tn), lambda j: (0, j)),
                  pl.BlockSpec((1, tn), lambda j: (0, j))],
        out_specs=pl.BlockSpec((m, tn), lambda j: (0, j)),
        out_shape=jax.ShapeDtypeStruct((m, n), F32),
        compiler_params=_params(("arbitrary",)),
        name="ada",
    )(c_all, w_ada, b_ada.reshape(1, n))


def _norm_proj_kernel(*refs, with_small):
    x_ref, sc_ref, sh_ref, g_ref, w_ref = refs[:5]
    if with_small:
        wsh_ref, wsl_ref, o_ref, os_ref, h_scr = refs[5:]
    else:
        o_ref, h_scr = refs[5:]

    @pl.when(pl.program_id(1) == 0)
    def _():
        x = x_ref[...]
        y = x * lax.rsqrt(jnp.mean(x * x, axis=-1, keepdims=True) + EPS) * g_ref[...]
        h = y * (1.0 + sc_ref[0]) + sh_ref[0]
        hb = h.astype(BF16)
        h_scr[...] = hb
        if with_small:
            hl = (h - hb.astype(F32)).astype(BF16)
            os_ref[...] = _mm(hb, wsh_ref[...]) + _mm(hb, wsl_ref[...]) + _mm(hl, wsh_ref[...])

    o_ref[...] = _mm(h_scr[...], w_ref[...])


def _norm_proj(x2d, sc, sh, g, w, ws=None, *, tm, tiles_per_b):
    r, d = x2d.shape
    n = w.shape[1]
    tn = _pick(n, 1024, 128)
    mr = sc.shape[1]
    mod_spec = pl.BlockSpec((1, mr, d), lambda i, j: (i // tiles_per_b, 0, 0))
    in_specs = [pl.BlockSpec((tm, d), lambda i, j: (i, 0)), mod_spec, mod_spec,
                pl.BlockSpec((1, d), lambda i, j: (0, 0)),
                pl.BlockSpec((d, tn), lambda i, j: (0, j))]
    args = [x2d, sc, sh, g.reshape(1, d), w]
    out_specs = pl.BlockSpec((tm, tn), lambda i, j: (i, j))
    out_shape = jax.ShapeDtypeStruct((r, n), F32)
    if ws is not None:
        ws_hi, ws_lo = ws
        ns = ws_hi.shape[1]
        in_specs += [pl.BlockSpec((d, ns), lambda i, j: (0, 0))] * 2
        args += [ws_hi, ws_lo]
        out_specs = [out_specs, pl.BlockSpec((tm, ns), lambda i, j: (i, 0))]
        out_shape = [out_shape, jax.ShapeDtypeStruct((r, ns), F32)]
    return pl.pallas_call(
        functools.partial(_norm_proj_kernel, with_small=ws is not None),
        grid=(r // tm, n // tn),
        in_specs=in_specs, out_specs=out_specs, out_shape=out_shape,
        scratch_shapes=[pltpu.VMEM((tm, d), BF16)],
        compiler_params=_params(("arbitrary", "arbitrary")),
        name="norm_proj",
    )(*args)


def _shift_rows(xe, s):
    return pltpu.roll(xe, s, 0)[GROUP:]


def _gdn_kernel(qkv_ref, z_ref, sm_ref, smt_ref, wc_ref, arow_ref, acol_ref, gn_ref, s0_ref,
                o_ref, sout_ref, s_scr, prev_scr, *, heads, chunk, n_chunks, front_pad):
    t = pl.program_id(1)
    tc = chunk * n_chunks
    aw = heads * 128

    @pl.when(t == 0)
    def _():
        s_scr[...] = s0_ref[0]
        prev_scr[...] = jnp.zeros_like(prev_scr)

    row = lax.broadcasted_iota(jnp.int32, (tc, 1), 0)
    col = lax.broadcasted_iota(jnp.int32, (1, tc), 1)
    valid_c = row >= front_pad
    valid_r = col >= front_pad

    sm = sm_ref[0]
    g_tile = jnp.where(valid_c, -jnp.exp(arow_ref[0:1, :]) * _softplus(sm + arow_ref[1:2, :]), 0.0)
    beta_tile = jnp.where(valid_c, _sigmoid(sm), 0.0)
    smt = smt_ref[0]
    gt_all = jnp.where(valid_r, -jnp.exp(acol_ref[:, 0:1]) * _softplus(smt + acol_ref[:, 1:2]), 0.0)

    lo_incl = _tri(chunk, "lower_incl")
    up_incl = _tri(chunk, "upper_incl")
    ri = lax.broadcasted_iota(jnp.int32, (chunk, chunk), 0)
    ci = lax.broadcasted_iota(jnp.int32, (chunk, chunk), 1)
    incl = ci <= ri
    strict = ci < ri
    eye = (ci == ri).astype(F32)
    n_double = max(chunk.bit_length() - 2, 0)

    gcol_tiles = [_mm(lo_incl, g_tile[c * chunk:(c + 1) * chunk], precision=HI) for c in range(n_chunks)]
    grow_tiles = [_mm(gt_all[:, c * chunk:(c + 1) * chunk], up_incl, precision=HI) for c in range(n_chunks)]

    wc = wc_ref[...]
    for h in range(heads):
        parts = []
        for p in range(3):
            lo = p * aw + h * 128
            x = qkv_ref[0, :, lo:lo + 128]
            xe = jnp.concatenate([prev_scr[:, lo:lo + 128], x], axis=0)
            conv = (wc[3:4, lo:lo + 128] * x + wc[2:3, lo:lo + 128] * _shift_rows(xe, 1)
                    + wc[1:2, lo:lo + 128] * _shift_rows(xe, 2) + wc[0:1, lo:lo + 128] * _shift_rows(xe, 3))
            parts.append(conv * _sigmoid(conv))
        q_all, k_all, v_all = parts
        q_all = q_all * lax.rsqrt(jnp.sum(q_all * q_all, axis=-1, keepdims=True) + EPS) * (128.0 ** -0.5)
        k_all = k_all * lax.rsqrt(jnp.sum(k_all * k_all, axis=-1, keepdims=True) + EPS)
        k_all = jnp.where(valid_c, k_all, 0.0)
        s_h = s_scr[h]
        for c in range(n_chunks):
            rs = slice(c * chunk, (c + 1) * chunk)
            q, k, v = q_all[rs], k_all[rs], v_all[rs]
            gcol = gcol_tiles[c][:, heads + h:heads + h + 1]
            grow = grow_tiles[c][heads + h:heads + h + 1, :]
            bcol = beta_tile[rs, h:h + 1]
            decay = jnp.where(incl, jnp.exp(jnp.where(incl, gcol - grow, 0.0)), 0.0)
            kb = k * bcol
            vb = v * bcol
            lower = jnp.where(strict, _mm(kb, k, NT, HI) * decay, 0.0)
            nm = -lower
            tinv = eye + nm
            for _ in range(n_double):
                nm = _mm(nm, nm, precision=HI)
                tinv = tinv + _mm(tinv, nm, precision=HI)
            eg = jnp.exp(gcol)
            u = _mm(tinv, vb, precision=HI)
            w = _mm(tinv, kb * eg, precision=HI)
            v_new = u - _mm(w, s_h, precision=HI)
            qk = jnp.where(incl, _mm(q, k, NT, HI) * decay, 0.0)
            o = _mm(q * eg, s_h, precision=HI) + _mm(qk, v_new, precision=HI)
            g_last = gcol[chunk - 1:chunk, :]
            s_h = s_h * jnp.exp(g_last) + _mm(k * jnp.exp(g_last - gcol), v_new, TN, HI)
            on = o * lax.rsqrt(jnp.mean(o * o, axis=-1, keepdims=True) + EPS) * gn_ref[...]
            z = z_ref[0, rs, h * 128:(h + 1) * 128]
            o_ref[0, rs, h * 128:(h + 1) * 128] = (on * (z * _sigmoid(z))).astype(o_ref.dtype)
        s_scr[h] = s_h

    prev_scr[...] = qkv_ref[0, tc - GROUP:, :]

    @pl.when(t == pl.num_programs(1) - 1)
    def _():
        sout_ref[0] = s_scr[...]


def _gdn(proj3, small3, smallt3, w_conv, a_row, a_col, gn, s0, *, heads, front_pad):
    b, t, _ = proj3.shape
    aw = heads * 128
    chunk = min(GDN_CHUNK, t)
    tc = t if t <= 128 else 128
    n_chunks = tc // chunk
    rows_t = smallt3.shape[1]
    kern = functools.partial(_gdn_kernel, heads=heads, chunk=chunk, n_chunks=n_chunks, front_pad=front_pad)
    return pl.pallas_call(
        kern,
        grid=(b, t // tc),
        in_specs=[pl.BlockSpec((1, tc, 3 * aw), lambda i, j: (i, j, 0)),
                  pl.BlockSpec((1, tc, aw), lambda i, j: (i, j, 3)),
                  pl.BlockSpec((1, tc, 128), lambda i, j: (i, j, 0)),
                  pl.BlockSpec((1, rows_t, tc), lambda i, j: (i, 0, j)),
                  pl.BlockSpec((4, 3 * aw), lambda i, j: (0, 0)),
                  pl.BlockSpec((2, 128), lambda i, j: (0, 0)),
                  pl.BlockSpec((rows_t, 2), lambda i, j: (0, 0)),
                  pl.BlockSpec((1, 128), lambda i, j: (0, 0)),
                  pl.BlockSpec((1, heads, 128, 128), lambda i, j: (i, 0, 0, 0))],
        out_specs=[pl.BlockSpec((1, tc, aw), lambda i, j: (i, j, 0)),
                   pl.BlockSpec((1, heads, 128, 128), lambda i, j: (i, 0, 0, 0))],
        out_shape=[jax.ShapeDtypeStruct((b, t, aw), BF16),
                   jax.ShapeDtypeStruct((b, heads, 128, 128), F32)],
        scratch_shapes=[pltpu.VMEM((heads, 128, 128), F32), pltpu.VMEM((GROUP, 3 * aw), F32)],
        compiler_params=_params(("arbitrary", "arbitrary")),
        name="gdn",
    )(proj3, proj3, small3, smallt3, w_conv, a_row, a_col, gn, s0)


def _fox_prep_kernel(q_ref, k_ref, v_ref, sm_ref, smt_ref, qg_ref, kg_ref, brow_ref, bcol_ref,
                     qn_ref, kn_ref, knb_ref, vb_ref, logf_ref, cc_ref, cr_ref, carry_c, carry_r,
                     *, heads, front_pad):
    t = pl.program_id(1)
    tr = q_ref.shape[1]

    @pl.when(t == 0)
    def _():
        carry_c[...] = jnp.zeros_like(carry_c)
        carry_r[...] = jnp.zeros_like(carry_r)

    for h in range(heads):
        cs = slice(h * 128, (h + 1) * 128)
        q = q_ref[0, :, cs]
        qn = q * lax.rsqrt(jnp.mean(q * q, axis=-1, keepdims=True) + EPS) * qg_ref[...]
        qn_ref[0, :, cs] = (qn * (128.0 ** -0.5)).astype(BF16)
        k = k_ref[0, :, cs]
        kn = k * lax.rsqrt(jnp.mean(k * k, axis=-1, keepdims=True) + EPS) * kg_ref[...]
        kn_ref[0, :, cs] = kn
        knb_ref[0, :, cs] = kn.astype(BF16)
    vb_ref[0] = v_ref[0].astype(BF16)

    row = lax.broadcasted_iota(jnp.int32, (tr, 1), 0)
    col = lax.broadcasted_iota(jnp.int32, (1, tr), 1)
    logf = -_softplus(-(sm_ref[0] + brow_ref[...]))
    logf_ref[0] = logf
    cum = _mm(_tri(tr, "lower_incl"), jnp.where(row >= front_pad, logf, 0.0), precision=HI) + carry_c[...]
    cc_ref[0] = cum
    carry_c[...] = cum[tr - 1:tr, :]
    logft = jnp.where(col >= front_pad, -_softplus(-(smt_ref[0] + bcol_ref[...])), 0.0)
    cumt = _mm(logft, _tri(tr, "upper_incl"), precision=HI) + carry_r[:, 0:1]
    cr_ref[0] = cumt
    carry_r[...] = jnp.broadcast_to(cumt[:, tr - 1:tr], carry_r.shape)


def _fox_prep(proj3, small3, smallt3, qg, kg, brow, bcol, *, heads, front_pad):
    b, t, _ = proj3.shape
    bw = heads * 128
    tr = _pick(t, 256, 128) if t >= 128 else t
    rows_t = smallt3.shape[1]
    wide = lambda blk: pl.BlockSpec((1, tr, bw), lambda i, j, blk=blk: (i, j, blk))
    out_w = pl.BlockSpec((1, tr, bw), lambda i, j: (i, j, 0))
    out_s = pl.BlockSpec((1, tr, 128), lambda i, j: (i, j, 0))
    return pl.pallas_call(
        functools.partial(_fox_prep_kernel, heads=heads, front_pad=front_pad),
        grid=(b, t // tr),
        in_specs=[wide(4), wide(5), wide(6),
                  pl.BlockSpec((1, tr, 128), lambda i, j: (i, j, 0)),
                  pl.BlockSpec((1, rows_t, tr), lambda i, j: (i, 0, j)),
                  pl.BlockSpec((1, 128), lambda i, j: (0, 0)),
                  pl.BlockSpec((1, 128), lambda i, j: (0, 0)),
                  pl.BlockSpec((1, 128), lambda i, j: (0, 0)),
                  pl.BlockSpec((rows_t, 1), lambda i, j: (0, 0))],
        out_specs=[out_w, out_w, out_w, out_w, out_s, out_s,
                   pl.BlockSpec((1, rows_t, tr), lambda i, j: (i, 0, j))],
        out_shape=[jax.ShapeDtypeStruct((b, t, bw), BF16), jax.ShapeDtypeStruct((b, t, bw), F32),
                   jax.ShapeDtypeStruct((b, t, bw), BF16), jax.ShapeDtypeStruct((b, t, bw), BF16),
                   jax.ShapeDtypeStruct((b, t, 128), F32), jax.ShapeDtypeStruct((b, t, 128), F32),
                   jax.ShapeDtypeStruct((b, rows_t, t), F32)],
        scratch_shapes=[pltpu.VMEM((1, 128), F32), pltpu.VMEM((rows_t, 128), F32)],
        compiler_params=_params(("arbitrary", "arbitrary")),
        name="fox_prep",
    )(proj3, proj3, proj3, small3, smallt3, qg, kg, brow, bcol)


def _flash_kernel(q_ref, k_ref, v_ref, cc_ref, cr_ref, gate_ref, o_ref, *, heads, tq):
    h = pl.program_id(1)
    qi = pl.program_id(2)
    q = q_ref[0]
    lane = lax.broadcasted_iota(jnp.int32, (1, 128), 1)
    cq = jnp.sum(jnp.where(lane == 2 * heads + h, cc_ref[0], 0.0), axis=-1, keepdims=True)

    def step(j, carry, masked):
        m, l, acc = carry
        start = pl.multiple_of(j * tq, tq)
        ks = k_ref[0, pl.ds(start, tq), :]
        vs = v_ref[0, pl.ds(start, tq), :]
        ck = cr_ref[0, 0, :, pl.ds(start, tq)]
        s = _mm(q, ks, NT) + (cq - ck)
        if masked:
            ri = lax.broadcasted_iota(jnp.int32, (tq, tq), 0)
            ci = lax.broadcasted_iota(jnp.int32, (tq, tq), 1)
            s = jnp.where(ci <= ri, s, NEG)
        m_new = jnp.maximum(m, jnp.max(s, axis=-1, keepdims=True))
        corr = jnp.exp(m - m_new)
        p = jnp.exp(s - m_new)
        l = l * corr + jnp.sum(p, axis=-1, keepdims=True)
        acc = acc * corr + _mm(p.astype(BF16), vs)
        return m_new, l, acc

    init = (jnp.full((tq, 1), NEG, F32), jnp.zeros((tq, 1), F32), jnp.zeros((tq, 128), F32))
    carry = lax.fori_loop(0, qi, lambda j, c: step(j, c, False), init)
    m, l, acc = step(qi, carry, True)
    g = gate_ref[0]
    o_ref[0] = (acc / l * _sigmoid(g)).astype(o_ref.dtype)


def _flash(qn, knb, vb, cc, cr4, proj3, *, heads):
    b, t, bw = qn.shape
    tq = _pick(t, 512, 128)
    qspec = pl.BlockSpec((1, tq, 128), lambda i, h, j: (i, j, h))
    kvspec = pl.BlockSpec((1, t, 128), lambda i, h, j: (i, 0, h))
    return pl.pallas_call(
        functools.partial(_flash_kernel, heads=heads, tq=tq),
        grid=(b, heads, t // tq),
        in_specs=[qspec, kvspec, kvspec,
                  pl.BlockSpec((1, tq, 128), lambda i, h, j: (i, j, 0)),
                  pl.BlockSpec((1, 1, 1, t), lambda i, h, j: (i, h, 0, 0)),
                  pl.BlockSpec((1, tq, 128), lambda i, h, j: (i, j, 7 * heads + h))],
        out_specs=qspec,
        out_shape=jax.ShapeDtypeStruct((b, t, bw), BF16),
        compiler_params=_params(("arbitrary", "arbitrary", "arbitrary")),
        name="fox_flash",
    )(qn, knb, vb, cc, cr4, proj3)


def _suffix_kernel(lf_ref, o_ref):
    lf = lf_ref[0]
    n = lf.shape[0]
    r = lax.broadcasted_iota(jnp.int32, (n, n), 0)
    c = lax.broadcasted_iota(jnp.int32, (n, n), 1)
    after = (r > c).astype(F32)
    heads = lf.shape[1]
    o_ref[0, 0:heads, :] = _mm(lf, after, TN, HI)
    o_ref[0, heads:2 * heads, :] = _mm(lf, jnp.ones((n, n), F32), TN, HI)


def _suffix(cache_logf3):
    n_pool, page, heads = cache_logf3.shape
    return pl.pallas_call(
        _suffix_kernel,
        grid=(n_pool,),
        in_specs=[pl.BlockSpec((1, page, heads), lambda i: (i, 0, 0))],
        out_specs=pl.BlockSpec((1, 2 * heads, page), lambda i: (i, 0, 0)),
        out_shape=jax.ShapeDtypeStruct((n_pool, 2 * heads, page), F32),
        compiler_params=_params(("arbitrary",)),
        name="page_suffix",
    )(cache_logf3)


def _paged_kernel(pt_ref, *refs, heads, n_new, g_pages):
    del pt_ref
    qn_ref, knb_ref, vb_ref, cr_ref, gate_ref = refs[:5]
    k_refs = refs[5:5 + g_pages]
    v_refs = refs[5 + g_pages:5 + 2 * g_pages]
    s_refs = refs[5 + 2 * g_pages:5 + 3 * g_pages]
    o_ref, qbd_scr, m_scr, l_scr, acc_scr, tail_scr, cn_scr = refs[5 + 3 * g_pages:]
    p = pl.program_id(1)
    bw = heads * 128
    nr = n_new * heads
    first = GROUP - n_new

    @pl.when(p == 0)
    def _():
        lane_head = lax.broadcasted_iota(jnp.int32, (heads, bw), 1) // 128
        head_mask = lane_head == lax.broadcasted_iota(jnp.int32, (heads, bw), 0)
        qn = qn_ref[0].astype(F32)
        qbd = jnp.concatenate(
            [jnp.where(head_mask, jnp.broadcast_to(qn[first + i:first + i + 1, :], (heads, bw)), 0.0)
             for i in range(n_new)], axis=0).astype(BF16)
        qbd_scr[...] = qbd
        cr = cr_ref[0][2 * heads:3 * heads, :]
        cn_col = jnp.concatenate([cr[:, first + i:first + i + 1] for i in range(n_new)], axis=0)
        cn_scr[...] = jnp.broadcast_to(cn_col, cn_scr.shape)
        cn_key = jnp.concatenate([cr] * n_new, axis=0)
        s = _mm(qbd, knb_ref[0], NT) + cn_col - cn_key
        ri = lax.broadcasted_iota(jnp.int32, (nr, GROUP), 0) // heads
        ci = lax.broadcasted_iota(jnp.int32, (nr, GROUP), 1)
        s = jnp.where((ci >= first) & (ci - first <= ri), s, NEG)
        m0 = jnp.max(s, axis=-1, keepdims=True)
        p0 = jnp.exp(s - m0)
        m_scr[...] = jnp.broadcast_to(m0, m_scr.shape)
        l_scr[...] = jnp.broadcast_to(jnp.sum(p0, axis=-1, keepdims=True), l_scr.shape)
        acc_scr[...] = _mm(p0.astype(BF16), vb_ref[0])
        tail_scr[...] = jnp.zeros_like(tail_scr)

    qbd = qbd_scr[...]
    cn = cn_scr[...]
    m = m_scr[...]
    l = l_scr[...]
    acc = acc_scr[...]
    tail = tail_scr[...]
    for i in range(g_pages):
        blk = s_refs[i][0]
        bias8 = blk[0:heads, :] + tail
        tail = tail + blk[heads:2 * heads, :]
        s = _mm(qbd, k_refs[i][0].astype(BF16), NT) + jnp.concatenate([bias8] * n_new, axis=0) + cn
        m_new = jnp.maximum(m, jnp.max(s, axis=-1, keepdims=True))
        corr = jnp.exp(m - m_new)
        pr = jnp.exp(s - m_new)
        l = l * corr + jnp.sum(pr, axis=-1, keepdims=True)
        acc = acc * corr[:, 0:1] + _mm(pr.astype(BF16), v_refs[i][0].astype(BF16))
        m = m_new
    m_scr[...] = m
    l_scr[...] = l
    acc_scr[...] = acc
    tail_scr[...] = tail

    @pl.when(p == pl.num_programs(1) - 1)
    def _():
        lane_head = lax.broadcasted_iota(jnp.int32, (heads, bw), 1) // 128
        head_mask = lane_head == lax.broadcasted_iota(jnp.int32, (heads, bw), 0)
        o = acc / l[:, 0:1]
        rows = [jnp.zeros((first, bw), F32)]
        for i in range(n_new):
            rows.append(jnp.sum(jnp.where(head_mask, o[i * heads:(i + 1) * heads, :], 0.0), axis=0, keepdims=True))
        out = jnp.concatenate(rows, axis=0)
        o_ref[0] = (out * _sigmoid(gate_ref[0])).astype(o_ref.dtype)


def _paged(page_table, qn, knb, vb, cr, proj3, cache_k3, cache_v3, suffix, *, heads, n_new):
    bd, _, bw = qn.shape
    n_pages = page_table.shape[1]
    page = cache_k3.shape[1]
    g_pages = _pick(n_pages, 8, 1)
    rows_t = cr.shape[1]
    nr = n_new * heads
    seq = lambda blk_w, blk: pl.BlockSpec((1, GROUP, blk_w), lambda b, p, pt, blk=blk: (b, 0, blk))

    def page_spec(i, shape):
        return pl.BlockSpec(shape, lambda b, p, pt, i=i: (pt[b, n_pages - 1 - (p * g_pages + i)], 0, 0))

    in_specs = [seq(bw, 0), seq(bw, 0), seq(bw, 0),
                pl.BlockSpec((1, rows_t, GROUP), lambda b, p, pt: (b, 0, 0)),
                seq(bw, 7)]
    in_specs += [page_spec(i, (1, page, bw)) for i in range(g_pages)]
    in_specs += [page_spec(i, (1, page, bw)) for i in range(g_pages)]
    in_specs += [page_spec(i, (1, 2 * heads, page)) for i in range(g_pages)]
    grid_spec = pltpu.PrefetchScalarGridSpec(
        num_scalar_prefetch=1,
        grid=(bd, n_pages // g_pages),
        in_specs=in_specs,
        out_specs=pl.BlockSpec((1, GROUP, bw), lambda b, p, pt: (b, 0, 0)),
        scratch_shapes=[pltpu.VMEM((nr, bw), BF16), pltpu.VMEM((nr, 128), F32), pltpu.VMEM((nr, 128), F32),
                        pltpu.VMEM((nr, bw), F32), pltpu.VMEM((heads, page), F32), pltpu.VMEM((nr, 128), F32)],
    )
    return pl.pallas_call(
        functools.partial(_paged_kernel, heads=heads, n_new=n_new, g_pages=g_pages),
        grid_spec=grid_spec,
        out_shape=jax.ShapeDtypeStruct((bd, GROUP, bw), BF16),
        compiler_params=_params(("arbitrary", "arbitrary")),
        name="fox_paged",
    )(page_table, qn, knb, vb, cr, proj3, *([cache_k3] * g_pages), *([cache_v3] * g_pages), *([suffix] * g_pages))


def _out_proj_kernel(oa_ref, ob_ref, wa_ref, wb_ref, x_ref, gt_ref, g_ref, o_ref):
    mix = _mm(oa_ref[...], wa_ref[...]) + _mm(ob_ref[...], wb_ref[...])
    normed = mix * lax.rsqrt(jnp.mean(mix * mix, axis=-1, keepdims=True) + EPS) * g_ref[...]
    o_ref[...] = x_ref[...] + gt_ref[0] * normed


def _out_proj(oa, ob, wa, wb, x2d, gt, g, *, tm, tiles_per_b):
    r, d = x2d.shape
    aw, bw = oa.shape[1], ob.shape[1]
    mr = gt.shape[1]
    return pl.pallas_call(
        _out_proj_kernel,
        grid=(r // tm,),
        in_specs=[pl.BlockSpec((tm, aw), lambda i: (i, 0)),
                  pl.BlockSpec((tm, bw), lambda i: (i, 0)),
                  pl.BlockSpec((aw, d), lambda i: (0, 0)),
                  pl.BlockSpec((bw, d), lambda i: (0, 0)),
                  pl.BlockSpec((tm, d), lambda i: (i, 0)),
                  pl.BlockSpec((1, mr, d), lambda i: (i // tiles_per_b, 0, 0)),
                  pl.BlockSpec((1, d), lambda i: (0, 0))],
        out_specs=pl.BlockSpec((tm, d), lambda i: (i, 0)),
        out_shape=jax.ShapeDtypeStruct((r, d), F32),
        compiler_params=_params(("arbitrary",)),
        name="out_proj",
    )(oa, ob, wa, wb, x2d, gt, g.reshape(1, d))


def _ffn_tail_kernel(ug_ref, uv_ref, hg_ref, hv_ref, wcg_ref, wcv_ref, bg_ref, bv_ref, wd_ref, x_ref, gt_ref, g_ref,
                     o_ref, acc_scr, *, tiles_per_b):
    i = pl.program_id(0)
    j = pl.program_id(1)
    first = (i % tiles_per_b) == 0

    def conv(u_ref, halo_ref, wc_ref, b_ref):
        x = u_ref[...]
        halo = jnp.where(first, 0.0, halo_ref[...])
        xe = jnp.concatenate([halo, x], axis=0)
        wc = wc_ref[...]
        return wc[2:3] * x + wc[1:2] * _shift_rows(xe, 1) + wc[0:1] * _shift_rows(xe, 2) + b_ref[...]

    gate = conv(ug_ref, hg_ref, wcg_ref, bg_ref)
    val = conv(uv_ref, hv_ref, wcv_ref, bv_ref)
    act = (gate * _sigmoid(gate) * val).astype(BF16)
    part = _mm(act, wd_ref[...])

    @pl.when(j == 0)
    def _():
        acc_scr[...] = part

    @pl.when(j > 0)
    def _():
        acc_scr[...] += part

    @pl.when(j == pl.num_programs(1) - 1)
    def _():
        y = acc_scr[...]
        normed = y * lax.rsqrt(jnp.mean(y * y, axis=-1, keepdims=True) + EPS) * g_ref[...]
        o_ref[...] = x_ref[...] + gt_ref[0] * normed


def _ffn_tail(up, wc, bc, wd, x2d, gt, g, *, tm, tiles_per_b, tf):
    r, d = x2d.shape
    fp = wd.shape[0]
    nf = fp // tf
    hb = tm // GROUP
    mr = gt.shape[1]
    halo = lambda off: pl.BlockSpec((GROUP, tf), lambda i, j, off=off: (jnp.maximum(i * hb - 1, 0), j + off))
    return pl.pallas_call(
        functools.partial(_ffn_tail_kernel, tiles_per_b=tiles_per_b),
        grid=(r // tm, nf),
        in_specs=[pl.BlockSpec((tm, tf), lambda i, j: (i, j)),
                  pl.BlockSpec((tm, tf), lambda i, j: (i, j + nf)),
                  halo(0), halo(nf),
                  pl.BlockSpec((3, tf), lambda i, j: (0, j)),
                  pl.BlockSpec((3, tf), lambda i, j: (0, j + nf)),
                  pl.BlockSpec((1, tf), lambda i, j: (0, j)),
                  pl.BlockSpec((1, tf), lambda i, j: (0, j + nf)),
                  pl.BlockSpec((tf, d), lambda i, j: (j, 0)),
                  pl.BlockSpec((tm, d), lambda i, j: (i, 0)),
                  pl.BlockSpec((1, mr, d), lambda i, j: (i // tiles_per_b, 0, 0)),
                  pl.BlockSpec((1, d), lambda i, j: (0, 0))],
        out_specs=pl.BlockSpec((tm, d), lambda i, j: (i, 0)),
        out_shape=jax.ShapeDtypeStruct((r, d), F32),
        scratch_shapes=[pltpu.VMEM((tm, d), F32)],
        compiler_params=_params(("arbitrary", "arbitrary")),
        name="ffn_tail",
    )(up, up, up, up, wc, wc, bc, bc, wd, x2d, gt, g.reshape(1, d))


def _pad_cols(a, n):
    return jnp.pad(a, [(0, 0)] * (a.ndim - 1) + [(0, n - a.shape[-1])])


def _split_hi_lo(w):
    hi = w.astype(BF16)
    return hi, (w - hi.astype(F32)).astype(BF16)


def _layer(x3, mods, st_conv, st_gdn, st_ffn, fox, lw, *, heads, front_pad, tm):
    (g_pre_mix, g_post_mix, g_pre_ffn, g_post_ffn, w_big, ws, w_conv_qkv, a_row, a_col, gn, qg, kg, brow, bcol,
     w_out_a, w_out_b, w_up, w_conv_ffn, b_conv_ffn, w_down, d_ff, tf) = lw
    sh_m, sc_m, gt_m, sh_f, sc_f, gt_f = mods
    nb, t, d = x3.shape
    aw = heads * 128
    x2d = x3.reshape(nb * t, d)
    tiles_per_b = max(t // tm, 1) if mods[0].shape[1] == 1 else 1

    proj, small = _norm_proj(x2d, sc_m, sh_m, g_pre_mix, w_big, ws, tm=tm, tiles_per_b=tiles_per_b)
    proj3 = proj.reshape(nb, t, 8 * aw)
    if st_conv is not None:
        k = st_conv.shape[1]
        proj3 = lax.dynamic_update_slice(proj3, st_conv, (0, GROUP - (t - front_pad) - k, 0))
    small3 = small.reshape(nb, t, 128)
    rows_t = 3 * GROUP
    smallt3 = jnp.swapaxes(small3[:, :, :rows_t], 1, 2)

    o_a, gdn_new = _gdn(proj3, small3, smallt3, w_conv_qkv, a_row, a_col, gn, st_gdn,
                        heads=heads, front_pad=front_pad)
    qn, kn, knb, vb, logf, cc, cr = _fox_prep(proj3, small3, smallt3, qg, kg, brow, bcol,
                                               heads=heads, front_pad=front_pad)
    o_b = fox(qn, knb, vb, cc, cr, proj3)

    x1 = _out_proj(o_a.reshape(nb * t, aw), o_b.reshape(nb * t, aw), w_out_a, w_out_b, x2d, gt_m, g_post_mix,
                   tm=min(tm, 512), tiles_per_b=max(t // min(tm, 512), 1) if mods[0].shape[1] == 1 else 1)
    up = _norm_proj(x1, sc_f, sh_f, g_pre_ffn, w_up, tm=tm, tiles_per_b=tiles_per_b)
    fp = w_down.shape[0]
    if st_ffn is not None:
        up3 = up.reshape(nb, t, 2 * fp)
        k = st_ffn.shape[1]
        up3 = lax.dynamic_update_slice(up3, st_ffn, (0, GROUP - (t - front_pad) - k, 0))
        up = up3.reshape(nb * t, 2 * fp)
    tmf = min(tm, 512)
    y = _ffn_tail(up, w_conv_ffn, b_conv_ffn, w_down, x1, gt_f, g_post_ffn, tm=tmf,
                  tiles_per_b=max(t // tmf, 1) if mods[0].shape[1] == 1 else 1, tf=tf)
    up3 = up.reshape(nb, t, 2 * fp)
    return y.reshape(nb, t, d), proj3, kn, logf, gdn_new, up3


def kernel(x_prompt, x_sample, cache_k, cache_v, cache_logf, state_gdn, state_conv_qkv, state_ffn_conv, page_table, c_prompt, c_sample, w_ada, b_ada, g_pre_mix, g_post_mix, g_pre_ffn, g_post_ffn, w_in, w_conv_qkv, a_log, dt_bias, g_gdn_norm, q_norm, k_norm, b_forget, w_out, w_up, w_conv_ffn, b_conv_ffn, w_down):
    depth = w_ada.shape[0]
    assert depth == 1, "single-layer step"
    b, t, d = x_prompt.shape
    bd, n_new, _ = x_sample.shape
    heads = state_gdn.shape[2]
    dh = state_gdn.shape[3]
    assert dh == 128 and cache_k.shape[3] == heads and n_new <= GROUP // 2
    aw = heads * dh
    page = cache_k.shape[2]
    n_pool = cache_k.shape[1]
    d_ff = w_down.shape[1]
    conv_a = w_conv_qkv.shape[1]
    ffn_conv = w_conv_ffn.shape[1]
    assert conv_a == 4 and ffn_conv == 3
    layer = 0

    wi = w_in[layer]
    o1 = 4 * aw
    o2 = o1 + 2 * heads
    o3 = o2 + 4 * aw
    w_big = jnp.concatenate([wi[:, :o1], wi[:, o2:o3]], axis=1).astype(BF16)
    w_small = _pad_cols(jnp.concatenate([wi[:, o1:o2], wi[:, o3:]], axis=1), 128)
    ws = _split_hi_lo(w_small)
    zeros_h = jnp.zeros((heads,), F32)
    a_row = _pad_cols(jnp.stack([jnp.concatenate([zeros_h, a_log[layer]]),
                                 jnp.concatenate([zeros_h, dt_bias[layer]])]), 128)
    rows_t = 3 * GROUP
    a_col = jnp.pad(a_row[:, :rows_t].T, ((0, 0), (0, 0)))
    brow = _pad_cols(jnp.concatenate([zeros_h, zeros_h, b_forget[layer]])[None, :], 128)
    bcol = brow[:, :rows_t].T
    assert heads == GROUP and page == 128
    gn = g_gdn_norm[layer].reshape(1, dh)
    qg = q_norm[layer].reshape(1, dh)
    kg = k_norm[layer].reshape(1, dh)
    wo = w_out[layer].astype(BF16)
    w_out_a, w_out_b = wo[:aw], wo[aw:]
    tf = 512
    fp = -(-d_ff // tf) * tf
    wu = w_up[layer]
    w_up_p = jnp.concatenate([_pad_cols(wu[:, :d_ff], fp), _pad_cols(wu[:, d_ff:], fp)], axis=1).astype(BF16)
    wcf = w_conv_ffn[layer]
    w_conv_ffn_p = jnp.concatenate([_pad_cols(wcf[:, :d_ff], fp), _pad_cols(wcf[:, d_ff:], fp)], axis=1)
    bcf = b_conv_ffn[layer][None, :]
    b_conv_ffn_p = jnp.concatenate([_pad_cols(bcf[:, :d_ff], fp), _pad_cols(bcf[:, d_ff:], fp)], axis=1)
    w_down_p = jnp.pad(w_down[layer], ((0, fp - d_ff), (0, 0))).astype(BF16)
    lw = (g_pre_mix[layer], g_post_mix[layer], g_pre_ffn[layer], g_post_ffn[layer], w_big, ws, w_conv_qkv[layer],
          a_row, a_col, gn, qg, kg, brow, bcol, w_out_a, w_out_b, w_up_p, w_conv_ffn_p, b_conv_ffn_p, w_down_p,
          d_ff, tf)

    n_c = b + bd
    c_all = jnp.pad(jnp.concatenate([c_prompt, c_sample], axis=0), ((0, -n_c % GROUP), (0, 0)))
    mod = _ada(c_all, w_ada[layer], b_ada[layer])
    mods_p = [m[:b].reshape(b, 1, d) for m in jnp.split(mod, 6, axis=-1)]
    mods_s = [jnp.repeat(m[b:n_c], GROUP, axis=0).reshape(1, bd * GROUP, d) for m in jnp.split(mod, 6, axis=-1)]

    tm_p = _pick(t, 1024, 128)
    fox_p = lambda qn, knb, vb, cc, cr, proj3: _flash(
        qn, knb, vb, cc, cr[:, 2 * heads:3 * heads].reshape(b, heads, 1, t), proj3, heads=heads)
    zeros_s0 = jnp.zeros((b, heads, dh, dh), F32)
    y_p, proj_p, kn_p, logf_p, gdn_p, up_p = _layer(
        x_prompt, mods_p, None, zeros_s0, None, fox_p, lw, heads=heads, front_pad=0, tm=tm_p)

    front = GROUP - n_new
    x_s = jnp.pad(x_sample, ((0, 0), (front, 0), (0, 0)))
    suffix = _suffix(cache_logf[layer])
    ck3 = cache_k[layer].reshape(n_pool, page, aw)
    cv3 = cache_v[layer].reshape(n_pool, page, aw)
    fox_s = lambda qn, knb, vb, cc, cr, proj3: _paged(
        page_table, qn, knb, vb, cr, proj3, ck3, cv3, suffix, heads=heads, n_new=n_new)
    st_ffn = state_ffn_conv[layer]
    st_ffn_p = jnp.concatenate([_pad_cols(st_ffn[:, :, :d_ff], fp), _pad_cols(st_ffn[:, :, d_ff:], fp)], axis=-1)
    st_conv = _pad_cols(state_conv_qkv[layer], 8 * aw)
    y_s, proj_s, kn_s, logf_s, gdn_s, up_s = _layer(
        x_s, mods_s, st_conv, state_gdn[layer], st_ffn_p, fox_s, lw, heads=heads, front_pad=front, tm=bd * GROUP)

    n_pg = t // page
    unpad = lambda u: jnp.concatenate([u[..., :d_ff], u[..., fp:fp + d_ff]], axis=-1)
    k_prompt = kn_p.reshape(1, b, n_pg, page, heads, dh)
    v_prompt = proj_p[:, :, 6 * aw:7 * aw].reshape(1, b, n_pg, page, heads, dh)
    logf_prompt = logf_p[:, :, 2 * heads:3 * heads].reshape(1, b, n_pg, page, heads)
    conv_qkv_prompt = proj_p[:, t - (conv_a - 1):, :3 * aw][None]
    ffn_conv_prompt = unpad(up_p[:, t - (ffn_conv - 1):, :])[None]
    k_sample = kn_s[:, front:].reshape(1, bd, n_new, heads, dh)
    v_sample = proj_s[:, front:, 6 * aw:7 * aw].reshape(1, bd, n_new, heads, dh)
    logf_sample = logf_s[:, front:, 2 * heads:3 * heads][None]
    conv_qkv_sample = proj_s[:, GROUP - (conv_a - 1):, :3 * aw][None]
    ffn_conv_sample = unpad(up_s[:, GROUP - (ffn_conv - 1):, :])[None]
    return (y_p, y_s[:, front:], k_prompt, v_prompt, logf_prompt, gdn_p[None], conv_qkv_prompt, ffn_conv_prompt,
            k_sample, v_sample, logf_sample, gdn_s.astype(state_gdn.dtype)[None], conv_qkv_sample, ffn_conv_sample)
```

```python
import functools

import jax
import jax.numpy as jnp
from jax import lax
from jax.experimental import pallas as pl
from jax.experimental.pallas import tpu as pltpu

EPS = 1e-6
F32 = jnp.float32
BF16 = jnp.bfloat16
HI = lax.Precision.HIGHEST
NEG = -1e30
GDN_CHUNK = 64
GROUP = 8
V7X_VMEM_LIMIT = 56 * 1024 * 1024

NN = (((1,), (0,)), ((), ()))
NT = (((1,), (1,)), ((), ()))
TN = (((0,), (0,)), ((), ()))


def _mm(a, b, dims=NN, precision=None):
    return lax.dot_general(a, b, dims, precision=precision, preferred_element_type=F32)


BNN = (((2,), (1,)), ((0,), (0,)))
BNT = (((2,), (2,)), ((0,), (0,)))
BTN = (((1,), (1,)), ((0,), (0,)))


def _bmm(a, b, dims=BNN):
    return lax.dot_general(a, b, dims, preferred_element_type=F32)


def _bmm1(a, b, dims):
    return _bmm(a.astype(BF16), b.astype(BF16), dims)


def _bmm3(a, b, dims):
    free = 2 if dims == BTN else 1
    m = a.shape[free]
    ah = a.astype(BF16).astype(F32)
    bh = b.astype(BF16)
    bl = (b - bh.astype(F32)).astype(BF16)
    stack = jnp.concatenate([ah, a - ah], axis=free).astype(BF16)
    r = _bmm(stack, bh, dims)
    r2 = _bmm(lax.slice_in_dim(stack, 0, m, axis=free), bl, dims)
    return lax.slice_in_dim(r, 0, m, axis=1) + lax.slice_in_dim(r, m, 2 * m, axis=1) + r2


_P_AQ = _P_INV = _P_MRG = _P_UW = _P_WS = _P_O = _P_S = _bmm1


def _pick(n, target, mult):
    best = None
    for d in range(mult, min(n, target) + 1, mult):
        if n % d == 0:
            best = d
    return best if best is not None else n


def _params(sem):
    return pltpu.CompilerParams(dimension_semantics=sem, vmem_limit_bytes=V7X_VMEM_LIMIT)


def _sigmoid(x):
    return 1.0 / (1.0 + jnp.exp(-x))


def _softplus(x):
    return jnp.maximum(x, 0.0) + jnp.log(1.0 + jnp.exp(-jnp.abs(x)))


def _tri(n, kind):
    r = lax.broadcasted_iota(jnp.int32, (n, n), 0)
    c = lax.broadcasted_iota(jnp.int32, (n, n), 1)
    if kind == "lower_incl":
        return (c <= r).astype(F32)
    if kind == "upper_incl":
        return (r <= c).astype(F32)
    raise ValueError(kind)


def _ada_kernel(c_ref, w_ref, b_ref, o_ref):
    c = c_ref[...]
    o_ref[...] = _mm(c * _sigmoid(c), w_ref[...], precision=HI) + b_ref[...]


def _ada(c_all, w_ada, b_ada):
    m, d = c_all.shape
    n = w_ada.shape[1]
    tn = _pick(n, 1024, 128)
    return pl.pallas_call(
        _ada_kernel,
        grid=(n // tn,),
        in_specs=[pl.BlockSpec((m, d), lambda j: (0, 0)),
                  pl.BlockSpec((d, tn), lambda j: (0, j)),
                  pl.BlockSpec((1, tn), lambda j: (0, j))],
        out_specs=pl.BlockSpec((m, tn), lambda j: (0, j)),
        out_shape=jax.ShapeDtypeStruct((m, n), F32),
        compiler_params=_params(("arbitrary",)),
        name="ada",
    )(c_all, w_ada, b_ada.reshape(1, n))


def _norm_proj_kernel(*refs, with_small):
    x_ref, sc_ref, sh_ref, g_ref, w_ref = refs[:5]
    if with_small:
        wsh_ref, wsl_ref, o_ref, os_ref, h_scr = refs[5:]
    else:
        o_ref, h_scr = refs[5:]

    @pl.when(pl.program_id(1) == 0)
    def _():
        x = x_ref[...]
        y = x * lax.rsqrt(jnp.mean(x * x, axis=-1, keepdims=True) + EPS) * g_ref[...]
        h = y * (1.0 + sc_ref[0]) + sh_ref[0]
        hb = h.astype(BF16)
        h_scr[...] = hb
        if with_small:
            hl = (h - hb.astype(F32)).astype(BF16)
            os_ref[...] = _mm(hb, wsh_ref[...]) + _mm(hb, wsl_ref[...]) + _mm(hl, wsh_ref[...])

    o_ref[...] = _mm(h_scr[...], w_ref[...])


def _norm_proj(x2d, sc, sh, g, w, ws=None, *, tm, tiles_per_b):
    r, d = x2d.shape
    n = w.shape[1]
    tn = _pick(n, 1024, 128)
    mr = sc.shape[1]
    mod_spec = pl.BlockSpec((1, mr, d), lambda i, j: (i // tiles_per_b, 0, 0))
    in_specs = [pl.BlockSpec((tm, d), lambda i, j: (i, 0)), mod_spec, mod_spec,
                pl.BlockSpec((1, d), lambda i, j: (0, 0)),
                pl.BlockSpec((d, tn), lambda i, j: (0, j))]
    args = [x2d, sc, sh, g.reshape(1, d), w]
    out_specs = pl.BlockSpec((tm, tn), lambda i, j: (i, j))
    out_shape = jax.ShapeDtypeStruct((r, n), F32)
    if ws is not None:
        ws_hi, ws_lo = ws
        ns = ws_hi.shape[1]
        in_specs += [pl.BlockSpec((d, ns), lambda i, j: (0, 0))] * 2
        args += [ws_hi, ws_lo]
        out_specs = [out_specs, pl.BlockSpec((tm, ns), lambda i, j: (i, 0))]
        out_shape = [out_shape, jax.ShapeDtypeStruct((r, ns), F32)]
    return pl.pallas_call(
        functools.partial(_norm_proj_kernel, with_small=ws is not None),
        grid=(r // tm, n // tn),
        in_specs=in_specs, out_specs=out_specs, out_shape=out_shape,
        scratch_shapes=[pltpu.VMEM((tm, d), BF16)],
        compiler_params=_params(("arbitrary", "arbitrary")),
        name="norm_proj",
    )(*args)


def _shift_rows(xe, s):
    return pltpu.roll(xe, s, 0)[GROUP:]


def _gdn_kernel(qkv_ref, z_ref, sm_ref, smt_ref, wc_ref, arow_ref, acol_ref, gn_ref, s0_ref,
                o_ref, sout_ref, s_scr, prev_scr, *, heads, chunk, n_chunks, front_pad):
    t = pl.program_id(1)
    tc = chunk * n_chunks
    aw = heads * 128

    @pl.when(t == 0)
    def _():
        s_scr[...] = s0_ref[0]
        prev_scr[...] = jnp.zeros_like(prev_scr)

    row = lax.broadcasted_iota(jnp.int32, (tc, 1), 0)
    col = lax.broadcasted_iota(jnp.int32, (1, tc), 1)
    valid_c = row >= front_pad
    valid_r = col >= front_pad

    sm = sm_ref[0]
    g_tile = jnp.where(valid_c, -jnp.exp(arow_ref[0:1, :]) * _softplus(sm + arow_ref[1:2, :]), 0.0)
    beta_tile = jnp.where(valid_c, _sigmoid(sm), 0.0)
    smt = smt_ref[0]
    gt_all = jnp.where(valid_r, -jnp.exp(acol_ref[:, 0:1]) * _softplus(smt + acol_ref[:, 1:2]), 0.0)

    lo_incl = _tri(chunk, "lower_incl")
    up_incl = _tri(chunk, "upper_incl")
    ri = lax.broadcasted_iota(jnp.int32, (chunk, chunk), 0)
    ci = lax.broadcasted_iota(jnp.int32, (chunk, chunk), 1)
    incl = ci <= ri
    strict = ci < ri
    eye = (ci == ri).astype(F32)

    gcol_tiles = [_mm(lo_incl, g_tile[c * chunk:(c + 1) * chunk], precision=HI) for c in range(n_chunks)]
    grow_tiles = [_mm(gt_all[:, c * chunk:(c + 1) * chunk], up_incl, precision=HI) for c in range(n_chunks)]

    wc = wc_ref[...]
    per_head = []
    for h in range(heads):
        parts = []
        for p in range(3):
            lo = p * aw + h * 128
            x = qkv_ref[0, :, lo:lo + 128]
            xe = jnp.concatenate([prev_scr[:, lo:lo + 128], x], axis=0)
            conv = (wc[3:4, lo:lo + 128] * x + wc[2:3, lo:lo + 128] * _shift_rows(xe, 1)
                    + wc[1:2, lo:lo + 128] * _shift_rows(xe, 2) + wc[0:1, lo:lo + 128] * _shift_rows(xe, 3))
            parts.append(conv * _sigmoid(conv))
        q_all, k_all, v_all = parts
        q_all = q_all * lax.rsqrt(jnp.sum(q_all * q_all, axis=-1, keepdims=True) + EPS) * (128.0 ** -0.5)
        k_all = k_all * lax.rsqrt(jnp.sum(k_all * k_all, axis=-1, keepdims=True) + EPS)
        per_head.append((q_all, jnp.where(valid_c, k_all, 0.0), v_all))

    units = [(c, h) for c in range(n_chunks) for h in range(heads)]
    rows = lambda c: slice(c * chunk, (c + 1) * chunk)
    q = jnp.stack([per_head[h][0][rows(c)] for c, h in units])
    k = jnp.stack([per_head[h][1][rows(c)] for c, h in units])
    v = jnp.stack([per_head[h][2][rows(c)] for c, h in units])
    gcol = jnp.stack([gcol_tiles[c][:, heads + h:heads + h + 1] for c, h in units])
    grow = jnp.stack([grow_tiles[c][heads + h:heads + h + 1, :] for c, h in units])
    bcol = jnp.stack([beta_tile[rows(c), h:h + 1] for c, h in units])

    decay = jnp.where(incl, jnp.exp(jnp.where(incl, gcol - grow, 0.0)), 0.0)
    kb = k * bcol
    vb = v * bcol
    aq = _P_AQ(jnp.concatenate([kb, q], axis=1), k, BNT)
    lower = jnp.where(strict, aq[:, :chunk] * decay, 0.0)
    qk = jnp.where(incl, aq[:, chunk:] * decay, 0.0)
    base = min(GROUP, chunk)
    same_blk = lambda s: (ri >> (s.bit_length() - 1)) == (ci >> (s.bit_length() - 1))
    neg_bd = jnp.where(same_blk(base), -lower, 0.0)
    nm = _P_INV(neg_bd, neg_bd, BNN)
    tinv = eye + neg_bd
    n_base = base.bit_length() - 2
    for lvl in range(n_base):
        if lvl < n_base - 1:
            r = _P_INV(jnp.concatenate([nm, tinv], axis=1), nm, BNN)
            tinv = tinv + r[:, chunk:]
            nm = r[:, :chunk]
        else:
            tinv = tinv + _P_INV(tinv, nm, BNN)
    s = base
    while s < chunk:
        off = jnp.where(same_blk(2 * s) & jnp.logical_not(same_blk(s)), lower, 0.0)
        tinv = tinv - _P_MRG(tinv, _P_MRG(off, tinv, BNN), BNN)
        s *= 2
    eg = jnp.exp(gcol)
    uw = _P_UW(tinv, jnp.concatenate([vb, kb * eg], axis=2), BNN)
    wq = jnp.concatenate([uw[:, :, 128:], q * eg], axis=1)
    g_last = gcol[:, chunk - 1:chunk, :]
    kd = k * jnp.exp(g_last - gcol)
    e_last = jnp.exp(g_last)

    s_all = s_scr[...]
    for c in range(n_chunks):
        us = slice(c * heads, (c + 1) * heads)
        ws = _P_WS(wq[us], s_all, BNN)
        v_new = uw[us, :, :128] - ws[:, :chunk]
        o = ws[:, chunk:] + _P_O(qk[us], v_new, BNN)
        s_all = s_all * e_last[us] + _P_S(kd[us], v_new, BTN)
        on = o * lax.rsqrt(jnp.mean(o * o, axis=-1, keepdims=True) + EPS) * gn_ref[...]
        for h in range(heads):
            z = z_ref[0, rows(c), h * 128:(h + 1) * 128]
            o_ref[0, rows(c), h * 128:(h + 1) * 128] = (on[h] * (z * _sigmoid(z))).astype(o_ref.dtype)
    s_scr[...] = s_all

    prev_scr[...] = qkv_ref[0, tc - GROUP:, :]

    @pl.when(t == pl.num_programs(1) - 1)
    def _():
        sout_ref[0] = s_scr[...]


def _gdn(proj3, small3, smallt3, w_conv, a_row, a_col, gn, s0, *, heads, front_pad):
    b, t, _ = proj3.shape
    aw = heads * 128
    chunk = min(GDN_CHUNK, t)
    tc = t if t <= 128 else 128
    n_chunks = tc // chunk
    rows_t = smallt3.shape[1]
    kern = functools.partial(_gdn_kernel, heads=heads, chunk=chunk, n_chunks=n_chunks, front_pad=front_pad)
    return pl.pallas_call(
        kern,
        grid=(b, t // tc),
        in_specs=[pl.BlockSpec((1, tc, 3 * aw), lambda i, j: (i, j, 0)),
                  pl.BlockSpec((1, tc, aw), lambda i, j: (i, j, 3)),
                  pl.BlockSpec((1, tc, 128), lambda i, j: (i, j, 0)),
                  pl.BlockSpec((1, rows_t, tc), lambda i, j: (i, 0, j)),
                  pl.BlockSpec((4, 3 * aw), lambda i, j: (0, 0)),
                  pl.BlockSpec((2, 128), lambda i, j: (0, 0)),
                  pl.BlockSpec((rows_t, 2), lambda i, j: (0, 0)),
                  pl.BlockSpec((1, 128), lambda i, j: (0, 0)),
                  pl.BlockSpec((1, heads, 128, 128), lambda i, j: (i, 0, 0, 0))],
        out_specs=[pl.BlockSpec((1, tc, aw), lambda i, j: (i, j, 0)),
                   pl.BlockSpec((1, heads, 128, 128), lambda i, j: (i, 0, 0, 0))],
        out_shape=[jax.ShapeDtypeStruct((b, t, aw), BF16),
                   jax.ShapeDtypeStruct((b, heads, 128, 128), F32)],
        scratch_shapes=[pltpu.VMEM((heads, 128, 128), F32), pltpu.VMEM((GROUP, 3 * aw), F32)],
        compiler_params=_params(("arbitrary", "arbitrary")),
        name="gdn",
    )(proj3, proj3, small3, smallt3, w_conv, a_row, a_col, gn, s0)


def _fox_prep_kernel(q_ref, k_ref, v_ref, sm_ref, smt_ref, qg_ref, kg_ref, brow_ref, bcol_ref,
                     qn_ref, kn_ref, knb_ref, vb_ref, logf_ref, cc_ref, cr_ref, carry_c, carry_r,
                     *, heads, front_pad):
    t = pl.program_id(1)
    tr = q_ref.shape[1]

    @pl.when(t == 0)
    def _():
        carry_c[...] = jnp.zeros_like(carry_c)
        carry_r[...] = jnp.zeros_like(carry_r)

    for h in range(heads):
        cs = slice(h * 128, (h + 1) * 128)
        q = q_ref[0, :, cs]
        qn = q * lax.rsqrt(jnp.mean(q * q, axis=-1, keepdims=True) + EPS) * qg_ref[...]
        qn_ref[0, :, cs] = (qn * (128.0 ** -0.5)).astype(BF16)
        k = k_ref[0, :, cs]
        kn = k * lax.rsqrt(jnp.mean(k * k, axis=-1, keepdims=True) + EPS) * kg_ref[...]
        kn_ref[0, :, cs] = kn
        knb_ref[0, :, cs] = kn.astype(BF16)
    vb_ref[0] = v_ref[0].astype(BF16)

    row = lax.broadcasted_iota(jnp.int32, (tr, 1), 0)
    col = lax.broadcasted_iota(jnp.int32, (1, tr), 1)
    logf = -_softplus(-(sm_ref[0] + brow_ref[...]))
    logf_ref[0] = logf
    cum = _mm(_tri(tr, "lower_incl"), jnp.where(row >= front_pad, logf, 0.0), precision=HI) + carry_c[...]
    cc_ref[0] = cum
    carry_c[...] = cum[tr - 1:tr, :]
    logft = jnp.where(col >= front_pad, -_softplus(-(smt_ref[0] + bcol_ref[...])), 0.0)
    cumt = _mm(logft, _tri(tr, "upper_incl"), precision=HI) + carry_r[:, 0:1]
    cr_ref[0] = cumt
    carry_r[...] = jnp.broadcast_to(cumt[:, tr - 1:tr], carry_r.shape)


def _fox_prep(proj3, small3, smallt3, qg, kg, brow, bcol, *, heads, front_pad):
    b, t, _ = proj3.shape
    bw = heads * 128
    tr = _pick(t, 256, 128) if t >= 128 else t
    rows_t = smallt3.shape[1]
    wide = lambda blk: pl.BlockSpec((1, tr, bw), lambda i, j, blk=blk: (i, j, blk))
    out_w = pl.BlockSpec((1, tr, bw), lambda i, j: (i, j, 0))
    out_s = pl.BlockSpec((1, tr, 128), lambda i, j: (i, j, 0))
    return pl.pallas_call(
        functools.partial(_fox_prep_kernel, heads=heads, front_pad=front_pad),
        grid=(b, t // tr),
        in_specs=[wide(4), wide(5), wide(6),
                  pl.BlockSpec((1, tr, 128), lambda i, j: (i, j, 0)),
                  pl.BlockSpec((1, rows_t, tr), lambda i, j: (i, 0, j)),
                  pl.BlockSpec((1, 128), lambda i, j: (0, 0)),
                  pl.BlockSpec((1, 128), lambda i, j: (0, 0)),
                  pl.BlockSpec((1, 128), lambda i, j: (0, 0)),
                  pl.BlockSpec((rows_t, 1), lambda i, j: (0, 0))],
        out_specs=[out_w, out_w, out_w, out_w, out_s, out_s,
                   pl.BlockSpec((1, rows_t, tr), lambda i, j: (i, 0, j))],
        out_shape=[jax.ShapeDtypeStruct((b, t, bw), BF16), jax.ShapeDtypeStruct((b, t, bw), F32),
                   jax.ShapeDtypeStruct((b, t, bw), BF16), jax.ShapeDtypeStruct((b, t, bw), BF16),
                   jax.ShapeDtypeStruct((b, t, 128), F32), jax.ShapeDtypeStruct((b, t, 128), F32),
                   jax.ShapeDtypeStruct((b, rows_t, t), F32)],
        scratch_shapes=[pltpu.VMEM((1, 128), F32), pltpu.VMEM((rows_t, 128), F32)],
        compiler_params=_params(("arbitrary", "arbitrary")),
        name="fox_prep",
    )(proj3, proj3, proj3, small3, smallt3, qg, kg, brow, bcol)


def _flash_kernel(q_ref, k_ref, v_ref, cc_ref, cr_ref, gate_ref, o_ref, *, heads, tq):
    h = pl.program_id(1)
    qi = pl.program_id(2)
    q = q_ref[0]
    lane = lax.broadcasted_iota(jnp.int32, (1, 128), 1)
    cq = jnp.sum(jnp.where(lane == 2 * heads + h, cc_ref[0], 0.0), axis=-1, keepdims=True)

    def step(j, carry, masked):
        m, l, acc = carry
        start = pl.multiple_of(j * tq, tq)
        ks = k_ref[0, pl.ds(start, tq), :]
        vs = v_ref[0, pl.ds(start, tq), :]
        ck = cr_ref[0, 0, :, pl.ds(start, tq)]
        s = _mm(q, ks, NT) + (cq - ck)
        if masked:
            ri = lax.broadcasted_iota(jnp.int32, (tq, tq), 0)
            ci = lax.broadcasted_iota(jnp.int32, (tq, tq), 1)
            s = jnp.where(ci <= ri, s, NEG)
        m_new = jnp.maximum(m, jnp.max(s, axis=-1, keepdims=True))
        corr = jnp.exp(m - m_new)
        p = jnp.exp(s - m_new)
        l = l * corr + jnp.sum(p, axis=-1, keepdims=True)
        acc = acc * corr + _mm(p.astype(BF16), vs)
        return m_new, l, acc

    init = (jnp.full((tq, 1), NEG, F32), jnp.zeros((tq, 1), F32), jnp.zeros((tq, 128), F32))
    carry = lax.fori_loop(0, qi, lambda j, c: step(j, c, False), init)
    m, l, acc = step(qi, carry, True)
    g = gate_ref[0]
    o_ref[0] = (acc / l * _sigmoid(g)).astype(o_ref.dtype)


def _flash(qn, knb, vb, cc, cr4, proj3, *, heads):
    b, t, bw = qn.shape
    tq = _pick(t, 512, 128)
    qspec = pl.BlockSpec((1, tq, 128), lambda i, h, j: (i, j, h))
    kvspec = pl.BlockSpec((1, t, 128), lambda i, h, j: (i, 0, h))
    return pl.pallas_call(
        functools.partial(_flash_kernel, heads=heads, tq=tq),
        grid=(b, heads, t // tq),
        in_specs=[qspec, kvspec, kvspec,
                  pl.BlockSpec((1, tq, 128), lambda i, h, j: (i, j, 0)),
                  pl.BlockSpec((1, 1, 1, t), lambda i, h, j: (i, h, 0, 0)),
                  pl.BlockSpec((1, tq, 128), lambda i, h, j: (i, j, 7 * heads + h))],
        out_specs=qspec,
        out_shape=jax.ShapeDtypeStruct((b, t, bw), BF16),
        compiler_params=_params(("arbitrary", "arbitrary", "arbitrary")),
        name="fox_flash",
    )(qn, knb, vb, cc, cr4, proj3)


def _suffix_kernel(lf_ref, o_ref, *, heads):
    lf = lf_ref[...]
    n = lf.shape[1]
    gp = lf.shape[0] // heads
    r = lax.broadcasted_iota(jnp.int32, (n, n), 0)
    c = lax.broadcasted_iota(jnp.int32, (n, n), 1)
    rhs = jnp.concatenate([(r > c).astype(F32), jnp.ones((n, n), F32)], axis=1)
    res = _mm(lf, rhs, precision=HI)
    o_ref[:, 0:heads, :] = res[:, :n].reshape(gp, heads, n)
    o_ref[:, heads:2 * heads, :] = res[:, n:].reshape(gp, heads, n)


def _suffix(logf_t):
    n_pool, heads, page = logf_t.shape
    gp = _pick(n_pool, 128, 1)
    return pl.pallas_call(
        functools.partial(_suffix_kernel, heads=heads),
        grid=(n_pool // gp,),
        in_specs=[pl.BlockSpec((gp * heads, page), lambda i: (i, 0))],
        out_specs=pl.BlockSpec((gp, 2 * heads, page), lambda i: (i, 0, 0)),
        out_shape=jax.ShapeDtypeStruct((n_pool, 2 * heads, page), F32),
        compiler_params=_params(("arbitrary",)),
        name="page_suffix",
    )(logf_t.reshape(n_pool * heads, page))


def _paged_kernel(pt_ref, *refs, heads, n_new, g_pages):
    del pt_ref
    qn_ref, knb_ref, vb_ref, cr_ref, gate_ref = refs[:5]
    k_refs = refs[5:5 + g_pages]
    v_refs = refs[5 + g_pages:5 + 2 * g_pages]
    s_refs = refs[5 + 2 * g_pages:5 + 3 * g_pages]
    o_ref, qbd_scr, m_scr, l_scr, acc_scr, tail_scr, cn_scr = refs[5 + 3 * g_pages:]
    p = pl.program_id(1)
    bw = heads * 128
    nr = n_new * heads
    first = GROUP - n_new

    @pl.when(p == 0)
    def _():
        lane_head = lax.broadcasted_iota(jnp.int32, (heads, bw), 1) // 128
        head_mask = lane_head == lax.broadcasted_iota(jnp.int32, (heads, bw), 0)
        qn = qn_ref[0].astype(F32)
        qbd = jnp.concatenate(
            [jnp.where(head_mask, jnp.broadcast_to(qn[first + i:first + i + 1, :], (heads, bw)), 0.0)
             for i in range(n_new)], axis=0).astype(BF16)
        qbd_scr[...] = qbd
        cr = cr_ref[0][2 * heads:3 * heads, :]
        cn_col = jnp.concatenate([cr[:, first + i:first + i + 1] for i in range(n_new)], axis=0)
        cn_scr[...] = jnp.broadcast_to(cn_col, cn_scr.shape)
        cn_key = jnp.concatenate([cr] * n_new, axis=0)
        s = _mm(qbd, knb_ref[0], NT) + cn_col - cn_key
        ri = lax.broadcasted_iota(jnp.int32, (nr, GROUP), 0) // heads
        ci = lax.broadcasted_iota(jnp.int32, (nr, GROUP), 1)
        s = jnp.where((ci >= first) & (ci - first <= ri), s, NEG)
        m0 = jnp.max(s, axis=-1, keepdims=True)
        p0 = jnp.exp(s - m0)
        m_scr[...] = jnp.broadcast_to(m0, m_scr.shape)
        l_scr[...] = jnp.broadcast_to(jnp.sum(p0, axis=-1, keepdims=True), l_scr.shape)
        acc_scr[...] = _mm(p0.astype(BF16), vb_ref[0])
        tail_scr[...] = jnp.zeros_like(tail_scr)

    qbd = qbd_scr[...]
    cn = cn_scr[...]
    m = m_scr[...]
    l = l_scr[...]
    acc = acc_scr[...]
    tail = tail_scr[...]
    scores = []
    for i in range(g_pages):
        blk = s_refs[i][0]
        bias8 = blk[0:heads, :] + tail
        tail = tail + blk[heads:2 * heads, :]
        scores.append(_mm(qbd, k_refs[i][0].astype(BF16), NT) + jnp.concatenate([bias8] * n_new, axis=0) + cn)
    s_max = scores[0]
    for s in scores[1:]:
        s_max = jnp.maximum(s_max, s)
    m_new = jnp.maximum(m, jnp.max(s_max, axis=-1, keepdims=True))
    corr = jnp.exp(m - m_new)
    probs = [jnp.exp(s - m_new) for s in scores]
    p_sum = probs[0]
    for pr in probs[1:]:
        p_sum = p_sum + pr
    pv = _mm(probs[0].astype(BF16), v_refs[0][0].astype(BF16))
    for i in range(1, g_pages):
        pv = pv + _mm(probs[i].astype(BF16), v_refs[i][0].astype(BF16))
    l = l * corr + jnp.sum(p_sum, axis=-1, keepdims=True)
    acc = acc * corr[:, 0:1] + pv
    m_scr[...] = m_new
    l_scr[...] = l
    acc_scr[...] = acc
    tail_scr[...] = tail

    @pl.when(p == pl.num_programs(1) - 1)
    def _():
        lane_head = lax.broadcasted_iota(jnp.int32, (heads, bw), 1) // 128
        head_mask = lane_head == lax.broadcasted_iota(jnp.int32, (heads, bw), 0)
        o = acc / l[:, 0:1]
        rows = [jnp.zeros((first, bw), F32)]
        for i in range(n_new):
            rows.append(jnp.sum(jnp.where(head_mask, o[i * heads:(i + 1) * heads, :], 0.0), axis=0, keepdims=True))
        out = jnp.concatenate(rows, axis=0)
        o_ref[0] = (out * _sigmoid(gate_ref[0])).astype(o_ref.dtype)


def _paged(page_table, qn, knb, vb, cr, proj3, cache_k3, cache_v3, suffix, *, heads, n_new):
    bd, _, bw = qn.shape
    n_pages = page_table.shape[1]
    page = cache_k3.shape[1]
    g_pages = _pick(n_pages, 8, 1)
    rows_t = cr.shape[1]
    nr = n_new * heads
    seq = lambda blk_w, blk: pl.BlockSpec((1, GROUP, blk_w), lambda b, p, pt, blk=blk: (b, 0, blk))

    def page_spec(i, shape):
        return pl.BlockSpec(shape, lambda b, p, pt, i=i: (pt[b, n_pages - 1 - (p * g_pages + i)], 0, 0))

    in_specs = [seq(bw, 0), seq(bw, 0), seq(bw, 0),
                pl.BlockSpec((1, rows_t, GROUP), lambda b, p, pt: (b, 0, 0)),
                seq(bw, 7)]
    in_specs += [page_spec(i, (1, page, bw)) for i in range(g_pages)]
    in_specs += [page_spec(i, (1, page, bw)) for i in range(g_pages)]
    in_specs += [page_spec(i, (1, 2 * heads, page)) for i in range(g_pages)]
    grid_spec = pltpu.PrefetchScalarGridSpec(
        num_scalar_prefetch=1,
        grid=(bd, n_pages // g_pages),
        in_specs=in_specs,
        out_specs=pl.BlockSpec((1, GROUP, bw), lambda b, p, pt: (b, 0, 0)),
        scratch_shapes=[pltpu.VMEM((nr, bw), BF16), pltpu.VMEM((nr, 128), F32), pltpu.VMEM((nr, 128), F32),
                        pltpu.VMEM((nr, bw), F32), pltpu.VMEM((heads, page), F32), pltpu.VMEM((nr, 128), F32)],
    )
    return pl.pallas_call(
        functools.partial(_paged_kernel, heads=heads, n_new=n_new, g_pages=g_pages),
        grid_spec=grid_spec,
        out_shape=jax.ShapeDtypeStruct((bd, GROUP, bw), BF16),
        compiler_params=_params(("arbitrary", "arbitrary")),
        name="fox_paged",
    )(page_table, qn, knb, vb, cr, proj3, *([cache_k3] * g_pages), *([cache_v3] * g_pages), *([suffix] * g_pages))


def _out_proj_kernel(oa_ref, ob_ref, wa_ref, wb_ref, x_ref, gt_ref, g_ref, o_ref):
    mix = _mm(oa_ref[...], wa_ref[...]) + _mm(ob_ref[...], wb_ref[...])
    normed = mix * lax.rsqrt(jnp.mean(mix * mix, axis=-1, keepdims=True) + EPS) * g_ref[...]
    o_ref[...] = x_ref[...] + gt_ref[0] * normed


def _out_proj(oa, ob, wa, wb, x2d, gt, g, *, tm, tiles_per_b):
    r, d = x2d.shape
    aw, bw = oa.shape[1], ob.shape[1]
    mr = gt.shape[1]
    return pl.pallas_call(
        _out_proj_kernel,
        grid=(r // tm,),
        in_specs=[pl.BlockSpec((tm, aw), lambda i: (i, 0)),
                  pl.BlockSpec((tm, bw), lambda i: (i, 0)),
                  pl.BlockSpec((aw, d), lambda i: (0, 0)),
                  pl.BlockSpec((bw, d), lambda i: (0, 0)),
                  pl.BlockSpec((tm, d), lambda i: (i, 0)),
                  pl.BlockSpec((1, mr, d), lambda i: (i // tiles_per_b, 0, 0)),
                  pl.BlockSpec((1, d), lambda i: (0, 0))],
        out_specs=pl.BlockSpec((tm, d), lambda i: (i, 0)),
        out_shape=jax.ShapeDtypeStruct((r, d), F32),
        compiler_params=_params(("arbitrary",)),
        name="out_proj",
    )(oa, ob, wa, wb, x2d, gt, g.reshape(1, d))


def _ffn_tail_kernel(ug_ref, uv_ref, hg_ref, hv_ref, wcg_ref, wcv_ref, bg_ref, bv_ref, wd_ref, x_ref, gt_ref, g_ref,
                     o_ref, acc_scr, *, tiles_per_b):
    i = pl.program_id(0)
    j = pl.program_id(1)
    first = (i % tiles_per_b) == 0

    def conv(u_ref, halo_ref, wc_ref, b_ref):
        x = u_ref[...]
        halo = jnp.where(first, 0.0, halo_ref[...])
        xe = jnp.concatenate([halo, x], axis=0)
        wc = wc_ref[...]
        return wc[2:3] * x + wc[1:2] * _shift_rows(xe, 1) + wc[0:1] * _shift_rows(xe, 2) + b_ref[...]

    gate = conv(ug_ref, hg_ref, wcg_ref, bg_ref)
    val = conv(uv_ref, hv_ref, wcv_ref, bv_ref)
    act = (gate * _sigmoid(gate) * val).astype(BF16)
    part = _mm(act, wd_ref[...])

    @pl.when(j == 0)
    def _():
        acc_scr[...] = part

    @pl.when(j > 0)
    def _():
        acc_scr[...] += part

    @pl.when(j == pl.num_programs(1) - 1)
    def _():
        y = acc_scr[...]
        normed = y * lax.rsqrt(jnp.mean(y * y, axis=-1, keepdims=True) + EPS) * g_ref[...]
        o_ref[...] = x_ref[...] + gt_ref[0] * normed


def _ffn_tail(up, wc, bc, wd, x2d, gt, g, *, tm, tiles_per_b, tf):
    r, d = x2d.shape
    fp = wd.shape[0]
    nf = fp // tf
    hb = tm // GROUP
    mr = gt.shape[1]
    halo = lambda off: pl.BlockSpec((GROUP, tf), lambda i, j, off=off: (jnp.maximum(i * hb - 1, 0), j + off))
    return pl.pallas_call(
        functools.partial(_ffn_tail_kernel, tiles_per_b=tiles_per_b),
        grid=(r // tm, nf),
        in_specs=[pl.BlockSpec((tm, tf), lambda i, j: (i, j)),
                  pl.BlockSpec((tm, tf), lambda i, j: (i, j + nf)),
                  halo(0), halo(nf),
                  pl.BlockSpec((3, tf), lambda i, j: (0, j)),
                  pl.BlockSpec((3, tf), lambda i, j: (0, j + nf)),
                  pl.BlockSpec((1, tf), lambda i, j: (0, j)),
                  pl.BlockSpec((1, tf), lambda i, j: (0, j + nf)),
                  pl.BlockSpec((tf, d), lambda i, j: (j, 0)),
                  pl.BlockSpec((tm, d), lambda i, j: (i, 0)),
                  pl.BlockSpec((1, mr, d), lambda i, j: (i // tiles_per_b, 0, 0)),
                  pl.BlockSpec((1, d), lambda i, j: (0, 0))],
        out_specs=pl.BlockSpec((tm, d), lambda i, j: (i, 0)),
        out_shape=jax.ShapeDtypeStruct((r, d), F32),
        scratch_shapes=[pltpu.VMEM((tm, d), F32)],
        compiler_params=_params(("arbitrary", "arbitrary")),
        name="ffn_tail",
    )(up, up, up, up, wc, wc, bc, bc, wd, x2d, gt, g.reshape(1, d))


def _pad_cols(a, n):
    return jnp.pad(a, [(0, 0)] * (a.ndim - 1) + [(0, n - a.shape[-1])])


def _split_hi_lo(w):
    hi = w.astype(BF16)
    return hi, (w - hi.astype(F32)).astype(BF16)


def _layer(x3, mods, st_conv, st_gdn, st_ffn, fox, lw, *, heads, front_pad, tm):
    (g_pre_mix, g_post_mix, g_pre_ffn, g_post_ffn, w_big, ws, w_conv_qkv, a_row, a_col, gn, qg, kg, brow, bcol,
     w_out_a, w_out_b, w_up, w_conv_ffn, b_conv_ffn, w_down, d_ff, tf) = lw
    sh_m, sc_m, gt_m, sh_f, sc_f, gt_f = mods
    nb, t, d = x3.shape
    aw = heads * 128
    x2d = x3.reshape(nb * t, d)
    tiles_per_b = max(t // tm, 1) if mods[0].shape[1] == 1 else 1

    proj, small = _norm_proj(x2d, sc_m, sh_m, g_pre_mix, w_big, ws, tm=tm, tiles_per_b=tiles_per_b)
    proj3 = proj.reshape(nb, t, 8 * aw)
    if st_conv is not None:
        k = st_conv.shape[1]
        proj3 = lax.dynamic_update_slice(proj3, st_conv, (0, GROUP - (t - front_pad) - k, 0))
    small3 = small.reshape(nb, t, 128)
    rows_t = 3 * GROUP
    smallt3 = jnp.swapaxes(small3[:, :, :rows_t], 1, 2)

    o_a, gdn_new = _gdn(proj3, small3, smallt3, w_conv_qkv, a_row, a_col, gn, st_gdn,
                        heads=heads, front_pad=front_pad)
    qn, kn, knb, vb, logf, cc, cr = _fox_prep(proj3, small3, smallt3, qg, kg, brow, bcol,
                                               heads=heads, front_pad=front_pad)
    o_b = fox(qn, knb, vb, cc, cr, proj3)

    x1 = _out_proj(o_a.reshape(nb * t, aw), o_b.reshape(nb * t, aw), w_out_a, w_out_b, x2d, gt_m, g_post_mix,
                   tm=min(tm, 512), tiles_per_b=max(t // min(tm, 512), 1) if mods[0].shape[1] == 1 else 1)
    up = _norm_proj(x1, sc_f, sh_f, g_pre_ffn, w_up, tm=tm, tiles_per_b=tiles_per_b)
    fp = w_down.shape[0]
    if st_ffn is not None:
        up3 = up.reshape(nb, t, 2 * fp)
        k = st_ffn.shape[1]
        up3 = lax.dynamic_update_slice(up3, st_ffn, (0, GROUP - (t - front_pad) - k, 0))
        up = up3.reshape(nb * t, 2 * fp)
    tmf = min(tm, 512)
    y = _ffn_tail(up, w_conv_ffn, b_conv_ffn, w_down, x1, gt_f, g_post_ffn, tm=tmf,
                  tiles_per_b=max(t // tmf, 1) if mods[0].shape[1] == 1 else 1, tf=tf)
    up3 = up.reshape(nb, t, 2 * fp)
    return y.reshape(nb, t, d), proj3, kn, logf, gdn_new, up3


def kernel(x_prompt, x_sample, cache_k, cache_v, cache_logf, state_gdn, state_conv_qkv, state_ffn_conv, page_table, c_prompt, c_sample, w_ada, b_ada, g_pre_mix, g_post_mix, g_pre_ffn, g_post_ffn, w_in, w_conv_qkv, a_log, dt_bias, g_gdn_norm, q_norm, k_norm, b_forget, w_out, w_up, w_conv_ffn, b_conv_ffn, w_down):
    depth = w_ada.shape[0]
    assert depth == 1, "single-layer step"
    b, t, d = x_prompt.shape
    bd, n_new, _ = x_sample.shape
    heads = state_gdn.shape[2]
    dh = state_gdn.shape[3]
    assert dh == 128 and cache_k.shape[3] == heads and n_new <= GROUP // 2
    aw = heads * dh
    page = cache_k.shape[2]
    n_pool = cache_k.shape[1]
    d_ff = w_down.shape[1]
    conv_a = w_conv_qkv.shape[1]
    ffn_conv = w_conv_ffn.shape[1]
    assert conv_a == 4 and ffn_conv == 3
    layer = 0
    (cache_k, cache_v, cache_logf, state_gdn, state_conv_qkv, state_ffn_conv, w_ada, b_ada, g_pre_mix, g_post_mix,
     g_pre_ffn, g_post_ffn, w_in, w_conv_qkv, a_log, dt_bias, g_gdn_norm, q_norm, k_norm, b_forget, w_out, w_up,
     w_conv_ffn, b_conv_ffn, w_down) = [
        (a.reshape(a.shape[1:]),) for a in
        (cache_k, cache_v, cache_logf, state_gdn, state_conv_qkv, state_ffn_conv, w_ada, b_ada, g_pre_mix, g_post_mix,
         g_pre_ffn, g_post_ffn, w_in, w_conv_qkv, a_log, dt_bias, g_gdn_norm, q_norm, k_norm, b_forget, w_out, w_up,
         w_conv_ffn, b_conv_ffn, w_down)]

    wi = w_in[layer]
    o1 = 4 * aw
    o2 = o1 + 2 * heads
    o3 = o2 + 4 * aw
    w_big = jnp.concatenate([wi[:, :o1], wi[:, o2:o3]], axis=1).astype(BF16)
    w_small = _pad_cols(jnp.concatenate([wi[:, o1:o2], wi[:, o3:]], axis=1), 128)
    ws = _split_hi_lo(w_small)
    zeros_h = jnp.zeros((heads,), F32)
    a_row = _pad_cols(jnp.stack([jnp.concatenate([zeros_h, a_log[layer]]),
                                 jnp.concatenate([zeros_h, dt_bias[layer]])]), 128)
    rows_t = 3 * GROUP
    a_col = jnp.pad(a_row[:, :rows_t].T, ((0, 0), (0, 0)))
    brow = _pad_cols(jnp.concatenate([zeros_h, zeros_h, b_forget[layer]])[None, :], 128)
    bcol = brow[:, :rows_t].T
    assert heads == GROUP and page == 128
    gn = g_gdn_norm[layer].reshape(1, dh)
    qg = q_norm[layer].reshape(1, dh)
    kg = k_norm[layer].reshape(1, dh)
    wo = w_out[layer].astype(BF16)
    w_out_a, w_out_b = wo[:aw], wo[aw:]
    tf = 512
    fp = -(-d_ff // tf) * tf
    wu = w_up[layer]
    w_up_p =jnp.concatenate([_pad_cols(wu[:, :d_ff], fp), _pad_cols(wu[:, d_ff:], fp)], axis=1).astype(BF16)
    wcf = w_conv_ffn[layer]
    w_conv_ffn_p =jnp.concatenate([_pad_cols(wcf[:, :d_ff], fp), _pad_cols(wcf[:, d_ff:], fp)], axis=1)
    bcf = b_conv_ffn[layer][None, :]
    b_conv_ffn_p = jnp.concatenate([_pad_cols(bcf[:, :d_ff], fp), _pad_cols(bcf[:, d_ff:], fp)], axis=1)
    w_down_p = jnp.pad(w_down[layer], ((0, fp - d_ff), (0, 0))).astype(BF16)
    lw = (g_pre_mix[layer], g_post_mix[layer], g_pre_ffn[layer], g_post_ffn[layer], w_big, ws, w_conv_qkv[layer],
          a_row, a_col, gn, qg, kg, brow, bcol, w_out_a, w_out_b, w_up_p, w_conv_ffn_p, b_conv_ffn_p, w_down_p,
          d_ff, tf)

    n_c = b + bd
    c_all = jnp.pad(jnp.concatenate([c_prompt, c_sample], axis=0), ((0, -n_c % GROUP), (0, 0)))
    mod = _ada(c_all, w_ada[layer], b_ada[layer])
    mods_p = [m[:b].reshape(b, 1, d) for m in jnp.split(mod, 6, axis=-1)]
    mods_s = [jnp.repeat(m[b:n_c], GROUP, axis=0).reshape(1, bd * GROUP, d) for m in jnp.split(mod, 6, axis=-1)]

    tm_p = _pick(t, 1024, 128)
    fox_p = lambda qn, knb, vb, cc, cr, proj3: _flash(
        qn, knb, vb, cc, cr[:, 2 * heads:3 * heads].reshape(b, heads, 1, t), proj3, heads=heads)
    zeros_s0 = jnp.zeros((b, heads, dh, dh), F32)
    y_p, proj_p, kn_p, logf_p, gdn_p, up_p = _layer(
        x_prompt, mods_p, None, zeros_s0, None, fox_p, lw, heads=heads, front_pad=0, tm=tm_p)

    front = GROUP - n_new
    x_s = jnp.pad(x_sample, ((0, 0), (front, 0), (0, 0)))
    suffix = _suffix(jnp.swapaxes(cache_logf[layer], 1, 2))
    ck3 = cache_k[layer].reshape(n_pool, page, aw)
    cv3 = cache_v[layer].reshape(n_pool, page, aw)
    fox_s = lambda qn, knb, vb, cc, cr, proj3: _paged(
        page_table, qn, knb, vb, cr, proj3, ck3, cv3, suffix, heads=heads, n_new=n_new)
    st_ffn = state_ffn_conv[layer]
    st_ffn_p =jnp.concatenate([_pad_cols(st_ffn[:, :, :d_ff], fp), _pad_cols(st_ffn[:, :, d_ff:], fp)], axis=-1)
    st_conv = _pad_cols(state_conv_qkv[layer], 8 * aw)
    y_s, proj_s, kn_s, logf_s, gdn_s, up_s = _layer(
        x_s, mods_s, st_conv, state_gdn[layer], st_ffn_p, fox_s, lw, heads=heads, front_pad=front, tm=bd * GROUP)

    n_pg = t // page
    unpad = lambda u: jnp.concatenate([u[..., :d_ff], u[..., fp:fp + d_ff]], axis=-1)
    k_prompt = kn_p.reshape(1, b, n_pg, page, heads, dh)
    v_prompt = proj_p[:, :, 6 * aw:7 * aw].reshape(1, b, n_pg, page, heads, dh)
    logf_prompt = logf_p[:, :, 2 * heads:3 * heads].reshape(1, b, n_pg, page, heads)
    conv_qkv_prompt = proj_p[:, t - (conv_a - 1):, :3 * aw][None]
    ffn_conv_prompt = unpad(up_p[:, t - (ffn_conv - 1):, :])[None]
    k_sample = kn_s[:, front:].reshape(1, bd, n_new, heads, dh)
    v_sample = proj_s[:, front:, 6 * aw:7 * aw].reshape(1, bd, n_new, heads, dh)
    logf_sample = logf_s[:, front:, 2 * heads:3 * heads][None]
    conv_qkv_sample = proj_s[:, GROUP - (conv_a - 1):, :3 * aw][None]
    ffn_conv_sample = unpad(up_s[:, GROUP - (ffn_conv - 1):, :])[None]
    return (y_p, y_s[:, front:], k_prompt, v_prompt, logf_prompt, gdn_p[None], conv_qkv_prompt, ffn_conv_prompt,
            k_sample, v_sample, logf_sample, gdn_s.astype(state_gdn[layer].dtype)[None], conv_qkv_sample, ffn_conv_sample)
```

```python
import functools

import jax
import jax.numpy as jnp
from jax import lax
from jax.experimental import pallas as pl
from jax.experimental.pallas import tpu as pltpu

EPS = 1e-6
F32 = jnp.float32
BF16 = jnp.bfloat16
HI = lax.Precision.HIGHEST
NEG = -1e30
GDN_CHUNK = 64
GROUP = 8
V7X_VMEM_LIMIT = 56 * 1024 * 1024

NN = (((1,), (0,)), ((), ()))
NT = (((1,), (1,)), ((), ()))
TN = (((0,), (0,)), ((), ()))


def _mm(a, b, dims=NN, precision=None):
    return lax.dot_general(a, b, dims, precision=precision, preferred_element_type=F32)


BNN = (((2,), (1,)), ((0,), (0,)))
BNT = (((2,), (2,)), ((0,), (0,)))
BTN = (((1,), (1,)), ((0,), (0,)))


def _bmm(a, b, dims=BNN):
    return lax.dot_general(a, b, dims, preferred_element_type=F32)


def _bmm1(a, b, dims):
    return _bmm(a.astype(BF16), b.astype(BF16), dims)


def _bmm3(a, b, dims):
    free = 2 if dims == BTN else 1
    m = a.shape[free]
    ah = a.astype(BF16).astype(F32)
    bh = b.astype(BF16)
    bl = (b - bh.astype(F32)).astype(BF16)
    stack = jnp.concatenate([ah, a - ah], axis=free).astype(BF16)
    r = _bmm(stack, bh, dims)
    r2 = _bmm(lax.slice_in_dim(stack, 0, m, axis=free), bl, dims)
    return lax.slice_in_dim(r, 0, m, axis=1) + lax.slice_in_dim(r, m, 2 * m, axis=1) + r2


_P_AQ = _P_INV = _P_MRG = _P_UW = _P_WS = _P_O = _P_S = _bmm1


def _pick(n, target, mult):
    best = None
    for d in range(mult, min(n, target) + 1, mult):
        if n % d == 0:
            best = d
    return best if best is not None else n


def _params(sem):
    return pltpu.CompilerParams(dimension_semantics=sem, vmem_limit_bytes=V7X_VMEM_LIMIT)


def _sigmoid(x):
    return 1.0 / (1.0 + jnp.exp(-x))


def _softplus(x):
    return jnp.maximum(x, 0.0) + jnp.log(1.0 + jnp.exp(-jnp.abs(x)))


def _tri(n, kind):
    r = lax.broadcasted_iota(jnp.int32, (n, n), 0)
    c = lax.broadcasted_iota(jnp.int32, (n, n), 1)
    if kind == "lower_incl":
        return (c <= r).astype(F32)
    if kind == "upper_incl":
        return (r <= c).astype(F32)
    raise ValueError(kind)


def _ada_kernel(c_ref, w_ref, b_ref, o_ref):
    c = c_ref[...]
    o_ref[...] = _mm(c * _sigmoid(c), w_ref[...], precision=HI) + b_ref[...]


def _ada(c_all, w_ada, b_ada):
    m, d = c_all.shape
    n = w_ada.shape[1]
    tn = _pick(n, 1024, 128)
    return pl.pallas_call(
        _ada_kernel,
        grid=(n // tn,),
        in_specs=[pl.BlockSpec((m, d), lambda j: (0, 0)),
                  pl.BlockSpec((d, tn), lambda j: (0, j)),
                  pl.BlockSpec((1, tn), lambda j: (0, j))],
        out_specs=pl.BlockSpec((m, tn), lambda j: (0, j)),
        out_shape=jax.ShapeDtypeStruct((m, n), F32),
        compiler_params=_params(("arbitrary",)),
        name="ada",
    )(c_all, w_ada, b_ada.reshape(1, n))


def _norm_proj_kernel(*refs, with_small):
    x_ref, sc_ref, sh_ref, g_ref, w_ref = refs[:5]
    if with_small:
        wsh_ref, wsl_ref, o_ref, os_ref, h_scr = refs[5:]
    else:
        o_ref, h_scr = refs[5:]

    @pl.when(pl.program_id(1) == 0)
    def _():
        x = x_ref[...]
        y = x * lax.rsqrt(jnp.mean(x * x, axis=-1, keepdims=True) + EPS) * g_ref[...]
        h = y * (1.0 + sc_ref[0]) + sh_ref[0]
        hb = h.astype(BF16)
        h_scr[...] = hb
        if with_small:
            hl = (h - hb.astype(F32)).astype(BF16)
            os_ref[...] = _mm(hb, wsh_ref[...]) + _mm(hb, wsl_ref[...]) + _mm(hl, wsh_ref[...])

    o_ref[...] = _mm(h_scr[...], w_ref[...])


def _norm_proj(x2d, sc, sh, g, w, ws=None, *, tm, tiles_per_b):
    r, d = x2d.shape
    n = w.shape[1]
    tn = _pick(n, 1024, 128)
    mr = sc.shape[1]
    mod_spec = pl.BlockSpec((1, mr, d), lambda i, j: (i // tiles_per_b, 0, 0))
    in_specs = [pl.BlockSpec((tm, d), lambda i, j: (i, 0)), mod_spec, mod_spec,
                pl.BlockSpec((1, d), lambda i, j: (0, 0)),
                pl.BlockSpec((d, tn), lambda i, j: (0, j))]
    args = [x2d, sc, sh, g.reshape(1, d), w]
    out_specs = pl.BlockSpec((tm, tn), lambda i, j: (i, j))
    out_shape = jax.ShapeDtypeStruct((r, n), F32)
    if ws is not None:
        ws_hi, ws_lo = ws
        ns = ws_hi.shape[1]
        in_specs += [pl.BlockSpec((d, ns), lambda i, j: (0, 0))] * 2
        args += [ws_hi, ws_lo]
        out_specs = [out_specs, pl.BlockSpec((tm, ns), lambda i, j: (i, 0))]
        out_shape = [out_shape, jax.ShapeDtypeStruct((r, ns), F32)]
    return pl.pallas_call(
        functools.partial(_norm_proj_kernel, with_small=ws is not None),
        grid=(r // tm, n // tn),
        in_specs=in_specs, out_specs=out_specs, out_shape=out_shape,
        scratch_shapes=[pltpu.VMEM((tm, d), BF16)],
        compiler_params=_params(("arbitrary", "arbitrary")),
        name="norm_proj",
    )(*args)


def _shift_rows(xe, s):
    return pltpu.roll(xe, s, 0)[GROUP:]


def _gdn_kernel(qkv_ref, z_ref, sm_ref, smt_ref, wc_ref, arow_ref, acol_ref, gn_ref, s0_ref,
                o_ref, sout_ref, s_scr, prev_scr, *, heads, chunk, n_chunks, front_pad):
    t = pl.program_id(1)
    tc = chunk * n_chunks
    aw = heads * 128

    @pl.when(t == 0)
    def _():
        s_scr[...] = s0_ref[0]
        prev_scr[...] = jnp.zeros_like(prev_scr)

    row = lax.broadcasted_iota(jnp.int32, (tc, 1), 0)
    col = lax.broadcasted_iota(jnp.int32, (1, tc), 1)
    valid_c = row >= front_pad
    valid_r = col >= front_pad

    sm = sm_ref[0]
    g_tile = jnp.where(valid_c, -jnp.exp(arow_ref[0:1, :]) * _softplus(sm + arow_ref[1:2, :]), 0.0)
    beta_tile = jnp.where(valid_c, _sigmoid(sm), 0.0)
    smt = smt_ref[0]
    gt_all = jnp.where(valid_r, -jnp.exp(acol_ref[:, 0:1]) * _softplus(smt + acol_ref[:, 1:2]), 0.0)

    lo_incl = _tri(chunk, "lower_incl")
    up_incl = _tri(chunk, "upper_incl")
    ri = lax.broadcasted_iota(jnp.int32, (chunk, chunk), 0)
    ci = lax.broadcasted_iota(jnp.int32, (chunk, chunk), 1)
    incl = ci <= ri
    strict = ci < ri
    eye = (ci == ri).astype(F32)

    gcol_tiles = [_mm(lo_incl, g_tile[c * chunk:(c + 1) * chunk], precision=HI) for c in range(n_chunks)]
    grow_tiles = [_mm(gt_all[:, c * chunk:(c + 1) * chunk], up_incl, precision=HI) for c in range(n_chunks)]

    wc = wc_ref[...]
    per_head = []
    for h in range(heads):
        parts = []
        for p in range(3):
            lo = p * aw + h * 128
            x = qkv_ref[0, :, lo:lo + 128]
            xe = jnp.concatenate([prev_scr[:, lo:lo + 128], x], axis=0)
            conv = (wc[3:4, lo:lo + 128] * x + wc[2:3, lo:lo + 128] * _shift_rows(xe, 1)
                    + wc[1:2, lo:lo + 128] * _shift_rows(xe, 2) + wc[0:1, lo:lo + 128] * _shift_rows(xe, 3))
            parts.append(conv * _sigmoid(conv))
        q_all, k_all, v_all = parts
        q_all = q_all * lax.rsqrt(jnp.sum(q_all * q_all, axis=-1, keepdims=True) + EPS) * (128.0 ** -0.5)
        k_all = k_all * lax.rsqrt(jnp.sum(k_all * k_all, axis=-1, keepdims=True) + EPS)
        per_head.append((q_all, jnp.where(valid_c, k_all, 0.0), v_all))

    units = [(c, h) for c in range(n_chunks) for h in range(heads)]
    rows = lambda c: slice(c * chunk, (c + 1) * chunk)
    q = jnp.stack([per_head[h][0][rows(c)] for c, h in units])
    k = jnp.stack([per_head[h][1][rows(c)] for c, h in units])
    v = jnp.stack([per_head[h][2][rows(c)] for c, h in units])
    gcol = jnp.stack([gcol_tiles[c][:, heads + h:heads + h + 1] for c, h in units])
    grow = jnp.stack([grow_tiles[c][heads + h:heads + h + 1, :] for c, h in units])
    bcol = jnp.stack([beta_tile[rows(c), h:h + 1] for c, h in units])

    decay = jnp.where(incl, jnp.exp(jnp.where(incl, gcol - grow, 0.0)), 0.0)
    kb = k * bcol
    vb = v * bcol
    aq = _P_AQ(jnp.concatenate([kb, q], axis=1), k, BNT)
    lower = jnp.where(strict, aq[:, :chunk] * decay, 0.0)
    qk = jnp.where(incl, aq[:, chunk:] * decay, 0.0)
    base = min(GROUP, chunk)
    same_blk = lambda s: (ri >> (s.bit_length() - 1)) == (ci >> (s.bit_length() - 1))
    neg_bd = jnp.where(same_blk(base), -lower, 0.0)
    nm = _P_INV(neg_bd, neg_bd, BNN)
    tinv = eye + neg_bd
    n_base = base.bit_length() - 2
    for lvl in range(n_base):
        if lvl < n_base - 1:
            r = _P_INV(jnp.concatenate([nm, tinv], axis=1), nm, BNN)
            tinv = tinv + r[:, chunk:]
            nm = r[:, :chunk]
        else:
            tinv = tinv + _P_INV(tinv, nm, BNN)
    s = base
    while s < chunk:
        off = jnp.where(same_blk(2 * s) & jnp.logical_not(same_blk(s)), lower, 0.0)
        tinv = tinv - _P_MRG(tinv, _P_MRG(off, tinv, BNN), BNN)
        s *= 2
    eg = jnp.exp(gcol)
    uw = _P_UW(tinv, jnp.concatenate([vb, kb * eg], axis=2), BNN)
    wq = jnp.concatenate([uw[:, :, 128:], q * eg], axis=1)
    g_last = gcol[:, chunk - 1:chunk, :]
    kd = k * jnp.exp(g_last - gcol)
    e_last = jnp.exp(g_last)

    s_all = s_scr[...]
    for c in range(n_chunks):
        us = slice(c * heads, (c + 1) * heads)
        ws = _P_WS(wq[us], s_all, BNN)
        v_new = uw[us, :, :128] - ws[:, :chunk]
        o = ws[:, chunk:] + _P_O(qk[us], v_new, BNN)
        s_all = s_all * e_last[us] + _P_S(kd[us], v_new, BTN)
        on = o * lax.rsqrt(jnp.mean(o * o, axis=-1, keepdims=True) + EPS) * gn_ref[...]
        for h in range(heads):
            z = z_ref[0, rows(c), h * 128:(h + 1) * 128]
            o_ref[0, rows(c), h * 128:(h + 1) * 128] = (on[h] * (z * _sigmoid(z))).astype(o_ref.dtype)
    s_scr[...] = s_all

    prev_scr[...] = qkv_ref[0, tc - GROUP:, :]

    @pl.when(t == pl.num_programs(1) - 1)
    def _():
        sout_ref[0] = s_scr[...]


def _gdn(proj3, small3, smallt3, w_conv, a_row, a_col, gn, s0, *, heads, front_pad):
    b, t, _ = proj3.shape
    aw = heads * 128
    chunk = min(GDN_CHUNK, t)
    tc = t if t <= 128 else 128
    n_chunks = tc // chunk
    rows_t = smallt3.shape[1]
    kern = functools.partial(_gdn_kernel, heads=heads, chunk=chunk, n_chunks=n_chunks, front_pad=front_pad)
    return pl.pallas_call(
        kern,
        grid=(b, t // tc),
        in_specs=[pl.BlockSpec((1, tc, 3 * aw), lambda i, j: (i, j, 0)),
                  pl.BlockSpec((1, tc, aw), lambda i, j: (i, j, 3)),
                  pl.BlockSpec((1, tc, 128), lambda i, j: (i, j, 0)),
                  pl.BlockSpec((1, rows_t, tc), lambda i, j: (i, 0, j)),
                  pl.BlockSpec((4, 3 * aw), lambda i, j: (0, 0)),
                  pl.BlockSpec((2, 128), lambda i, j: (0, 0)),
                  pl.BlockSpec((rows_t, 2), lambda i, j: (0, 0)),
                  pl.BlockSpec((1, 128), lambda i, j: (0, 0)),
                  pl.BlockSpec((1, heads, 128, 128), lambda i, j: (i, 0, 0, 0))],
        out_specs=[pl.BlockSpec((1, tc, aw), lambda i, j: (i, j, 0)),
                   pl.BlockSpec((1, heads, 128, 128), lambda i, j: (i, 0, 0, 0))],
        out_shape=[jax.ShapeDtypeStruct((b, t, aw), BF16),
                   jax.ShapeDtypeStruct((b, heads, 128, 128), F32)],
        scratch_shapes=[pltpu.VMEM((heads, 128, 128), F32), pltpu.VMEM((GROUP, 3 * aw), F32)],
        compiler_params=_params(("arbitrary", "arbitrary")),
        name="gdn",
    )(proj3, proj3, small3, smallt3, w_conv, a_row, a_col, gn, s0)


def _fox_prep_kernel(q_ref, k_ref, v_ref, sm_ref, smt_ref, qg_ref, kg_ref, brow_ref, bcol_ref,
                     qn_ref, kn_ref, knb_ref, vb_ref, logf_ref, cc_ref, cr_ref, carry_c, carry_r,
                     *, heads, front_pad):
    t = pl.program_id(1)
    tr = q_ref.shape[1]

    @pl.when(t == 0)
    def _():
        carry_c[...] = jnp.zeros_like(carry_c)
        carry_r[...] = jnp.zeros_like(carry_r)

    for h in range(heads):
        cs = slice(h * 128, (h + 1) * 128)
        q = q_ref[0, :, cs]
        qn = q * lax.rsqrt(jnp.mean(q * q, axis=-1, keepdims=True) + EPS) * qg_ref[...]
        qn_ref[0, :, cs] = (qn * (128.0 ** -0.5)).astype(BF16)
        k = k_ref[0, :, cs]
        kn = k * lax.rsqrt(jnp.mean(k * k, axis=-1, keepdims=True) + EPS) * kg_ref[...]
        kn_ref[0, :, cs] = kn
        knb_ref[0, :, cs] = kn.astype(BF16)
    vb_ref[0] = v_ref[0].astype(BF16)

    row = lax.broadcasted_iota(jnp.int32, (tr, 1), 0)
    col = lax.broadcasted_iota(jnp.int32, (1, tr), 1)
    logf = -_softplus(-(sm_ref[0] + brow_ref[...]))
    logf_ref[0] = logf
    cum = _mm(_tri(tr, "lower_incl"), jnp.where(row >= front_pad, logf, 0.0), precision=HI) + carry_c[...]
    cc_ref[0] = cum
    carry_c[...] = cum[tr - 1:tr, :]
    logft = jnp.where(col >= front_pad, -_softplus(-(smt_ref[0] + bcol_ref[...])), 0.0)
    cumt = _mm(logft, _tri(tr, "upper_incl"), precision=HI) + carry_r[:, 0:1]
    cr_ref[0] = cumt
    carry_r[...] = jnp.broadcast_to(cumt[:, tr - 1:tr], carry_r.shape)


def _fox_prep(proj3, small3, smallt3, qg, kg, brow, bcol, *, heads, front_pad):
    b, t, _ = proj3.shape
    bw = heads * 128
    tr = _pick(t, 256, 128) if t >= 128 else t
    rows_t = smallt3.shape[1]
    wide = lambda blk: pl.BlockSpec((1, tr, bw), lambda i, j, blk=blk: (i, j, blk))
    out_w = pl.BlockSpec((1, tr, bw), lambda i, j: (i, j, 0))
    out_s = pl.BlockSpec((1, tr, 128), lambda i, j: (i, j, 0))
    return pl.pallas_call(
        functools.partial(_fox_prep_kernel, heads=heads, front_pad=front_pad),
        grid=(b, t // tr),
        in_specs=[wide(4), wide(5), wide(6),
                  pl.BlockSpec((1, tr, 128), lambda i, j: (i, j, 0)),
                  pl.BlockSpec((1, rows_t, tr), lambda i, j: (i, 0, j)),
                  pl.BlockSpec((1, 128), lambda i, j: (0, 0)),
                  pl.BlockSpec((1, 128), lambda i, j: (0, 0)),
                  pl.BlockSpec((1, 128), lambda i, j: (0, 0)),
                  pl.BlockSpec((rows_t, 1), lambda i, j: (0, 0))],
        out_specs=[out_w, out_w, out_w, out_w, out_s, out_s,
                   pl.BlockSpec((1, rows_t, tr), lambda i, j: (i, 0, j))],
        out_shape=[jax.ShapeDtypeStruct((b, t, bw), BF16), jax.ShapeDtypeStruct((b, t, bw), F32),
                   jax.ShapeDtypeStruct((b, t, bw), BF16), jax.ShapeDtypeStruct((b, t, bw), BF16),
                   jax.ShapeDtypeStruct((b, t, 128), F32), jax.ShapeDtypeStruct((b, t, 128), F32),
                   jax.ShapeDtypeStruct((b, rows_t, t), F32)],
        scratch_shapes=[pltpu.VMEM((1, 128), F32), pltpu.VMEM((rows_t, 128), F32)],
        compiler_params=_params(("arbitrary", "arbitrary")),
        name="fox_prep",
    )(proj3, proj3, proj3, small3, smallt3, qg, kg, brow, bcol)


def _flash_kernel(q_ref, k_ref, v_ref, cc_ref, cr_ref, gate_ref, o_ref, *, heads, tq):
    h = pl.program_id(1)
    qi = pl.program_id(2)
    q = q_ref[0]
    lane = lax.broadcasted_iota(jnp.int32, (1, 128), 1)
    cq = jnp.sum(jnp.where(lane == 2 * heads + h, cc_ref[0], 0.0), axis=-1, keepdims=True)

    def step(j, carry, masked):
        m, l, acc = carry
        start = pl.multiple_of(j * tq, tq)
        ks = k_ref[0, pl.ds(start, tq), :]
        vs = v_ref[0, pl.ds(start, tq), :]
        ck = cr_ref[0, 0, :, pl.ds(start, tq)]
        s = _mm(q, ks, NT) + (cq - ck)
        if masked:
            ri = lax.broadcasted_iota(jnp.int32, (tq, tq), 0)
            ci = lax.broadcasted_iota(jnp.int32, (tq, tq), 1)
            s = jnp.where(ci <= ri, s, NEG)
        m_new = jnp.maximum(m, jnp.max(s, axis=-1, keepdims=True))
        corr = jnp.exp(m - m_new)
        p = jnp.exp(s - m_new)
        l = l * corr + jnp.sum(p, axis=-1, keepdims=True)
        acc = acc * corr + _mm(p.astype(BF16), vs)
        return m_new, l, acc

    init = (jnp.full((tq, 1), NEG, F32), jnp.zeros((tq, 1), F32), jnp.zeros((tq, 128), F32))
    carry = lax.fori_loop(0, qi, lambda j, c: step(j, c, False), init)
    m, l, acc = step(qi, carry, True)
    g = gate_ref[0]
    o_ref[0] = (acc / l * _sigmoid(g)).astype(o_ref.dtype)


def _flash(qn, knb, vb, cc, cr4, proj3, *, heads):
    b, t, bw = qn.shape
    tq = _pick(t, 512, 128)
    qspec = pl.BlockSpec((1, tq, 128), lambda i, h, j: (i, j, h))
    kvspec = pl.BlockSpec((1, t, 128), lambda i, h, j: (i, 0, h))
    return pl.pallas_call(
        functools.partial(_flash_kernel, heads=heads, tq=tq),
        grid=(b, heads, t // tq),
        in_specs=[qspec, kvspec, kvspec,
                  pl.BlockSpec((1, tq, 128), lambda i, h, j: (i, j, 0)),
                  pl.BlockSpec((1, 1, 1, t), lambda i, h, j: (i, h, 0, 0)),
                  pl.BlockSpec((1, tq, 128), lambda i, h, j: (i, j, 7 * heads + h))],
        out_specs=qspec,
        out_shape=jax.ShapeDtypeStruct((b, t, bw), BF16),
        compiler_params=_params(("arbitrary", "arbitrary", "arbitrary")),
        name="fox_flash",
    )(qn, knb, vb, cc, cr4, proj3)


def _suffix_kernel(lf_ref, o_ref, *, heads):
    x = lf_ref[...]
    n = x.shape[1]
    lane = lax.broadcasted_iota(jnp.int32, (1, n), 1)
    incl = x
    tot = x
    d = heads
    while d < n:
        incl = incl + jnp.where(lane < n - d, pltpu.roll(incl, n - d, 1), 0.0)
        tot = tot + pltpu.roll(tot, d, 1)
        d *= 2
    o_ref[:, :n] = incl - x
    o_ref[:, n:] = tot


def _suffix(logf_flat, *, heads):
    n_pool, n = logf_flat.shape
    gp = _pick(n_pool, 256, GROUP)
    return pl.pallas_call(
        functools.partial(_suffix_kernel, heads=heads),
        grid=(n_pool // gp,),
        in_specs=[pl.BlockSpec((gp, n), lambda i: (i, 0))],
        out_specs=pl.BlockSpec((gp, 2 * n), lambda i: (i, 0)),
        out_shape=jax.ShapeDtypeStruct((n_pool, 2 * n), F32),
        compiler_params=_params(("arbitrary",)),
        name="page_suffix",
    )(logf_flat)


def _paged_kernel(pt_ref, *refs, heads, n_new, g_pages):
    del pt_ref
    qn_ref, knb_ref, vb_ref, cr_ref, gate_ref = refs[:5]
    k_refs = refs[5:5 + g_pages]
    v_refs = refs[5 + g_pages:5 + 2 * g_pages]
    s_refs = refs[5 + 2 * g_pages:5 + 3 * g_pages]
    o_ref, q_scr, m_scr, l_scr, acc_scr, tail_scr, cn_scr = refs[5 + 3 * g_pages:]
    p = pl.program_id(1)
    bw = heads * 128
    nr = n_new * heads
    first = GROUP - n_new
    n = tail_scr.shape[1]
    row_head = lax.broadcasted_iota(jnp.int32, (nr, 1), 0) & (heads - 1)

    @pl.when(p == 0)
    def _():
        lane_head = lax.broadcasted_iota(jnp.int32, (heads, bw), 1) // 128
        head_mask = lane_head == lax.broadcasted_iota(jnp.int32, (heads, bw), 0)
        qn = qn_ref[0].astype(F32)
        qbd = jnp.concatenate(
            [jnp.where(head_mask, jnp.broadcast_to(qn[first + i:first + i + 1, :], (heads, bw)), 0.0)
             for i in range(n_new)], axis=0)
        q_all = qbd[:, 0:128]
        for h in range(1, heads):
            q_all = q_all + qbd[:, h * 128:(h + 1) * 128]
        q_scr[...] = q_all.astype(BF16)
        cr = cr_ref[0][2 * heads:3 * heads, :]
        cn_col = jnp.concatenate([cr[:, first + i:first + i + 1] for i in range(n_new)], axis=0)
        cn_scr[...] = jnp.broadcast_to(cn_col, cn_scr.shape)
        cn_key = jnp.concatenate([cr] * n_new, axis=0)
        s = _mm(qbd.astype(BF16), knb_ref[0], NT) + cn_col - cn_key
        ri = lax.broadcasted_iota(jnp.int32, (nr, GROUP), 0) // heads
        ci = lax.broadcasted_iota(jnp.int32, (nr, GROUP), 1)
        s = jnp.where((ci >= first) & (ci - first <= ri), s, NEG)
        m0 = jnp.max(s, axis=-1, keepdims=True)
        p0 = jnp.exp(s - m0)
        m_scr[...] = jnp.broadcast_to(m0, m_scr.shape)
        l_scr[...] = jnp.broadcast_to(jnp.sum(p0, axis=-1, keepdims=True), l_scr.shape)
        full = _mm(p0.astype(BF16), vb_ref[0])
        acc0 = jnp.where(row_head == 0, full[:, 0:128], 0.0)
        for h in range(1, heads):
            acc0 = acc0 + jnp.where(row_head == h, full[:, h * 128:(h + 1) * 128], 0.0)
        acc_scr[...] = acc0
        tail_scr[...] = jnp.zeros_like(tail_scr)

    wide = lambda a: jnp.concatenate([a] * (n // 128), axis=1)
    q_all = q_scr[...]
    cn_w = wide(cn_scr[...])
    m = m_scr[...]
    l = l_scr[...]
    acc = acc_scr[...]
    tail = tail_scr[...]
    valid = (lax.broadcasted_iota(jnp.int32, (nr, n), 1) & (heads - 1)) == row_head
    scores = []
    for i in range(g_pages):
        blk = s_refs[i][0]
        bias = blk[:, :n] + tail
        tail = tail + blk[:, n:]
        kf = k_refs[i][0, 0].reshape(n, 128).astype(BF16)
        scores.append(jnp.where(valid, _mm(q_all, kf, NT) + bias + cn_w, NEG))
    s_max = scores[0]
    for s in scores[1:]:
        s_max = jnp.maximum(s_max, s)
    m_new = jnp.maximum(m, jnp.max(s_max, axis=-1, keepdims=True))
    corr = jnp.exp(m - m_new)
    m_w = wide(m_new)
    probs = [jnp.exp(s - m_w) for s in scores]
    p_sum = probs[0]
    for pr in probs[1:]:
        p_sum = p_sum + pr
    pv = _mm(probs[0].astype(BF16), v_refs[0][0, 0].reshape(n, 128).astype(BF16))
    for i in range(1, g_pages):
        pv = pv + _mm(probs[i].astype(BF16), v_refs[i][0, 0].reshape(n, 128).astype(BF16))
    l = l * corr + jnp.sum(p_sum, axis=-1, keepdims=True)
    acc = acc * corr + pv
    m_scr[...] = m_new
    l_scr[...] = l
    acc_scr[...] = acc
    tail_scr[...] = tail

    @pl.when(p == pl.num_programs(1) - 1)
    def _():
        o = jnp.concatenate([acc / l] * heads, axis=1)
        keep = (lax.broadcasted_iota(jnp.int32, (nr, bw), 1) // 128) == row_head
        o = jnp.where(keep, o, 0.0)
        rows = [jnp.zeros((first, bw), F32)]
        for i in range(n_new):
            rows.append(jnp.sum(o[i * heads:(i + 1) * heads, :], axis=0, keepdims=True))
        out = jnp.concatenate(rows, axis=0)
        o_ref[0] = (out * _sigmoid(gate_ref[0])).astype(o_ref.dtype)


def _paged(page_table, qn, knb, vb, cr, proj3, cache_k, cache_v, suffix3, *, heads, n_new):
    bd, _, bw = qn.shape
    n_pages = page_table.shape[1]
    page = cache_k.shape[2]
    n = page * heads
    g_pages = _pick(n_pages, 8, 1)
    rows_t = cr.shape[1]
    nr = n_new * heads
    seq = lambda blk_w, blk: pl.BlockSpec((1, GROUP, blk_w), lambda b, p, pt, blk=blk: (b, 0, blk))
    page_of = lambda b, p, pt, i: pt[b, n_pages - 1 - (p * g_pages + i)]
    kv_spec = lambda i: pl.BlockSpec((1, 1, page, heads, 128),
                                     lambda b, p, pt, i=i: (0, page_of(b, p, pt, i), 0, 0, 0))
    suf_spec = lambda i: pl.BlockSpec((1, 1, 2 * n), lambda b, p, pt, i=i: (page_of(b, p, pt, i), 0, 0))

    in_specs = [seq(bw, 0), seq(bw, 0), seq(bw, 0),
                pl.BlockSpec((1, rows_t, GROUP), lambda b, p, pt: (b, 0, 0)),
                seq(bw, 7)]
    in_specs += [kv_spec(i) for i in range(g_pages)] * 2
    in_specs += [suf_spec(i) for i in range(g_pages)]
    grid_spec = pltpu.PrefetchScalarGridSpec(
        num_scalar_prefetch=1,
        grid=(bd, n_pages // g_pages),
        in_specs=in_specs,
        out_specs=pl.BlockSpec((1, GROUP, bw), lambda b, p, pt: (b, 0, 0)),
        scratch_shapes=[pltpu.VMEM((nr, 128), BF16), pltpu.VMEM((nr, 128), F32), pltpu.VMEM((nr, 128), F32),
                        pltpu.VMEM((nr, 128), F32), pltpu.VMEM((1, n), F32), pltpu.VMEM((nr, 128), F32)],
    )
    return pl.pallas_call(
        functools.partial(_paged_kernel, heads=heads, n_new=n_new, g_pages=g_pages),
        grid_spec=grid_spec,
        out_shape=jax.ShapeDtypeStruct((bd, GROUP, bw), BF16),
        compiler_params=_params(("arbitrary", "arbitrary")),
        name="fox_paged",
    )(page_table, qn, knb, vb, cr, proj3, *([cache_k] * g_pages), *([cache_v] * g_pages), *([suffix3] * g_pages))


def _out_proj_kernel(oa_ref, ob_ref, wa_ref, wb_ref, x_ref, gt_ref, g_ref, o_ref):
    mix = _mm(oa_ref[...], wa_ref[...]) + _mm(ob_ref[...], wb_ref[...])
    normed = mix * lax.rsqrt(jnp.mean(mix * mix, axis=-1, keepdims=True) + EPS) * g_ref[...]
    o_ref[...] = x_ref[...] + gt_ref[0] * normed


def _out_proj(oa, ob, wa, wb, x2d, gt, g, *, tm, tiles_per_b):
    r, d = x2d.shape
    aw, bw = oa.shape[1], ob.shape[1]
    mr = gt.shape[1]
    return pl.pallas_call(
        _out_proj_kernel,
        grid=(r // tm,),
        in_specs=[pl.BlockSpec((tm, aw), lambda i: (i, 0)),
                  pl.BlockSpec((tm, bw), lambda i: (i, 0)),
                  pl.BlockSpec((aw, d), lambda i: (0, 0)),
                  pl.BlockSpec((bw, d), lambda i: (0, 0)),
                  pl.BlockSpec((tm, d), lambda i: (i, 0)),
                  pl.BlockSpec((1, mr, d), lambda i: (i // tiles_per_b, 0, 0)),
                  pl.BlockSpec((1, d), lambda i: (0, 0))],
        out_specs=pl.BlockSpec((tm, d), lambda i: (i, 0)),
        out_shape=jax.ShapeDtypeStruct((r, d), F32),
        compiler_params=_params(("arbitrary",)),
        name="out_proj",
    )(oa, ob, wa, wb, x2d, gt, g.reshape(1, d))


def _ffn_tail_kernel(ug_ref, uv_ref, hg_ref, hv_ref, wcg_ref, wcv_ref, bg_ref, bv_ref, wd_ref, x_ref, gt_ref, g_ref,
                     o_ref, acc_scr, *, tiles_per_b):
    i = pl.program_id(0)
    j = pl.program_id(1)
    first = (i % tiles_per_b) == 0

    def conv(u_ref, halo_ref, wc_ref, b_ref):
        x = u_ref[...]
        halo = jnp.where(first, 0.0, halo_ref[...])
        xe = jnp.concatenate([halo, x], axis=0)
        wc = wc_ref[...]
        return wc[2:3] * x + wc[1:2] * _shift_rows(xe, 1) + wc[0:1] * _shift_rows(xe, 2) + b_ref[...]

    gate = conv(ug_ref, hg_ref, wcg_ref, bg_ref)
    val = conv(uv_ref, hv_ref, wcv_ref, bv_ref)
    act = (gate * _sigmoid(gate) * val).astype(BF16)
    part = _mm(act, wd_ref[...])

    @pl.when(j == 0)
    def _():
        acc_scr[...] = part

    @pl.when(j > 0)
    def _():
        acc_scr[...] += part

    @pl.when(j == pl.num_programs(1) - 1)
    def _():
        y = acc_scr[...]
        normed = y * lax.rsqrt(jnp.mean(y * y, axis=-1, keepdims=True) + EPS) * g_ref[...]
        o_ref[...] = x_ref[...] + gt_ref[0] * normed


def _ffn_tail(up, wc, bc, wd, x2d, gt, g, *, tm, tiles_per_b, tf):
    r, d = x2d.shape
    fp = wd.shape[0]
    nf = fp // tf
    hb = tm // GROUP
    mr = gt.shape[1]
    halo = lambda off: pl.BlockSpec((GROUP, tf), lambda i, j, off=off: (jnp.maximum(i * hb - 1, 0), j + off))
    return pl.pallas_call(
        functools.partial(_ffn_tail_kernel, tiles_per_b=tiles_per_b),
        grid=(r // tm, nf),
        in_specs=[pl.BlockSpec((tm, tf), lambda i, j: (i, j)),
                  pl.BlockSpec((tm, tf), lambda i, j: (i, j + nf)),
                  halo(0), halo(nf),
                  pl.BlockSpec((3, tf), lambda i, j: (0, j)),
                  pl.BlockSpec((3, tf), lambda i, j: (0, j + nf)),
                  pl.BlockSpec((1, tf), lambda i, j: (0, j)),
                  pl.BlockSpec((1, tf), lambda i, j: (0, j + nf)),
                  pl.BlockSpec((tf, d), lambda i, j: (j, 0)),
                  pl.BlockSpec((tm, d), lambda i, j: (i, 0)),
                  pl.BlockSpec((1, mr, d), lambda i, j: (i // tiles_per_b, 0, 0)),
                  pl.BlockSpec((1, d), lambda i, j: (0, 0))],
        out_specs=pl.BlockSpec((tm, d), lambda i, j: (i, 0)),
        out_shape=jax.ShapeDtypeStruct((r, d), F32),
        scratch_shapes=[pltpu.VMEM((tm, d), F32)],
        compiler_params=_params(("arbitrary", "arbitrary")),
        name="ffn_tail",
    )(up, up, up, up, wc, wc, bc, bc, wd, x2d, gt, g.reshape(1, d))


def _pad_cols(a, n):
    return jnp.pad(a, [(0, 0)] * (a.ndim - 1) + [(0, n - a.shape[-1])])


def _split_hi_lo(w):
    hi = w.astype(BF16)
    return hi, (w - hi.astype(F32)).astype(BF16)


def _layer(x3, mods, st_conv, st_gdn, st_ffn, fox, lw, *, heads, front_pad, tm):
    (g_pre_mix, g_post_mix, g_pre_ffn, g_post_ffn, w_big, ws, w_conv_qkv, a_row, a_col, gn, qg, kg, brow, bcol,
     w_out_a, w_out_b, w_up, w_conv_ffn, b_conv_ffn, w_down, d_ff, tf) = lw
    sh_m, sc_m, gt_m, sh_f, sc_f, gt_f = mods
    nb, t, d = x3.shape
    aw = heads * 128
    x2d = x3.reshape(nb * t, d)
    tiles_per_b = max(t // tm, 1) if mods[0].shape[1] == 1 else 1

    proj, small = _norm_proj(x2d, sc_m, sh_m, g_pre_mix, w_big, ws, tm=tm, tiles_per_b=tiles_per_b)
    proj3 = proj.reshape(nb, t, 8 * aw)
    if st_conv is not None:
        k = st_conv.shape[1]
        proj3 = lax.dynamic_update_slice(proj3, st_conv, (0, GROUP - (t - front_pad) - k, 0))
    small3 = small.reshape(nb, t, 128)
    rows_t = 3 * GROUP
    smallt3 = jnp.swapaxes(small3[:, :, :rows_t], 1, 2)

    o_a, gdn_new = _gdn(proj3, small3, smallt3, w_conv_qkv, a_row, a_col, gn, st_gdn,
                        heads=heads, front_pad=front_pad)
    qn, kn, knb, vb, logf, cc, cr = _fox_prep(proj3, small3, smallt3, qg, kg, brow, bcol,
                                               heads=heads, front_pad=front_pad)
    o_b = fox(qn, knb, vb, cc, cr, proj3)

    x1 = _out_proj(o_a.reshape(nb * t, aw), o_b.reshape(nb * t, aw), w_out_a, w_out_b, x2d, gt_m, g_post_mix,
                   tm=min(tm, 512), tiles_per_b=max(t // min(tm, 512), 1) if mods[0].shape[1] == 1 else 1)
    up = _norm_proj(x1, sc_f, sh_f, g_pre_ffn, w_up, tm=tm, tiles_per_b=tiles_per_b)
    fp = w_down.shape[0]
    if st_ffn is not None:
        up3 = up.reshape(nb, t, 2 * fp)
        k = st_ffn.shape[1]
        up3 = lax.dynamic_update_slice(up3, st_ffn, (0, GROUP - (t - front_pad) - k, 0))
        up = up3.reshape(nb * t, 2 * fp)
    tmf = min(tm, 512)
    y = _ffn_tail(up, w_conv_ffn, b_conv_ffn, w_down, x1, gt_f, g_post_ffn, tm=tmf,
                  tiles_per_b=max(t // tmf, 1) if mods[0].shape[1] == 1 else 1, tf=tf)
    up3 = up.reshape(nb, t, 2 * fp)
    return y.reshape(nb, t, d), proj3, kn, logf, gdn_new, up3


def kernel(x_prompt, x_sample, cache_k, cache_v, cache_logf, state_gdn, state_conv_qkv, state_ffn_conv, page_table, c_prompt, c_sample, w_ada, b_ada, g_pre_mix, g_post_mix, g_pre_ffn, g_post_ffn, w_in, w_conv_qkv, a_log, dt_bias, g_gdn_norm, q_norm, k_norm, b_forget, w_out, w_up, w_conv_ffn, b_conv_ffn, w_down):
    depth = w_ada.shape[0]
    assert depth == 1, "single-layer step"
    b, t, d = x_prompt.shape
    bd, n_new, _ = x_sample.shape
    heads = state_gdn.shape[2]
    dh = state_gdn.shape[3]
    assert dh == 128 and cache_k.shape[3] == heads and n_new <= GROUP // 2
    aw = heads * dh
    page = cache_k.shape[2]
    n_pool = cache_k.shape[1]
    d_ff = w_down.shape[1]
    conv_a = w_conv_qkv.shape[1]
    ffn_conv = w_conv_ffn.shape[1]
    assert conv_a == 4 and ffn_conv == 3
    layer = 0
    (cache_logf, state_gdn, state_conv_qkv, state_ffn_conv, w_ada, b_ada, g_pre_mix, g_post_mix,
     g_pre_ffn, g_post_ffn, w_in, w_conv_qkv, a_log, dt_bias, g_gdn_norm, q_norm, k_norm, b_forget, w_out, w_up,
     w_conv_ffn, b_conv_ffn, w_down) = [
        (a.reshape(a.shape[1:]),) for a in
        (cache_logf, state_gdn, state_conv_qkv, state_ffn_conv, w_ada, b_ada, g_pre_mix, g_post_mix,
         g_pre_ffn, g_post_ffn, w_in, w_conv_qkv, a_log, dt_bias, g_gdn_norm, q_norm, k_norm, b_forget, w_out, w_up,
         w_conv_ffn, b_conv_ffn, w_down)]

    wi = w_in[layer]
    o1 = 4 * aw
    o2 = o1 + 2 * heads
    o3 = o2 + 4 * aw
    w_big = jnp.concatenate([wi[:, :o1], wi[:, o2:o3]], axis=1).astype(BF16)
    w_small = _pad_cols(jnp.concatenate([wi[:, o1:o2], wi[:, o3:]], axis=1), 128)
    ws = _split_hi_lo(w_small)
    zeros_h = jnp.zeros((heads,), F32)
    a_row = _pad_cols(jnp.stack([jnp.concatenate([zeros_h, a_log[layer]]),
                                 jnp.concatenate([zeros_h, dt_bias[layer]])]), 128)
    rows_t = 3 * GROUP
    a_col = jnp.pad(a_row[:, :rows_t].T, ((0, 0), (0, 0)))
    brow = _pad_cols(jnp.concatenate([zeros_h, zeros_h, b_forget[layer]])[None, :], 128)
    bcol = brow[:, :rows_t].T
    assert heads == GROUP and page == 128
    gn = g_gdn_norm[layer].reshape(1, dh)
    qg = q_norm[layer].reshape(1, dh)
    kg = k_norm[layer].reshape(1, dh)
    wo = w_out[layer].astype(BF16)
    w_out_a, w_out_b = wo[:aw], wo[aw:]
    tf = 512
    fp = -(-d_ff // tf) * tf
    wu = w_up[layer]
    w_up_p =jnp.concatenate([_pad_cols(wu[:, :d_ff], fp), _pad_cols(wu[:, d_ff:], fp)], axis=1).astype(BF16)
    wcf = w_conv_ffn[layer]
    w_conv_ffn_p =jnp.concatenate([_pad_cols(wcf[:, :d_ff], fp), _pad_cols(wcf[:, d_ff:], fp)], axis=1)
    bcf = b_conv_ffn[layer][None, :]
    b_conv_ffn_p = jnp.concatenate([_pad_cols(bcf[:, :d_ff], fp), _pad_cols(bcf[:, d_ff:], fp)], axis=1)
    w_down_p = jnp.pad(w_down[layer], ((0, fp - d_ff), (0, 0))).astype(BF16)
    lw = (g_pre_mix[layer], g_post_mix[layer], g_pre_ffn[layer], g_post_ffn[layer], w_big, ws, w_conv_qkv[layer],
          a_row, a_col, gn, qg, kg, brow, bcol, w_out_a, w_out_b, w_up_p, w_conv_ffn_p, b_conv_ffn_p, w_down_p,
          d_ff, tf)

    n_c = b + bd
    c_all = jnp.pad(jnp.concatenate([c_prompt, c_sample], axis=0), ((0, -n_c % GROUP), (0, 0)))
    mod = _ada(c_all, w_ada[layer], b_ada[layer])
    mods_p = [m[:b].reshape(b, 1, d) for m in jnp.split(mod, 6, axis=-1)]
    mods_s = [jnp.repeat(m[b:n_c], GROUP, axis=0).reshape(1, bd * GROUP, d) for m in jnp.split(mod, 6, axis=-1)]

    tm_p = _pick(t, 1024, 128)
    fox_p = lambda qn, knb, vb, cc, cr, proj3: _flash(
        qn, knb, vb, cc, cr[:, 2 * heads:3 * heads].reshape(b, heads, 1, t), proj3, heads=heads)
    zeros_s0 = jnp.zeros((b, heads, dh, dh), F32)
    y_p, proj_p, kn_p, logf_p, gdn_p, up_p = _layer(
        x_prompt, mods_p, None, zeros_s0, None, fox_p, lw, heads=heads, front_pad=0, tm=tm_p)

    front = GROUP - n_new
    x_s = jnp.pad(x_sample, ((0, 0), (front, 0), (0, 0)))
    suffix = _suffix(cache_logf[layer].reshape(n_pool, page * heads), heads=heads)
    suffix3 = suffix.reshape(n_pool, 1, 2 * page * heads)
    fox_s = lambda qn, knb, vb, cc, cr, proj3: _paged(
        page_table, qn, knb, vb, cr, proj3, cache_k, cache_v, suffix3, heads=heads, n_new=n_new)
    st_ffn = state_ffn_conv[layer]
    st_ffn_p =jnp.concatenate([_pad_cols(st_ffn[:, :, :d_ff], fp), _pad_cols(st_ffn[:, :, d_ff:], fp)], axis=-1)
    st_conv = _pad_cols(state_conv_qkv[layer], 8 * aw)
    y_s, proj_s, kn_s, logf_s, gdn_s, up_s = _layer(
        x_s, mods_s, st_conv, state_gdn[layer], st_ffn_p, fox_s, lw, heads=heads, front_pad=front, tm=bd * GROUP)

    n_pg = t // page
    unpad = lambda u: jnp.concatenate([u[..., :d_ff], u[..., fp:fp + d_ff]], axis=-1)
    k_prompt = kn_p.reshape(1, b, n_pg, page, heads, dh)
    v_prompt = proj_p[:, :, 6 * aw:7 * aw].reshape(1, b, n_pg, page, heads, dh)
    logf_prompt = logf_p[:, :, 2 * heads:3 * heads].reshape(1, b, n_pg, page, heads)
    conv_qkv_prompt = proj_p[:, t - (conv_a - 1):, :3 * aw][None]
    ffn_conv_prompt = unpad(up_p[:, t - (ffn_conv - 1):, :])[None]
    k_sample = kn_s[:, front:].reshape(1, bd, n_new, heads, dh)
    v_sample = proj_s[:, front:, 6 * aw:7 * aw].reshape(1, bd, n_new, heads, dh)
    logf_sample = logf_s[:, front:, 2 * heads:3 * heads][None]
    conv_qkv_sample = proj_s[:, GROUP - (conv_a - 1):, :3 * aw][None]
    ffn_conv_sample = unpad(up_s[:, GROUP - (ffn_conv - 1):, :])[None]
    return (y_p, y_s[:, front:], k_prompt, v_prompt, logf_prompt, gdn_p[None], conv_qkv_prompt, ffn_conv_prompt,
            k_sample, v_sample, logf_sample, gdn_s.astype(state_gdn[layer].dtype)[None], conv_qkv_sample, ffn_conv_sample)
```

```python
import functools

import jax
import jax.numpy as jnp
from jax import lax
from jax.experimental import pallas as pl
from jax.experimental.pallas import tpu as pltpu

EPS = 1e-6
F32 = jnp.float32
BF16 = jnp.bfloat16
HI = lax.Precision.HIGHEST
NEG = -1e30
LOG2E = 1.4426950408889634
GDN_CHUNK = 64
GROUP = 8
FFN_ROWS, FFN_LANES = 64, 256
V7X_VMEM_LIMIT = 56 * 1024 * 1024

NN = (((1,), (0,)), ((), ()))
NT = (((1,), (1,)), ((), ()))
TN = (((0,), (0,)), ((), ()))


def _mm(a, b, dims=NN, precision=None):
    return lax.dot_general(a, b, dims, precision=precision, preferred_element_type=F32)


BNN = (((2,), (1,)), ((0,), (0,)))
BNT = (((2,), (2,)), ((0,), (0,)))
BTN = (((1,), (1,)), ((0,), (0,)))


def _bmm(a, b, dims=BNN):
    return lax.dot_general(a, b, dims, preferred_element_type=F32)


def _bmm1(a, b, dims):
    return _bmm(a.astype(BF16), b.astype(BF16), dims)


def _bmm3(a, b, dims):
    free = 2 if dims == BTN else 1
    m = a.shape[free]
    ah = a.astype(BF16).astype(F32)
    bh = b.astype(BF16)
    bl = (b - bh.astype(F32)).astype(BF16)
    stack = jnp.concatenate([ah, a - ah], axis=free).astype(BF16)
    r = _bmm(stack, bh, dims)
    r2 = _bmm(lax.slice_in_dim(stack, 0, m, axis=free), bl, dims)
    return lax.slice_in_dim(r, 0, m, axis=1) + lax.slice_in_dim(r, m, 2 * m, axis=1) + r2


_P_AQ = _P_INV = _P_MRG = _P_UW = _P_WS = _P_O = _P_S = _bmm1


def _pick(n, target, mult):
    best = None
    for d in range(mult, min(n, target) + 1, mult):
        if n % d == 0:
            best = d
    return best if best is not None else n


def _params(sem):
    return pltpu.CompilerParams(dimension_semantics=sem, vmem_limit_bytes=V7X_VMEM_LIMIT)


def _sigmoid(x):
    return 1.0 / (1.0 + jnp.exp(-x))


def _softplus(x):
    return jnp.maximum(x, 0.0) + jnp.log(1.0 + jnp.exp(-jnp.abs(x)))


def _tri(n, kind):
    r = lax.broadcasted_iota(jnp.int32, (n, n), 0)
    c = lax.broadcasted_iota(jnp.int32, (n, n), 1)
    if kind == "lower_incl":
        return (c <= r).astype(F32)
    if kind == "upper_incl":
        return (r <= c).astype(F32)
    raise ValueError(kind)


def _ada_kernel(c_ref, w_ref, b_ref, o_ref):
    c = c_ref[...]
    o_ref[...] = _mm(c * _sigmoid(c), w_ref[...], precision=HI) + b_ref[...]


def _ada(c_all, w_ada, b_ada):
    m, d = c_all.shape
    n = w_ada.shape[1]
    tn = _pick(n, 1024, 128)
    return pl.pallas_call(
        _ada_kernel,
        grid=(n // tn,),
        in_specs=[pl.BlockSpec((m, d), lambda j: (0, 0)),
                  pl.BlockSpec((d, tn), lambda j: (0, j)),
                  pl.BlockSpec((1, tn), lambda j: (0, j))],
        out_specs=pl.BlockSpec((m, tn), lambda j: (0, j)),
        out_shape=jax.ShapeDtypeStruct((m, n), F32),
        compiler_params=_params(("arbitrary",)),
        name="ada",
    )(c_all, w_ada, b_ada.reshape(1, n))


def _norm_proj_kernel(*refs, with_small):
    x_ref, sc_ref, sh_ref, g_ref, w_ref = refs[:5]
    if with_small:
        wsh_ref, wsl_ref, o_ref, os_ref, h_scr = refs[5:]
    else:
        o_ref, h_scr = refs[5:]

    @pl.when(pl.program_id(1) == 0)
    def _():
        x = x_ref[...]
        y = x * lax.rsqrt(jnp.mean(x * x, axis=-1, keepdims=True) + EPS) * g_ref[...]
        h = y * (1.0 + sc_ref[0]) + sh_ref[0]
        hb = h.astype(BF16)
        h_scr[...] = hb
        if with_small:
            hl = (h - hb.astype(F32)).astype(BF16)
            os_ref[...] = _mm(hb, wsh_ref[...]) + _mm(hb, wsl_ref[...]) + _mm(hl, wsh_ref[...])

    o_ref[...] = _mm(h_scr[...], w_ref[...])


def _norm_proj(x2d, sc, sh, g, w, ws=None, *, tm, tiles_per_b):
    r, d = x2d.shape
    n = w.shape[1]
    tn = _pick(n, 1024, 128)
    mr = sc.shape[1]
    mod_spec = pl.BlockSpec((1, mr, d), lambda i, j: (i // tiles_per_b, 0, 0))
    in_specs = [pl.BlockSpec((tm, d), lambda i, j: (i, 0)), mod_spec, mod_spec,
                pl.BlockSpec((1, d), lambda i, j: (0, 0)),
                pl.BlockSpec((d, tn), lambda i, j: (0, j))]
    args = [x2d, sc, sh, g.reshape(1, d), w]
    out_specs = pl.BlockSpec((tm, tn), lambda i, j: (i, j))
    out_shape = jax.ShapeDtypeStruct((r, n), F32)
    if ws is not None:
        ws_hi, ws_lo = ws
        ns = ws_hi.shape[1]
        in_specs += [pl.BlockSpec((d, ns), lambda i, j: (0, 0))] * 2
        args += [ws_hi, ws_lo]
        out_specs = [out_specs, pl.BlockSpec((tm, ns), lambda i, j: (i, 0))]
        out_shape = [out_shape, jax.ShapeDtypeStruct((r, ns), F32)]
    return pl.pallas_call(
        functools.partial(_norm_proj_kernel, with_small=ws is not None),
        grid=(r // tm, n // tn),
        in_specs=in_specs, out_specs=out_specs, out_shape=out_shape,
        scratch_shapes=[pltpu.VMEM((tm, d), BF16)],
        compiler_params=_params(("arbitrary", "arbitrary")),
        name="norm_proj",
    )(*args)


def _shift_rows(xe, s):
    return pltpu.roll(xe, s, 0)[GROUP:]


def _gdn_kernel(qkv_ref, z_ref, sm_ref, smt_ref, wc_ref, arow_ref, acol_ref, gn_ref, s0_ref,
                o_ref, sout_ref, s_scr, prev_scr, *, heads, chunk, n_chunks, front_pad):
    t = pl.program_id(1)
    tc = chunk * n_chunks
    aw = heads * 128

    @pl.when(t == 0)
    def _():
        s_scr[...] = s0_ref[0]
        prev_scr[...] = jnp.zeros_like(prev_scr)

    row = lax.broadcasted_iota(jnp.int32, (tc, 1), 0)
    col = lax.broadcasted_iota(jnp.int32, (1, tc), 1)
    valid_c = row >= front_pad
    valid_r = col >= front_pad

    sm = sm_ref[0]
    g_tile = jnp.where(valid_c, -jnp.exp(arow_ref[0:1, :]) * _softplus(sm + arow_ref[1:2, :]), 0.0)
    beta_tile = jnp.where(valid_c, _sigmoid(sm), 0.0)
    smt = smt_ref[0]
    gt_all = jnp.where(valid_r, -jnp.exp(acol_ref[:, 0:1]) * _softplus(smt + acol_ref[:, 1:2]), 0.0)

    lo_incl = _tri(chunk, "lower_incl")
    up_incl = _tri(chunk, "upper_incl")
    ri = lax.broadcasted_iota(jnp.int32, (chunk, chunk), 0)
    ci = lax.broadcasted_iota(jnp.int32, (chunk, chunk), 1)
    incl = ci <= ri
    strict = ci < ri
    eye = (ci == ri).astype(F32)

    gcol_tiles = [_mm(lo_incl, g_tile[c * chunk:(c + 1) * chunk], precision=HI) for c in range(n_chunks)]
    grow_tiles = [_mm(gt_all[:, c * chunk:(c + 1) * chunk], up_incl, precision=HI) for c in range(n_chunks)]

    wc = wc_ref[...]
    per_head = []
    for h in range(heads):
        parts = []
        for p in range(3):
            lo = p * aw + h * 128
            x = qkv_ref[0, :, lo:lo + 128]
            xe = jnp.concatenate([prev_scr[:, lo:lo + 128], x], axis=0)
            conv = (wc[3:4, lo:lo + 128] * x + wc[2:3, lo:lo + 128] * _shift_rows(xe, 1)
                    + wc[1:2, lo:lo + 128] * _shift_rows(xe, 2) + wc[0:1, lo:lo + 128] * _shift_rows(xe, 3))
            parts.append(conv * _sigmoid(conv))
        q_all, k_all, v_all = parts
        q_all = q_all * lax.rsqrt(jnp.sum(q_all * q_all, axis=-1, keepdims=True) + EPS) * (128.0 ** -0.5)
        k_all = k_all * lax.rsqrt(jnp.sum(k_all * k_all, axis=-1, keepdims=True) + EPS)
        per_head.append((q_all, jnp.where(valid_c, k_all, 0.0), v_all))

    units = [(c, h) for c in range(n_chunks) for h in range(heads)]
    rows = lambda c: slice(c * chunk, (c + 1) * chunk)
    q = jnp.stack([per_head[h][0][rows(c)] for c, h in units])
    k = jnp.stack([per_head[h][1][rows(c)] for c, h in units])
    v = jnp.stack([per_head[h][2][rows(c)] for c, h in units])
    gcol = jnp.stack([gcol_tiles[c][:, heads + h:heads + h + 1] for c, h in units])
    grow = jnp.stack([grow_tiles[c][heads + h:heads + h + 1, :] for c, h in units])
    bcol = jnp.stack([beta_tile[rows(c), h:h + 1] for c, h in units])

    decay = jnp.where(incl, jnp.exp(jnp.where(incl, gcol - grow, 0.0)), 0.0)
    kb = k * bcol
    vb = v * bcol
    aq = _P_AQ(jnp.concatenate([kb, q], axis=1), k, BNT)
    lower = jnp.where(strict, aq[:, :chunk] * decay, 0.0)
    qk = jnp.where(incl, aq[:, chunk:] * decay, 0.0)
    base = min(GROUP, chunk)
    same_blk = lambda s: (ri >> (s.bit_length() - 1)) == (ci >> (s.bit_length() - 1))
    neg_bd = jnp.where(same_blk(base), -lower, 0.0)
    nm = _P_INV(neg_bd, neg_bd, BNN)
    tinv = eye + neg_bd
    n_base = base.bit_length() - 2
    for lvl in range(n_base):
        if lvl < n_base - 1:
            r = _P_INV(jnp.concatenate([nm, tinv], axis=1), nm, BNN)
            tinv = tinv + r[:, chunk:]
            nm = r[:, :chunk]
        else:
            tinv = tinv + _P_INV(tinv, nm, BNN)
    s = base
    while s < chunk:
        off = jnp.where(same_blk(2 * s) & jnp.logical_not(same_blk(s)), lower, 0.0)
        tinv = tinv - _P_MRG(tinv, _P_MRG(off, tinv, BNN), BNN)
        s *= 2
    eg = jnp.exp(gcol)
    uw = _P_UW(tinv, jnp.concatenate([vb, kb * eg], axis=2), BNN)
    wq = jnp.concatenate([uw[:, :, 128:], q * eg], axis=1)
    g_last = gcol[:, chunk - 1:chunk, :]
    kd = k * jnp.exp(g_last - gcol)
    e_last = jnp.exp(g_last)

    s_all = s_scr[...]
    for c in range(n_chunks):
        us = slice(c * heads, (c + 1) * heads)
        ws = _P_WS(wq[us], s_all, BNN)
        v_new = uw[us, :, :128] - ws[:, :chunk]
        o = ws[:, chunk:] + _P_O(qk[us], v_new, BNN)
        s_all = s_all * e_last[us] + _P_S(kd[us], v_new, BTN)
        on = o * lax.rsqrt(jnp.mean(o * o, axis=-1, keepdims=True) + EPS) * gn_ref[...]
        for h in range(heads):
            z = z_ref[0, rows(c), h * 128:(h + 1) * 128]
            o_ref[0, rows(c), h * 128:(h + 1) * 128] = (on[h] * (z * _sigmoid(z))).astype(o_ref.dtype)
    s_scr[...] = s_all

    prev_scr[...] = qkv_ref[0, tc - GROUP:, :]

    @pl.when(t == pl.num_programs(1) - 1)
    def _():
        sout_ref[0] = s_scr[...]


def _gdn(proj3, small3, smallt3, w_conv, a_row, a_col, gn, s0, *, heads, front_pad):
    b, t, _ = proj3.shape
    aw = heads * 128
    chunk = min(GDN_CHUNK, t)
    tc = t if t <= 128 else 128
    n_chunks = tc // chunk
    rows_t = smallt3.shape[1]
    kern = functools.partial(_gdn_kernel, heads=heads, chunk=chunk, n_chunks=n_chunks, front_pad=front_pad)
    return pl.pallas_call(
        kern,
        grid=(b, t // tc),
        in_specs=[pl.BlockSpec((1, tc, 3 * aw), lambda i, j: (i, j, 0)),
                  pl.BlockSpec((1, tc, aw), lambda i, j: (i, j, 3)),
                  pl.BlockSpec((1, tc, 128), lambda i, j: (i, j, 0)),
                  pl.BlockSpec((1, rows_t, tc), lambda i, j: (i, 0, j)),
                  pl.BlockSpec((4, 3 * aw), lambda i, j: (0, 0)),
                  pl.BlockSpec((2, 128), lambda i, j: (0, 0)),
                  pl.BlockSpec((rows_t, 2), lambda i, j: (0, 0)),
                  pl.BlockSpec((1, 128), lambda i, j: (0, 0)),
                  pl.BlockSpec((1, heads, 128, 128), lambda i, j: (i, 0, 0, 0))],
        out_specs=[pl.BlockSpec((1, tc, aw), lambda i, j: (i, j, 0)),
                   pl.BlockSpec((1, heads, 128, 128), lambda i, j: (i, 0, 0, 0))],
        out_shape=[jax.ShapeDtypeStruct((b, t, aw), BF16),
                   jax.ShapeDtypeStruct((b, heads, 128, 128), F32)],
        scratch_shapes=[pltpu.VMEM((heads, 128, 128), F32), pltpu.VMEM((GROUP, 3 * aw), F32)],
        compiler_params=_params(("arbitrary", "arbitrary")),
        name="gdn",
    )(proj3, proj3, small3, smallt3, w_conv, a_row, a_col, gn, s0)


def _fox_prep_kernel(q_ref, k_ref, v_ref, sm_ref, smt_ref, qg_ref, kg_ref, brow_ref, bcol_ref,
                     qn_ref, kn_ref, knb_ref, vb_ref, logf_ref, cr_ref, carry_r,
                     *, heads, front_pad, q_scale):
    t = pl.program_id(1)
    tr = q_ref.shape[1]

    @pl.when(t == 0)
    def _():
        carry_r[...] = jnp.zeros_like(carry_r)

    for h in range(heads):
        cs = slice(h * 128, (h + 1) * 128)
        q = q_ref[0, :, cs]
        qn = q * lax.rsqrt(jnp.mean(q * q, axis=-1, keepdims=True) + EPS) * qg_ref[...]
        qn_ref[0, :, cs] = (qn * q_scale).astype(BF16)
        k = k_ref[0, :, cs]
        kn = k * lax.rsqrt(jnp.mean(k * k, axis=-1, keepdims=True) + EPS) * kg_ref[...]
        kn_ref[0, :, cs] = kn
        knb_ref[0, :, cs] = kn.astype(BF16)
    vb_ref[0] = v_ref[0].astype(BF16)

    col = lax.broadcasted_iota(jnp.int32, (1, tr), 1)
    logf_ref[0] = -_softplus(-(sm_ref[0] + brow_ref[...]))
    logft = jnp.where(col >= front_pad, -_softplus(-(smt_ref[0] + bcol_ref[...])), 0.0)
    cumt = _mm(logft, _tri(tr, "upper_incl"), precision=HI) + carry_r[:, 0:1]
    cr_ref[0] = cumt
    carry_r[...] = jnp.broadcast_to(cumt[:, tr - 1:tr], carry_r.shape)


def _fox_prep(proj3, small3, smallt3, qg, kg, brow, bcol, *, heads, front_pad, q_scale):
    b, t, _ = proj3.shape
    bw = heads * 128
    tr = _pick(t, 256, 128) if t >= 128 else t
    rows_t = smallt3.shape[1]
    wide = lambda blk: pl.BlockSpec((1, tr, bw), lambda i, j, blk=blk: (i, j, blk))
    out_w = pl.BlockSpec((1, tr, bw), lambda i, j: (i, j, 0))
    out_s = pl.BlockSpec((1, tr, 128), lambda i, j: (i, j, 0))
    return pl.pallas_call(
        functools.partial(_fox_prep_kernel, heads=heads, front_pad=front_pad, q_scale=q_scale),
        grid=(b, t // tr),
        in_specs=[wide(4), wide(5), wide(6),
                  pl.BlockSpec((1, tr, 128), lambda i, j: (i, j, 0)),
                  pl.BlockSpec((1, rows_t, tr), lambda i, j: (i, 0, j)),
                  pl.BlockSpec((1, 128), lambda i, j: (0, 0)),
                  pl.BlockSpec((1, 128), lambda i, j: (0, 0)),
                  pl.BlockSpec((1, 128), lambda i, j: (0, 0)),
                  pl.BlockSpec((rows_t, 1), lambda i, j: (0, 0))],
        out_specs=[out_w, out_w, out_w, out_w, out_s,
                   pl.BlockSpec((1, rows_t, tr), lambda i, j: (i, 0, j))],
        out_shape=[jax.ShapeDtypeStruct((b, t, bw), BF16), jax.ShapeDtypeStruct((b, t, bw), F32),
                   jax.ShapeDtypeStruct((b, t, bw), BF16), jax.ShapeDtypeStruct((b, t, bw), BF16),
                   jax.ShapeDtypeStruct((b, t, 128), F32),
                   jax.ShapeDtypeStruct((b, rows_t, t), F32)],
        scratch_shapes=[pltpu.VMEM((rows_t, 128), F32)],
        compiler_params=_params(("arbitrary", "arbitrary")),
        name="fox_prep",
    )(proj3, proj3, proj3, small3, smallt3, qg, kg, brow, bcol)


def _flash_kernel(q_ref, k_ref, v_ref, cr_ref, gate_ref, o_ref, *, tq, tk, n_sub):
    qi = pl.program_id(2)
    ts = tq // n_sub
    qs = [q_ref[0, a * ts:(a + 1) * ts, :] for a in range(n_sub)]
    q0 = qi * tq
    jd = q0 // tk

    def step(j, carry, masked):
        start = pl.multiple_of(j * tk, tk)
        ks = k_ref[0, pl.ds(start, tk), :]
        vs = v_ref[0, pl.ds(start, tk), :]
        ck = cr_ref[0, 0, :, pl.ds(start, tk)] * LOG2E
        ss = [_mm(q, ks, NT) - ck for q in qs]
        if masked:
            ci = lax.broadcasted_iota(jnp.int32, (ts, tk), 1) + start
            ss = [jnp.where(ci <= lax.broadcasted_iota(jnp.int32, (ts, tk), 0) + (q0 + a * ts), s, NEG)
                  for a, s in enumerate(ss)]
        m_new = [jnp.maximum(c[0], jnp.max(s, axis=-1, keepdims=True)) for c, s in zip(carry, ss)]
        ps = [jnp.exp2(s - m) for s, m in zip(ss, m_new)]
        pv = [_mm(p.astype(BF16), vs) for p in ps]
        out = []
        for (m, l, acc), mn, p, o in zip(carry, m_new, ps, pv):
            corr = jnp.exp2(m - mn)
            out.append((mn, l * corr + jnp.sum(p, axis=-1, keepdims=True), acc * corr + o))
        return tuple(out)

    init = tuple((jnp.full((ts, 1), NEG, F32), jnp.zeros((ts, 1), F32), jnp.zeros((ts, 128), F32))
                 for _ in range(n_sub))
    carry = lax.fori_loop(0, jd, lambda j, c: step(j, c, False), init)
    carry = step(jd, carry, True)
    for a, (m, l, acc) in enumerate(carry):
        g = gate_ref[0, a * ts:(a + 1) * ts, :]
        o_ref[0, a * ts:(a + 1) * ts, :] = (acc / l * _sigmoid(g)).astype(o_ref.dtype)


def _flash(qn, knb, vb, cr4, proj3, *, heads):
    b, t, bw = qn.shape
    tk = _pick(t, 512, 128)
    tq = tk
    n_sub = 2 if tq % 256 == 0 else 1
    qspec = pl.BlockSpec((1, tq, 128), lambda i, h, j: (i, j, h))
    kvspec = pl.BlockSpec((1, t, 128), lambda i, h, j: (i, 0, h))
    return pl.pallas_call(
        functools.partial(_flash_kernel, tq=tq, tk=tk, n_sub=n_sub),
        grid=(b, heads, t // tq),
        in_specs=[qspec, kvspec, kvspec,
                  pl.BlockSpec((1, 1, 1, t), lambda i, h, j: (i, h, 0, 0)),
                  pl.BlockSpec((1, tq, 128), lambda i, h, j: (i, j, 7 * heads + h))],
        out_specs=qspec,
        out_shape=jax.ShapeDtypeStruct((b, t, bw), BF16),
        compiler_params=_params(("arbitrary", "arbitrary", "arbitrary")),
        name="fox_flash",
    )(qn, knb, vb, cr4, proj3)


def _suffix_kernel(lf_ref, o_ref, *, heads):
    x = lf_ref[...]
    n = x.shape[1]
    lane = lax.broadcasted_iota(jnp.int32, (1, n), 1)
    incl = x
    tot = x
    d = heads
    while d < n:
        incl = incl + jnp.where(lane < n - d, pltpu.roll(incl, n - d, 1), 0.0)
        tot = tot + pltpu.roll(tot, d, 1)
        d *= 2
    o_ref[:, :n] = incl - x
    o_ref[:, n:] = tot


def _suffix(logf_flat, *, heads):
    n_pool, n = logf_flat.shape
    gp = _pick(n_pool, 256, GROUP)
    return pl.pallas_call(
        functools.partial(_suffix_kernel, heads=heads),
        grid=(n_pool // gp,),
        in_specs=[pl.BlockSpec((gp, n), lambda i: (i, 0))],
        out_specs=pl.BlockSpec((gp, 2 * n), lambda i: (i, 0)),
        out_shape=jax.ShapeDtypeStruct((n_pool, 2 * n), F32),
        compiler_params=_params(("arbitrary",)),
        name="page_suffix",
    )(logf_flat)


def _paged_kernel(pt_ref, *refs, heads, n_new, g_pages):
    del pt_ref
    qn_ref, knb_ref, vb_ref, cr_ref, gate_ref = refs[:5]
    k_refs = refs[5:5 + g_pages]
    v_refs = refs[5 + g_pages:5 + 2 * g_pages]
    s_refs = refs[5 + 2 * g_pages:5 + 3 * g_pages]
    o_ref, q_scr, m_scr, l_scr, acc_scr, tail_scr, cn_scr = refs[5 + 3 * g_pages:]
    p = pl.program_id(1)
    bw = heads * 128
    nr = n_new * heads
    first = GROUP - n_new
    n = tail_scr.shape[1]
    row_head = lax.broadcasted_iota(jnp.int32, (nr, 1), 0) & (heads - 1)

    @pl.when(p == 0)
    def _():
        lane_head = lax.broadcasted_iota(jnp.int32, (heads, bw), 1) // 128
        head_mask = lane_head == lax.broadcasted_iota(jnp.int32, (heads, bw), 0)
        qn = qn_ref[0].astype(F32)
        qbd = jnp.concatenate(
            [jnp.where(head_mask, jnp.broadcast_to(qn[first + i:first + i + 1, :], (heads, bw)), 0.0)
             for i in range(n_new)], axis=0)
        q_all = qbd[:, 0:128]
        for h in range(1, heads):
            q_all = q_all + qbd[:, h * 128:(h + 1) * 128]
        q_scr[...] = q_all.astype(BF16)
        cr = cr_ref[0][2 * heads:3 * heads, :]
        cn_col = jnp.concatenate([cr[:, first + i:first + i + 1] for i in range(n_new)], axis=0)
        cn_scr[...] = jnp.broadcast_to(cn_col, cn_scr.shape)
        cn_key = jnp.concatenate([cr] * n_new, axis=0)
        s = _mm(qbd.astype(BF16), knb_ref[0], NT) + cn_col - cn_key
        ri = lax.broadcasted_iota(jnp.int32, (nr, GROUP), 0) // heads
        ci = lax.broadcasted_iota(jnp.int32, (nr, GROUP), 1)
        s = jnp.where((ci >= first) & (ci - first <= ri), s, NEG)
        m0 = jnp.max(s, axis=-1, keepdims=True)
        p0 = jnp.exp(s - m0)
        m_scr[...] = jnp.broadcast_to(m0, m_scr.shape)
        l_scr[...] = jnp.broadcast_to(jnp.sum(p0, axis=-1, keepdims=True), l_scr.shape)
        full = _mm(p0.astype(BF16), vb_ref[0])
        acc0 = jnp.where(row_head == 0, full[:, 0:128], 0.0)
        for h in range(1, heads):
            acc0 = acc0 + jnp.where(row_head == h, full[:, h * 128:(h + 1) * 128], 0.0)
        acc_scr[...] = acc0
        tail_scr[...] = jnp.zeros_like(tail_scr)

    wide = lambda a: jnp.concatenate([a] * (n // 128), axis=1)
    q_all = q_scr[...]
    cn_w = wide(cn_scr[...])
    m = m_scr[...]
    l = l_scr[...]
    acc = acc_scr[...]
    tail = tail_scr[...]
    valid = (lax.broadcasted_iota(jnp.int32, (nr, n), 1) & (heads - 1)) == row_head
    scores = []
    for i in range(g_pages):
        blk = s_refs[i][0]
        bias = blk[:, :n] + tail
        tail = tail + blk[:, n:]
        kf = k_refs[i][0, 0].reshape(n, 128).astype(BF16)
        scores.append(jnp.where(valid, _mm(q_all, kf, NT) + bias + cn_w, NEG))
    s_max = scores[0]
    for s in scores[1:]:
        s_max = jnp.maximum(s_max, s)
    m_new = jnp.maximum(m, jnp.max(s_max, axis=-1, keepdims=True))
    corr = jnp.exp(m - m_new)
    m_w = wide(m_new)
    probs = [jnp.exp(s - m_w) for s in scores]
    p_sum = probs[0]
    for pr in probs[1:]:
        p_sum = p_sum + pr
    pv = _mm(probs[0].astype(BF16), v_refs[0][0, 0].reshape(n, 128).astype(BF16))
    for i in range(1, g_pages):
        pv = pv + _mm(probs[i].astype(BF16), v_refs[i][0, 0].reshape(n, 128).astype(BF16))
    l = l * corr + jnp.sum(p_sum, axis=-1, keepdims=True)
    acc = acc * corr + pv
    m_scr[...] = m_new
    l_scr[...] = l
    acc_scr[...] = acc
    tail_scr[...] = tail

    @pl.when(p == pl.num_programs(1) - 1)
    def _():
        o = jnp.concatenate([acc / l] * heads, axis=1)
        keep = (lax.broadcasted_iota(jnp.int32, (nr, bw), 1) // 128) == row_head
        o = jnp.where(keep, o, 0.0)
        rows = [jnp.zeros((first, bw), F32)]
        for i in range(n_new):
            rows.append(jnp.sum(o[i * heads:(i + 1) * heads, :], axis=0, keepdims=True))
        out = jnp.concatenate(rows, axis=0)
        o_ref[0] = (out * _sigmoid(gate_ref[0])).astype(o_ref.dtype)


def _paged(page_table, qn, knb, vb, cr, proj3, cache_k, cache_v, suffix3, *, heads, n_new):
    bd, _, bw = qn.shape
    n_pages = page_table.shape[1]
    page = cache_k.shape[2]
    n = page * heads
    g_pages = _pick(n_pages, 8, 1)
    rows_t = cr.shape[1]
    nr = n_new * heads
    seq = lambda blk_w, blk: pl.BlockSpec((1, GROUP, blk_w), lambda b, p, pt, blk=blk: (b, 0, blk))
    page_of = lambda b, p, pt, i: pt[b, n_pages - 1 - (p * g_pages + i)]
    kv_spec = lambda i: pl.BlockSpec((1, 1, page, heads, 128),
                                     lambda b, p, pt, i=i: (0, page_of(b, p, pt, i), 0, 0, 0))
    suf_spec = lambda i: pl.BlockSpec((1, 1, 2 * n), lambda b, p, pt, i=i: (page_of(b, p, pt, i), 0, 0))

    in_specs = [seq(bw, 0), seq(bw, 0), seq(bw, 0),
                pl.BlockSpec((1, rows_t, GROUP), lambda b, p, pt: (b, 0, 0)),
                seq(bw, 7)]
    in_specs += [kv_spec(i) for i in range(g_pages)] * 2
    in_specs += [suf_spec(i) for i in range(g_pages)]
    grid_spec = pltpu.PrefetchScalarGridSpec(
        num_scalar_prefetch=1,
        grid=(bd, n_pages // g_pages),
        in_specs=in_specs,
        out_specs=pl.BlockSpec((1, GROUP, bw), lambda b, p, pt: (b, 0, 0)),
        scratch_shapes=[pltpu.VMEM((nr, 128), BF16), pltpu.VMEM((nr, 128), F32), pltpu.VMEM((nr, 128), F32),
                        pltpu.VMEM((nr, 128), F32), pltpu.VMEM((1, n), F32), pltpu.VMEM((nr, 128), F32)],
    )
    return pl.pallas_call(
        functools.partial(_paged_kernel, heads=heads, n_new=n_new, g_pages=g_pages),
        grid_spec=grid_spec,
        out_shape=jax.ShapeDtypeStruct((bd, GROUP, bw), BF16),
        compiler_params=_params(("arbitrary", "arbitrary")),
        name="fox_paged",
    )(page_table, qn, knb, vb, cr, proj3, *([cache_k] * g_pages), *([cache_v] * g_pages), *([suffix3] * g_pages))


def _out_proj_kernel(oa_ref, ob_ref, wa_ref, wb_ref, x_ref, gt_ref, g_ref, o_ref):
    mix = _mm(oa_ref[...], wa_ref[...]) + _mm(ob_ref[...], wb_ref[...])
    normed = mix * lax.rsqrt(jnp.mean(mix * mix, axis=-1, keepdims=True) + EPS) * g_ref[...]
    o_ref[...] = x_ref[...] + gt_ref[0] * normed


def _out_proj(oa, ob, wa, wb, x2d, gt, g, *, tm, tiles_per_b):
    r, d = x2d.shape
    aw, bw = oa.shape[1], ob.shape[1]
    mr = gt.shape[1]
    return pl.pallas_call(
        _out_proj_kernel,
        grid=(r // tm,),
        in_specs=[pl.BlockSpec((tm, aw), lambda i: (i, 0)),
                  pl.BlockSpec((tm, bw), lambda i: (i, 0)),
                  pl.BlockSpec((aw, d), lambda i: (0, 0)),
                  pl.BlockSpec((bw, d), lambda i: (0, 0)),
                  pl.BlockSpec((tm, d), lambda i: (i, 0)),
                  pl.BlockSpec((1, mr, d), lambda i: (i // tiles_per_b, 0, 0)),
                  pl.BlockSpec((1, d), lambda i: (0, 0))],
        out_specs=pl.BlockSpec((tm, d), lambda i: (i, 0)),
        out_shape=jax.ShapeDtypeStruct((r, d), F32),
        compiler_params=_params(("arbitrary",)),
        name="out_proj",
    )(oa, ob, wa, wb, x2d, gt, g.reshape(1, d))


def _ffn_tail_kernel(ug_ref, uv_ref, hg_ref, hv_ref, wcg_ref, wcv_ref, bg_ref, bv_ref, wd_ref, x_ref, gt_ref, g_ref,
                     o_ref, acc_scr, *, tiles_per_b):
    i = pl.program_id(0)
    j = pl.program_id(1)
    first = (i % tiles_per_b) == 0

    def conv(u_ref, halo_ref, wc_ref, b_ref):
        x = u_ref[...]
        halo = jnp.where(first, 0.0, halo_ref[...])
        xe = jnp.concatenate([halo, x], axis=0)
        wc = wc_ref[...]
        return wc[2:3] * x + wc[1:2] * _shift_rows(xe, 1) + wc[0:1] * _shift_rows(xe, 2) + b_ref[...]

    gate = conv(ug_ref, hg_ref, wcg_ref, bg_ref)
    val = conv(uv_ref, hv_ref, wcv_ref, bv_ref)
    act = (gate * _sigmoid(gate) * val).astype(BF16)
    part = _mm(act, wd_ref[...])

    @pl.when(j == 0)
    def _():
        acc_scr[...] = part

    @pl.when(j > 0)
    def _():
        acc_scr[...] += part

    @pl.when(j == pl.num_programs(1) - 1)
    def _():
        y = acc_scr[...]
        normed = y * lax.rsqrt(jnp.mean(y * y, axis=-1, keepdims=True) + EPS) * g_ref[...]
        o_ref[...] = x_ref[...] + gt_ref[0] * normed


def _ffn_tail(up, wc, bc, wd, x2d, gt, g, *, tm, tiles_per_b, tf):
    r, d = x2d.shape
    fp = wd.shape[0]
    nf = fp // tf
    hb = tm // GROUP
    mr = gt.shape[1]
    halo = lambda off: pl.BlockSpec((GROUP, tf), lambda i, j, off=off: (jnp.maximum(i * hb - 1, 0), j + off))
    return pl.pallas_call(
        functools.partial(_ffn_tail_kernel, tiles_per_b=tiles_per_b),
        grid=(r // tm, nf),
        in_specs=[pl.BlockSpec((tm, tf), lambda i, j: (i, j)),
                  pl.BlockSpec((tm, tf), lambda i, j: (i, j + nf)),
                  halo(0), halo(nf),
                  pl.BlockSpec((3, tf), lambda i, j: (0, j)),
                  pl.BlockSpec((3, tf), lambda i, j: (0, j + nf)),
                  pl.BlockSpec((1, tf), lambda i, j: (0, j)),
                  pl.BlockSpec((1, tf), lambda i, j: (0, j + nf)),
                  pl.BlockSpec((tf, d), lambda i, j: (j, 0)),
                  pl.BlockSpec((tm, d), lambda i, j: (i, 0)),
                  pl.BlockSpec((1, mr, d), lambda i, j: (i // tiles_per_b, 0, 0)),
                  pl.BlockSpec((1, d), lambda i, j: (0, 0))],
        out_specs=pl.BlockSpec((tm, d), lambda i, j: (i, 0)),
        out_shape=jax.ShapeDtypeStruct((r, d), F32),
        scratch_shapes=[pltpu.VMEM((tm, d), F32)],
        compiler_params=_params(("arbitrary", "arbitrary")),
        name="ffn_tail",
    )(up, up, up, up, wc, wc, bc, bc, wd, x2d, gt, g.reshape(1, d))


def _ffn_fused_kernel(x_ref, sc_ref, sh_ref, gpre_ref, wg_ref, wv_ref, wcg_ref, wcv_ref, bg_ref, bv_ref, wd_ref,
                      gt_ref, gpost_ref, o_ref, lg_ref, lv_ref, h_scr, acc_scr, cg_scr, cv_scr, xg_scr, xv_scr, act_scr,
                      *, tiles_per_b):
    i = pl.program_id(0)
    j = pl.program_id(1)
    tm = x_ref.shape[0]
    first = (i % tiles_per_b) == 0

    @pl.when(j == 0)
    def _():
        x = x_ref[...]
        y = x * lax.rsqrt(jnp.mean(x * x, axis=-1, keepdims=True) + EPS) * gpre_ref[...]
        h_scr[...] = (y * (1.0 + sc_ref[0]) + sh_ref[0]).astype(BF16)

    h = h_scr[...]
    tf = wd_ref.shape[0]

    def project(w_ref, xe_scr, carry_scr, last_ref):
        xe_scr[0:GROUP, :] = jnp.where(first, 0.0, carry_scr[j])
        xe_scr[GROUP:, :] = _mm(h, w_ref[...])
        last = xe_scr[tm:tm + GROUP, :]
        carry_scr[j] = last
        last_ref[0] = last

    project(wg_ref, xg_scr, cg_scr, lg_ref)
    project(wv_ref, xv_scr, cv_scr, lv_ref)

    def conv(xe_scr, wc_ref, b_ref, r0, c0):
        xe = xe_scr[r0:r0 + FFN_ROWS + GROUP, c0:c0 + FFN_LANES]
        wc = wc_ref[:, c0:c0 + FFN_LANES]
        return (wc[2:3] * xe[GROUP:] + wc[1:2] * _shift_rows(xe, 1) + wc[0:1] * _shift_rows(xe, 2)
                + b_ref[:, c0:c0 + FFN_LANES])

    for r0 in range(0, tm, FFN_ROWS):
        for c0 in range(0, tf, FFN_LANES):
            gate = conv(xg_scr, wcg_ref, bg_ref, r0, c0)
            val = conv(xv_scr, wcv_ref, bv_ref, r0, c0)
            act_scr[r0:r0 + FFN_ROWS, c0:c0 + FFN_LANES] = (gate * _sigmoid(gate) * val).astype(BF16)
    part = _mm(act_scr[...], wd_ref[...])

    @pl.when(j == 0)
    def _():
        acc_scr[...] = part

    @pl.when(j > 0)
    def _():
        acc_scr[...] += part

    @pl.when(j == pl.num_programs(1) - 1)
    def _():
        y = acc_scr[...]
        normed = y * lax.rsqrt(jnp.mean(y * y, axis=-1, keepdims=True) + EPS) * gpost_ref[...]
        o_ref[...] = x_ref[...] + gt_ref[0] * normed


def _ffn_fused(x2d, sc, sh, gt, g_pre, g_post, w_up, wc, bc, wd, *, nb, tm, tf):
    r, d = x2d.shape
    fp = wd.shape[0]
    nf = fp // tf
    tiles_per_b = r // nb // tm
    mod = pl.BlockSpec((1, 1, d), lambda i, j: (i // tiles_per_b, 0, 0))
    vec = pl.BlockSpec((1, d), lambda i, j: (0, 0))
    col = lambda rows, off: pl.BlockSpec((rows, tf), lambda i, j, off=off: (0, j + off))
    last = pl.BlockSpec((1, GROUP, tf), lambda i, j: (i, 0, j))
    return pl.pallas_call(
        functools.partial(_ffn_fused_kernel, tiles_per_b=tiles_per_b),
        grid=(r // tm, nf),
        in_specs=[pl.BlockSpec((tm, d), lambda i, j: (i, 0)), mod, mod, vec,
                  col(d, 0), col(d, nf), col(3, 0), col(3, nf), col(1, 0), col(1, nf),
                  pl.BlockSpec((tf, d), lambda i, j: (j, 0)), mod, vec],
        out_specs=[pl.BlockSpec((tm, d), lambda i, j: (i, 0)), last, last],
        out_shape=[jax.ShapeDtypeStruct((r, d), F32), jax.ShapeDtypeStruct((r // tm, GROUP, fp), F32),
                   jax.ShapeDtypeStruct((r // tm, GROUP, fp), F32)],
        scratch_shapes=[pltpu.VMEM((tm, d), BF16), pltpu.VMEM((tm, d), F32),
                        pltpu.VMEM((nf, GROUP, tf), F32), pltpu.VMEM((nf, GROUP, tf), F32),
                        pltpu.VMEM((tm + GROUP, tf), F32), pltpu.VMEM((tm + GROUP, tf), F32),
                        pltpu.VMEM((tm, tf), BF16)],
        compiler_params=_params(("arbitrary", "arbitrary")),
        name="ffn_fused",
    )(x2d, sc, sh, g_pre.reshape(1, d), w_up, w_up, wc, wc, bc, bc, wd, gt, g_post.reshape(1, d))


def _pad_cols(a, n):
    return jnp.pad(a, [(0, 0)] * (a.ndim - 1) + [(0, n - a.shape[-1])])


def _split_hi_lo(w):
    hi = w.astype(BF16)
    return hi, (w - hi.astype(F32)).astype(BF16)


def _layer(x3, mods, st_conv, st_gdn, st_ffn, fox, lw, *, heads, front_pad, tm, q_scale):
    (g_pre_mix, g_post_mix, g_pre_ffn, g_post_ffn, w_big, ws, w_conv_qkv, a_row, a_col, gn, qg, kg, brow, bcol,
     w_out_a, w_out_b, w_up, w_conv_ffn, b_conv_ffn, w_down, d_ff, tf) = lw
    sh_m, sc_m, gt_m, sh_f, sc_f, gt_f = mods
    nb, t, d = x3.shape
    aw = heads * 128
    x2d = x3.reshape(nb * t, d)
    tiles_per_b = max(t // tm, 1) if mods[0].shape[1] == 1 else 1

    proj, small = _norm_proj(x2d, sc_m, sh_m, g_pre_mix, w_big, ws, tm=tm, tiles_per_b=tiles_per_b)
    proj3 = proj.reshape(nb, t, 8 * aw)
    if st_conv is not None:
        k = st_conv.shape[1]
        proj3 = lax.dynamic_update_slice(proj3, st_conv, (0, GROUP - (t - front_pad) - k, 0))
    small3 = small.reshape(nb, t, 128)
    rows_t = 3 * GROUP
    smallt3 = jnp.swapaxes(small3[:, :, :rows_t], 1, 2)

    o_a, gdn_new = _gdn(proj3, small3, smallt3, w_conv_qkv, a_row, a_col, gn, st_gdn,
                        heads=heads, front_pad=front_pad)
    qn, kn, knb, vb, logf, cr = _fox_prep(proj3, small3, smallt3, qg, kg, brow, bcol,
                                               heads=heads, front_pad=front_pad, q_scale=q_scale)
    o_b = fox(qn, knb, vb, cr, proj3)

    x1 = _out_proj(o_a.reshape(nb * t, aw), o_b.reshape(nb * t, aw), w_out_a, w_out_b, x2d, gt_m, g_post_mix,
                   tm=min(tm, 512), tiles_per_b=max(t // min(tm, 512), 1) if mods[0].shape[1] == 1 else 1)
    fp = w_down.shape[0]
    unpad = lambda g, v: jnp.concatenate([g[..., :d_ff], v[..., :d_ff]], axis=-1)
    if st_ffn is None:
        y, last_g, last_v = _ffn_fused(x1, sc_f, sh_f, gt_f, g_pre_ffn, g_post_ffn, w_up, w_conv_ffn, b_conv_ffn,
                                       w_down, nb=nb, tm=min(tm, 512), tf=tf)
        per_b = last_g.shape[0] // nb
        up_last = unpad(last_g[per_b - 1::per_b], last_v[per_b - 1::per_b])
    else:
        up = _norm_proj(x1, sc_f, sh_f, g_pre_ffn, w_up, tm=tm, tiles_per_b=tiles_per_b)
        up3 = up.reshape(nb, t, 2 * fp)
        k = st_ffn.shape[1]
        up3 = lax.dynamic_update_slice(up3, st_ffn, (0, GROUP - (t - front_pad) - k, 0))
        y = _ffn_tail(up3.reshape(nb * t, 2 * fp), w_conv_ffn, b_conv_ffn, w_down, x1, gt_f, g_post_ffn,
                      tm=min(tm, 512), tiles_per_b=1, tf=tf)
        up_last = unpad(up3[:, t - GROUP:, :fp], up3[:, t - GROUP:, fp:])
    return y.reshape(nb, t, d), proj3, kn, logf, gdn_new, up_last


def kernel(x_prompt, x_sample, cache_k, cache_v, cache_logf, state_gdn, state_conv_qkv, state_ffn_conv, page_table, c_prompt, c_sample, w_ada, b_ada, g_pre_mix, g_post_mix, g_pre_ffn, g_post_ffn, w_in, w_conv_qkv, a_log, dt_bias, g_gdn_norm, q_norm, k_norm, b_forget, w_out, w_up, w_conv_ffn, b_conv_ffn, w_down):
    depth = w_ada.shape[0]
    assert depth == 1, "single-layer step"
    b, t, d = x_prompt.shape
    bd, n_new, _ = x_sample.shape
    heads = state_gdn.shape[2]
    dh = state_gdn.shape[3]
    assert dh == 128 and cache_k.shape[3] == heads and n_new <= GROUP // 2
    aw = heads * dh
    page = cache_k.shape[2]
    n_pool = cache_k.shape[1]
    d_ff = w_down.shape[1]
    conv_a = w_conv_qkv.shape[1]
    ffn_conv = w_conv_ffn.shape[1]
    assert conv_a == 4 and ffn_conv == 3
    layer = 0
    (cache_logf, state_gdn, state_conv_qkv, state_ffn_conv, w_ada, b_ada, g_pre_mix, g_post_mix,
     g_pre_ffn, g_post_ffn, w_in, w_conv_qkv, a_log, dt_bias, g_gdn_norm, q_norm, k_norm, b_forget, w_out, w_up,
     w_conv_ffn, b_conv_ffn, w_down) = [
        (a.reshape(a.shape[1:]),) for a in
        (cache_logf, state_gdn, state_conv_qkv, state_ffn_conv, w_ada, b_ada, g_pre_mix, g_post_mix,
         g_pre_ffn, g_post_ffn, w_in, w_conv_qkv, a_log, dt_bias, g_gdn_norm, q_norm, k_norm, b_forget, w_out, w_up,
         w_conv_ffn, b_conv_ffn, w_down)]

    wi = w_in[layer]
    o1 = 4 * aw
    o2 = o1 + 2 * heads
    o3 = o2 + 4 * aw
    w_big = jnp.concatenate([wi[:, :o1], wi[:, o2:o3]], axis=1).astype(BF16)
    w_small = _pad_cols(jnp.concatenate([wi[:, o1:o2], wi[:, o3:]], axis=1), 128)
    ws = _split_hi_lo(w_small)
    zeros_h = jnp.zeros((heads,), F32)
    a_row = _pad_cols(jnp.stack([jnp.concatenate([zeros_h, a_log[layer]]),
                                 jnp.concatenate([zeros_h, dt_bias[layer]])]), 128)
    rows_t = 3 * GROUP
    a_col = jnp.pad(a_row[:, :rows_t].T, ((0, 0), (0, 0)))
    brow = _pad_cols(jnp.concatenate([zeros_h, zeros_h, b_forget[layer]])[None, :], 128)
    bcol = brow[:, :rows_t].T
    assert heads == GROUP and page == 128
    gn = g_gdn_norm[layer].reshape(1, dh)
    qg = q_norm[layer].reshape(1, dh)
    kg = k_norm[layer].reshape(1, dh)
    wo = w_out[layer].astype(BF16)
    w_out_a, w_out_b = wo[:aw], wo[aw:]
    tf = 512
    fp = -(-d_ff // tf) * tf
    wu = w_up[layer]
    w_up_p =jnp.concatenate([_pad_cols(wu[:, :d_ff], fp), _pad_cols(wu[:, d_ff:], fp)], axis=1).astype(BF16)
    wcf = w_conv_ffn[layer]
    w_conv_ffn_p =jnp.concatenate([_pad_cols(wcf[:, :d_ff], fp), _pad_cols(wcf[:, d_ff:], fp)], axis=1)
    bcf = b_conv_ffn[layer][None, :]
    b_conv_ffn_p = jnp.concatenate([_pad_cols(bcf[:, :d_ff], fp), _pad_cols(bcf[:, d_ff:], fp)], axis=1)
    w_down_p = jnp.pad(w_down[layer], ((0, fp - d_ff), (0, 0))).astype(BF16)
    lw = (g_pre_mix[layer], g_post_mix[layer], g_pre_ffn[layer], g_post_ffn[layer], w_big, ws, w_conv_qkv[layer],
          a_row, a_col, gn, qg, kg, brow, bcol, w_out_a, w_out_b, w_up_p, w_conv_ffn_p, b_conv_ffn_p, w_down_p,
          d_ff, tf)

    n_c = b + bd
    c_all = jnp.pad(jnp.concatenate([c_prompt, c_sample], axis=0), ((0, -n_c % GROUP), (0, 0)))
    mod = _ada(c_all, w_ada[layer], b_ada[layer])
    mods_p = [m[:b].reshape(b, 1, d) for m in jnp.split(mod, 6, axis=-1)]
    mods_s = [jnp.repeat(m[b:n_c], GROUP, axis=0).reshape(1, bd * GROUP, d) for m in jnp.split(mod, 6, axis=-1)]

    tm_p = _pick(t, 1024, 128)
    fox_p = lambda qn, knb, vb, cr, proj3: _flash(
        qn, knb, vb, cr[:, 2 * heads:3 * heads].reshape(b, heads, 1, t), proj3, heads=heads)
    zeros_s0 = jnp.zeros((b, heads, dh, dh), F32)
    y_p, proj_p, kn_p, logf_p, gdn_p, up_p = _layer(
        x_prompt, mods_p, None, zeros_s0, None, fox_p, lw, heads=heads, front_pad=0, tm=tm_p,
        q_scale=dh ** -0.5 * LOG2E)

    front = GROUP - n_new
    x_s = jnp.pad(x_sample, ((0, 0), (front, 0), (0, 0)))
    suffix = _suffix(cache_logf[layer].reshape(n_pool, page * heads), heads=heads)
    suffix3 = suffix.reshape(n_pool, 1, 2 * page * heads)
    fox_s = lambda qn, knb, vb, cr, proj3: _paged(
        page_table, qn, knb, vb, cr, proj3, cache_k, cache_v, suffix3, heads=heads, n_new=n_new)
    st_ffn = state_ffn_conv[layer]
    st_ffn_p =jnp.concatenate([_pad_cols(st_ffn[:, :, :d_ff], fp), _pad_cols(st_ffn[:, :, d_ff:], fp)], axis=-1)
    st_conv = _pad_cols(state_conv_qkv[layer], 8 * aw)
    y_s, proj_s, kn_s, logf_s, gdn_s, up_s = _layer(
        x_s, mods_s, st_conv, state_gdn[layer], st_ffn_p, fox_s, lw, heads=heads, front_pad=front, tm=bd * GROUP,
        q_scale=dh ** -0.5)

    n_pg = t // page
    k_prompt = kn_p.reshape(1, b, n_pg, page, heads, dh)
    v_prompt = proj_p[:, :, 6 * aw:7 * aw].reshape(1, b, n_pg, page, heads, dh)
    logf_prompt = logf_p[:, :, 2 * heads:3 * heads].reshape(1, b, n_pg, page, heads)
    conv_qkv_prompt = proj_p[:, t - (conv_a - 1):, :3 * aw][None]
    ffn_conv_prompt = up_p[:, GROUP - (ffn_conv - 1):, :][None]
    k_sample = kn_s[:, front:].reshape(1, bd, n_new, heads, dh)
    v_sample = proj_s[:, front:, 6 * aw:7 * aw].reshape(1, bd, n_new, heads, dh)
    logf_sample = logf_s[:, front:, 2 * heads:3 * heads][None]
    conv_qkv_sample = proj_s[:, GROUP - (conv_a - 1):, :3 * aw][None]
    ffn_conv_sample = up_s[:, GROUP - (ffn_conv - 1):, :][None]
    return (y_p, y_s[:, front:], k_prompt, v_prompt, logf_prompt, gdn_p[None], conv_qkv_prompt, ffn_conv_prompt,
            k_sample, v_sample, logf_sample, gdn_s.astype(state_gdn[layer].dtype)[None], conv_qkv_sample, ffn_conv_sample)
```

```python
import functools

import jax
import jax.numpy as jnp
from jax import lax
from jax.experimental import pallas as pl
from jax.experimental.pallas import tpu as pltpu

EPS = 1e-6
F32 = jnp.float32
BF16 = jnp.bfloat16
HI = lax.Precision.HIGHEST
NEG = -1e30
LOG2E = 1.4426950408889634
GDN_CHUNK = 64
GROUP = 8
FFN_ROWS, FFN_LANES = 64, 256
V7X_VMEM_LIMIT = 56 * 1024 * 1024

NN = (((1,), (0,)), ((), ()))
NT = (((1,), (1,)), ((), ()))
TN = (((0,), (0,)), ((), ()))


def _mm(a, b, dims=NN, precision=None):
    return lax.dot_general(a, b, dims, precision=precision, preferred_element_type=F32)


BNN = (((2,), (1,)), ((0,), (0,)))
BNT = (((2,), (2,)), ((0,), (0,)))
BTN = (((1,), (1,)), ((0,), (0,)))


def _bmm(a, b, dims=BNN):
    return lax.dot_general(a, b, dims, preferred_element_type=F32)


def _bmm1(a, b, dims):
    return _bmm(a.astype(BF16), b.astype(BF16), dims)


def _bmm3(a, b, dims):
    free = 2 if dims == BTN else 1
    m = a.shape[free]
    ah = a.astype(BF16).astype(F32)
    bh = b.astype(BF16)
    bl = (b - bh.astype(F32)).astype(BF16)
    stack = jnp.concatenate([ah, a - ah], axis=free).astype(BF16)
    r = _bmm(stack, bh, dims)
    r2 = _bmm(lax.slice_in_dim(stack, 0, m, axis=free), bl, dims)
    return lax.slice_in_dim(r, 0, m, axis=1) + lax.slice_in_dim(r, m, 2 * m, axis=1) + r2


_P_AQ = _P_INV = _P_MRG = _P_UW = _P_WS = _P_O = _P_S = _bmm1


def _pick(n, target, mult):
    best = None
    for d in range(mult, min(n, target) + 1, mult):
        if n % d == 0:
            best = d
    return best if best is not None else n


def _params(sem):
    return pltpu.CompilerParams(dimension_semantics=sem, vmem_limit_bytes=V7X_VMEM_LIMIT)


def _sigmoid(x):
    return 1.0 / (1.0 + jnp.exp(-x))


def _softplus(x):
    return jnp.maximum(x, 0.0) + jnp.log(1.0 + jnp.exp(-jnp.abs(x)))


def _tri(n, kind):
    r = lax.broadcasted_iota(jnp.int32, (n, n), 0)
    c = lax.broadcasted_iota(jnp.int32, (n, n), 1)
    if kind == "lower_incl":
        return (c <= r).astype(F32)
    if kind == "upper_incl":
        return (r <= c).astype(F32)
    raise ValueError(kind)


def _ada_kernel(c_ref, w_ref, b_ref, o_ref):
    c = c_ref[...]
    o_ref[...] = _mm(c * _sigmoid(c), w_ref[...], precision=HI) + b_ref[...]


def _ada(c_all, w_ada, b_ada):
    m, d = c_all.shape
    n = w_ada.shape[1]
    tn = _pick(n, 1024, 128)
    return pl.pallas_call(
        _ada_kernel,
        grid=(n // tn,),
        in_specs=[pl.BlockSpec((m, d), lambda j: (0, 0)),
                  pl.BlockSpec((d, tn), lambda j: (0, j)),
                  pl.BlockSpec((1, tn), lambda j: (0, j))],
        out_specs=pl.BlockSpec((m, tn), lambda j: (0, j)),
        out_shape=jax.ShapeDtypeStruct((m, n), F32),
        compiler_params=_params(("arbitrary",)),
        name="ada",
    )(c_all, w_ada, b_ada.reshape(1, n))


def _norm_proj_kernel(*refs, with_small):
    x_ref, sc_ref, sh_ref, g_ref, w_ref = refs[:5]
    if with_small:
        wsh_ref, wsl_ref, o_ref, os_ref, h_scr = refs[5:]
    else:
        o_ref, h_scr = refs[5:]

    @pl.when(pl.program_id(1) == 0)
    def _():
        x = x_ref[...]
        y = x * lax.rsqrt(jnp.mean(x * x, axis=-1, keepdims=True) + EPS) * g_ref[...]
        h = y * (1.0 + sc_ref[0]) + sh_ref[0]
        hb = h.astype(BF16)
        h_scr[...] = hb
        if with_small:
            hl = (h - hb.astype(F32)).astype(BF16)
            os_ref[...] = _mm(hb, wsh_ref[...]) + _mm(hb, wsl_ref[...]) + _mm(hl, wsh_ref[...])

    o_ref[...] = _mm(h_scr[...], w_ref[0])


def _norm_proj(x2d, sc, sh, g, w, ws=None, *, tm, tiles_per_b):
    r, d = x2d.shape
    n = w.shape[1]
    tn = _pick(n, 1024, 128)
    mr = sc.shape[1]
    mod_spec = pl.BlockSpec((1, mr, d), lambda i, j: (i // tiles_per_b, 0, 0))
    w_t = jnp.swapaxes(w.reshape(d, n // tn, tn), 0, 1)
    in_specs = [pl.BlockSpec((tm, d), lambda i, j: (i, 0)), mod_spec, mod_spec,
                pl.BlockSpec((1, d), lambda i, j: (0, 0)),
                pl.BlockSpec((1, d, tn), lambda i, j: (j, 0, 0))]
    args = [x2d, sc, sh, g.reshape(1, d), w_t]
    out_specs = pl.BlockSpec((tm, tn), lambda i, j: (i, j))
    out_shape = jax.ShapeDtypeStruct((r, n), F32)
    if ws is not None:
        ws_hi, ws_lo = ws
        ns = ws_hi.shape[1]
        in_specs += [pl.BlockSpec((d, ns), lambda i, j: (0, 0))] * 2
        args += [ws_hi, ws_lo]
        out_specs = [out_specs, pl.BlockSpec((tm, ns), lambda i, j: (i, 0))]
        out_shape = [out_shape, jax.ShapeDtypeStruct((r, ns), F32)]
    return pl.pallas_call(
        functools.partial(_norm_proj_kernel, with_small=ws is not None),
        grid=(r // tm, n // tn),
        in_specs=in_specs, out_specs=out_specs, out_shape=out_shape,
        scratch_shapes=[pltpu.VMEM((tm, d), BF16)],
        compiler_params=_params(("arbitrary", "arbitrary")),
        name="norm_proj",
    )(*args)


def _shift_rows(xe, s):
    return pltpu.roll(xe, s, 0)[GROUP:]


def _gdn_kernel(qkv_ref, z_ref, sm_ref, smt_ref, wc_ref, arow_ref, acol_ref, gn_ref, s0_ref,
                o_ref, sout_ref, s_scr, prev_scr, *, heads, chunk, n_chunks, front_pad):
    t = pl.program_id(1)
    tc = chunk * n_chunks
    aw = heads * 128

    @pl.when(t == 0)
    def _():
        s_scr[...] = s0_ref[0]
        prev_scr[...] = jnp.zeros_like(prev_scr)

    row = lax.broadcasted_iota(jnp.int32, (tc, 1), 0)
    col = lax.broadcasted_iota(jnp.int32, (1, tc), 1)
    valid_c = row >= front_pad
    valid_r = col >= front_pad

    sm = sm_ref[0]
    g_tile = jnp.where(valid_c, -jnp.exp(arow_ref[0:1, :]) * _softplus(sm + arow_ref[1:2, :]), 0.0)
    beta_tile = jnp.where(valid_c, _sigmoid(sm), 0.0)
    smt = smt_ref[0]
    gt_all = jnp.where(valid_r, -jnp.exp(acol_ref[:, 0:1]) * _softplus(smt + acol_ref[:, 1:2]), 0.0)

    lo_incl = _tri(chunk, "lower_incl")
    up_incl = _tri(chunk, "upper_incl")
    ri = lax.broadcasted_iota(jnp.int32, (chunk, chunk), 0)
    ci = lax.broadcasted_iota(jnp.int32, (chunk, chunk), 1)
    incl = ci <= ri
    strict = ci < ri
    eye = (ci == ri).astype(F32)

    gcol_tiles = [_mm(lo_incl, g_tile[c * chunk:(c + 1) * chunk], precision=HI) for c in range(n_chunks)]
    grow_tiles = [_mm(gt_all[:, c * chunk:(c + 1) * chunk], up_incl, precision=HI) for c in range(n_chunks)]

    wc = wc_ref[...]
    per_head = []
    for h in range(heads):
        parts = []
        for p in range(3):
            lo = p * aw + h * 128
            x = qkv_ref[0, :, lo:lo + 128]
            xe = jnp.concatenate([prev_scr[:, lo:lo + 128], x], axis=0)
            conv = (wc[3:4, lo:lo + 128] * x + wc[2:3, lo:lo + 128] * _shift_rows(xe, 1)
                    + wc[1:2, lo:lo + 128] * _shift_rows(xe, 2) + wc[0:1, lo:lo + 128] * _shift_rows(xe, 3))
            parts.append(conv * _sigmoid(conv))
        q_all, k_all, v_all = parts
        q_all = q_all * lax.rsqrt(jnp.sum(q_all * q_all, axis=-1, keepdims=True) + EPS) * (128.0 ** -0.5)
        k_all = k_all * lax.rsqrt(jnp.sum(k_all * k_all, axis=-1, keepdims=True) + EPS)
        per_head.append((q_all, jnp.where(valid_c, k_all, 0.0), v_all))

    units = [(c, h) for c in range(n_chunks) for h in range(heads)]
    rows = lambda c: slice(c * chunk, (c + 1) * chunk)
    q = jnp.stack([per_head[h][0][rows(c)] for c, h in units])
    k = jnp.stack([per_head[h][1][rows(c)] for c, h in units])
    v = jnp.stack([per_head[h][2][rows(c)] for c, h in units])
    gcol = jnp.stack([gcol_tiles[c][:, heads + h:heads + h + 1] for c, h in units])
    grow = jnp.stack([grow_tiles[c][heads + h:heads + h + 1, :] for c, h in units])
    bcol = jnp.stack([beta_tile[rows(c), h:h + 1] for c, h in units])

    decay = jnp.where(incl, jnp.exp(jnp.where(incl, gcol - grow, 0.0)), 0.0)
    kb = k * bcol
    vb = v * bcol
    aq = _P_AQ(jnp.concatenate([kb, q], axis=1), k, BNT)
    lower = jnp.where(strict, aq[:, :chunk] * decay, 0.0)
    qk = jnp.where(incl, aq[:, chunk:] * decay, 0.0)
    base = min(GROUP, chunk)
    same_blk = lambda s: (ri >> (s.bit_length() - 1)) == (ci >> (s.bit_length() - 1))
    neg_bd = jnp.where(same_blk(base), -lower, 0.0)
    nm = _P_INV(neg_bd, neg_bd, BNN)
    tinv = eye + neg_bd
    n_base = base.bit_length() - 2
    for lvl in range(n_base):
        if lvl < n_base - 1:
            r = _P_INV(jnp.concatenate([nm, tinv], axis=1), nm, BNN)
            tinv = tinv + r[:, chunk:]
            nm = r[:, :chunk]
        else:
            tinv = tinv + _P_INV(tinv, nm, BNN)
    s = base
    while s < chunk:
        off = jnp.where(same_blk(2 * s) & jnp.logical_not(same_blk(s)), lower, 0.0)
        tinv = tinv - _P_MRG(tinv, _P_MRG(off, tinv, BNN), BNN)
        s *= 2
    eg = jnp.exp(gcol)
    uw = _P_UW(tinv, jnp.concatenate([vb, kb * eg], axis=2), BNN)
    wq = jnp.concatenate([uw[:, :, 128:], q * eg], axis=1)
    g_last = gcol[:, chunk - 1:chunk, :]
    kd = k * jnp.exp(g_last - gcol)
    e_last = jnp.exp(g_last)

    s_all = s_scr[...]
    for c in range(n_chunks):
        us = slice(c * heads, (c + 1) * heads)
        ws = _P_WS(wq[us], s_all, BNN)
        v_new = uw[us, :, :128] - ws[:, :chunk]
        o = ws[:, chunk:] + _P_O(qk[us], v_new, BNN)
        s_all = s_all * e_last[us] + _P_S(kd[us], v_new, BTN)
        on = o * lax.rsqrt(jnp.mean(o * o, axis=-1, keepdims=True) + EPS) * gn_ref[...]
        for h in range(heads):
            z = z_ref[0, rows(c), h * 128:(h + 1) * 128]
            o_ref[0, rows(c), h * 128:(h + 1) * 128] = (on[h] * (z * _sigmoid(z))).astype(o_ref.dtype)
    s_scr[...] = s_all

    prev_scr[...] = qkv_ref[0, tc - GROUP:, :]

    @pl.when(t == pl.num_programs(1) - 1)
    def _():
        sout_ref[0] = s_scr[...]


def _gdn(proj3, small3, smallt3, w_conv, a_row, a_col, gn, s0, *, heads, front_pad):
    b, t, _ = proj3.shape
    aw = heads * 128
    chunk = min(GDN_CHUNK, t)
    tc = t if t <= 128 else 128
    n_chunks = tc // chunk
    rows_t = smallt3.shape[1]
    kern = functools.partial(_gdn_kernel, heads=heads, chunk=chunk, n_chunks=n_chunks, front_pad=front_pad)
    return pl.pallas_call(
        kern,
        grid=(b, t // tc),
        in_specs=[pl.BlockSpec((1, tc, 3 * aw), lambda i, j: (i, j, 0)),
                  pl.BlockSpec((1, tc, aw), lambda i, j: (i, j, 3)),
                  pl.BlockSpec((1, tc, 128), lambda i, j: (i, j, 0)),
                  pl.BlockSpec((1, rows_t, tc), lambda i, j: (i, 0, j)),
                  pl.BlockSpec((4, 3 * aw), lambda i, j: (0, 0)),
                  pl.BlockSpec((2, 128), lambda i, j: (0, 0)),
                  pl.BlockSpec((rows_t, 2), lambda i, j: (0, 0)),
                  pl.BlockSpec((1, 128), lambda i, j: (0, 0)),
                  pl.BlockSpec((1, heads, 128, 128), lambda i, j: (i, 0, 0, 0))],
        out_specs=[pl.BlockSpec((1, tc, aw), lambda i, j: (i, j, 0)),
                   pl.BlockSpec((1, heads, 128, 128), lambda i, j: (i, 0, 0, 0))],
        out_shape=[jax.ShapeDtypeStruct((b, t, aw), BF16),
                   jax.ShapeDtypeStruct((b, heads, 128, 128), F32)],
        scratch_shapes=[pltpu.VMEM((heads, 128, 128), F32), pltpu.VMEM((GROUP, 3 * aw), F32)],
        compiler_params=_params(("arbitrary", "arbitrary")),
        name="gdn",
    )(proj3, proj3, small3, smallt3, w_conv, a_row, a_col, gn, s0)


def _fox_prep_kernel(q_ref, k_ref, v_ref, sm_ref, smt_ref, qg_ref, kg_ref, brow_ref, bcol_ref,
                     qn_ref, kn_ref, knb_ref, vb_ref, logf_ref, cr_ref, carry_r,
                     *, heads, front_pad, q_scale):
    t = pl.program_id(1)
    tr = q_ref.shape[1]

    @pl.when(t == 0)
    def _():
        carry_r[...] = jnp.zeros_like(carry_r)

    for h in range(heads):
        cs = slice(h * 128, (h + 1) * 128)
        q = q_ref[0, :, cs]
        qn = q * lax.rsqrt(jnp.mean(q * q, axis=-1, keepdims=True) + EPS) * qg_ref[...]
        qn_ref[0, :, cs] = (qn * q_scale).astype(BF16)
        k = k_ref[0, :, cs]
        kn = k * lax.rsqrt(jnp.mean(k * k, axis=-1, keepdims=True) + EPS) * kg_ref[...]
        kn_ref[0, :, cs] = kn
        knb_ref[0, :, cs] = kn.astype(BF16)
    vb_ref[0] = v_ref[0].astype(BF16)

    col = lax.broadcasted_iota(jnp.int32, (1, tr), 1)
    logf_ref[0] = -_softplus(-(sm_ref[0] + brow_ref[...]))
    logft = jnp.where(col >= front_pad, -_softplus(-(smt_ref[0] + bcol_ref[...])), 0.0)
    cumt = _mm(logft, _tri(tr, "upper_incl"), precision=HI) + carry_r[:, 0:1]
    cr_ref[0] = cumt
    carry_r[...] = jnp.broadcast_to(cumt[:, tr - 1:tr], carry_r.shape)


def _fox_prep(proj3, small3, smallt3, qg, kg, brow, bcol, *, heads, front_pad, q_scale):
    b, t, _ = proj3.shape
    bw = heads * 128
    tr = _pick(t, 256, 128) if t >= 128 else t
    rows_t = smallt3.shape[1]
    wide = lambda blk: pl.BlockSpec((1, tr, bw), lambda i, j, blk=blk: (i, j, blk))
    out_w = pl.BlockSpec((1, tr, bw), lambda i, j: (i, j, 0))
    out_s = pl.BlockSpec((1, tr, 128), lambda i, j: (i, j, 0))
    return pl.pallas_call(
        functools.partial(_fox_prep_kernel, heads=heads, front_pad=front_pad, q_scale=q_scale),
        grid=(b, t // tr),
        in_specs=[wide(4), wide(5), wide(6),
                  pl.BlockSpec((1, tr, 128), lambda i, j: (i, j, 0)),
                  pl.BlockSpec((1, rows_t, tr), lambda i, j: (i, 0, j)),
                  pl.BlockSpec((1, 128), lambda i, j: (0, 0)),
                  pl.BlockSpec((1, 128), lambda i, j: (0, 0)),
                  pl.BlockSpec((1, 128), lambda i, j: (0, 0)),
                  pl.BlockSpec((rows_t, 1), lambda i, j: (0, 0))],
        out_specs=[out_w, out_w, out_w, out_w, out_s,
                   pl.BlockSpec((1, rows_t, tr), lambda i, j: (i, 0, j))],
        out_shape=[jax.ShapeDtypeStruct((b, t, bw), BF16), jax.ShapeDtypeStruct((b, t, bw), F32),
                   jax.ShapeDtypeStruct((b, t, bw), BF16), jax.ShapeDtypeStruct((b, t, bw), BF16),
                   jax.ShapeDtypeStruct((b, t, 128), F32),
                   jax.ShapeDtypeStruct((b, rows_t, t), F32)],
        scratch_shapes=[pltpu.VMEM((rows_t, 128), F32)],
        compiler_params=_params(("arbitrary", "arbitrary")),
        name="fox_prep",
    )(proj3, proj3, proj3, small3, smallt3, qg, kg, brow, bcol)


def _flash_kernel(q_ref, k_ref, v_ref, cr_ref, gate_ref, o_ref, *, tq, tk, n_sub):
    qi = pl.program_id(2)
    ts = tq // n_sub
    qs = [q_ref[0, a * ts:(a + 1) * ts, :] for a in range(n_sub)]
    q0 = qi * tq
    jd = q0 // tk

    def step(j, carry, masked):
        start = pl.multiple_of(j * tk, tk)
        ks = k_ref[0, pl.ds(start, tk), :]
        vs = v_ref[0, pl.ds(start, tk), :]
        ck = cr_ref[0, 0, :, pl.ds(start, tk)] * LOG2E
        ss = [_mm(q, ks, NT) - ck for q in qs]
        if masked:
            ci = lax.broadcasted_iota(jnp.int32, (ts, tk), 1) + start
            ss = [jnp.where(ci <= lax.broadcasted_iota(jnp.int32, (ts, tk), 0) + (q0 + a * ts), s, NEG)
                  for a, s in enumerate(ss)]
        m_new = [jnp.maximum(c[0], jnp.max(s, axis=-1, keepdims=True)) for c, s in zip(carry, ss)]
        ps = [jnp.exp2(s - m) for s, m in zip(ss, m_new)]
        pv = [_mm(p.astype(BF16), vs) for p in ps]
        out = []
        for (m, l, acc), mn, p, o in zip(carry, m_new, ps, pv):
            corr = jnp.exp2(m - mn)
            out.append((mn, l * corr + jnp.sum(p, axis=-1, keepdims=True), acc * corr + o))
        return tuple(out)

    init = tuple((jnp.full((ts, 1), NEG, F32), jnp.zeros((ts, 1), F32), jnp.zeros((ts, 128), F32))
                 for _ in range(n_sub))
    carry = lax.fori_loop(0, jd, lambda j, c: step(j, c, False), init)
    carry = step(jd, carry, True)
    for a, (m, l, acc) in enumerate(carry):
        g = gate_ref[0, a * ts:(a + 1) * ts, :]
        o_ref[0, a * ts:(a + 1) * ts, :] = (acc / l * _sigmoid(g)).astype(o_ref.dtype)


def _flash(qn, knb, vb, cr4, proj3, *, heads):
    b, t, bw = qn.shape
    tk = _pick(t, 512, 128)
    tq = tk
    n_sub = 2 if tq % 256 == 0 else 1
    qspec = pl.BlockSpec((1, tq, 128), lambda i, h, j: (i, j, h))
    kvspec = pl.BlockSpec((1, t, 128), lambda i, h, j: (i, 0, h))
    return pl.pallas_call(
        functools.partial(_flash_kernel, tq=tq, tk=tk, n_sub=n_sub),
        grid=(b, heads, t // tq),
        in_specs=[qspec, kvspec, kvspec,
                  pl.BlockSpec((1, 1, 1, t), lambda i, h, j: (i, h, 0, 0)),
                  pl.BlockSpec((1, tq, 128), lambda i, h, j: (i, j, 7 * heads + h))],
        out_specs=qspec,
        out_shape=jax.ShapeDtypeStruct((b, t, bw), BF16),
        compiler_params=_params(("arbitrary", "arbitrary", "arbitrary")),
        name="fox_flash",
    )(qn, knb, vb, cr4, proj3)


def _suffix_kernel(lf_ref, o_ref, *, heads):
    x = lf_ref[...]
    n = x.shape[1]
    lane = lax.broadcasted_iota(jnp.int32, (1, n), 1)
    incl = x
    tot = x
    d = heads
    while d < n:
        incl = incl + jnp.where(lane < n - d, pltpu.roll(incl, n - d, 1), 0.0)
        tot = tot + pltpu.roll(tot, d, 1)
        d *= 2
    o_ref[:, :n] = incl - x
    o_ref[:, n:] = tot


def _suffix(logf_flat, *, heads):
    n_pool, n = logf_flat.shape
    gp = _pick(n_pool, 256, GROUP)
    return pl.pallas_call(
        functools.partial(_suffix_kernel, heads=heads),
        grid=(n_pool // gp,),
        in_specs=[pl.BlockSpec((gp, n), lambda i: (i, 0))],
        out_specs=pl.BlockSpec((gp, 2 * n), lambda i: (i, 0)),
        out_shape=jax.ShapeDtypeStruct((n_pool, 2 * n), F32),
        compiler_params=_params(("arbitrary",)),
        name="page_suffix",
    )(logf_flat)


def _paged_kernel(pt_ref, *refs, heads, n_new, g_pages):
    del pt_ref
    qn_ref, knb_ref, vb_ref, cr_ref, gate_ref = refs[:5]
    k_refs = refs[5:5 + g_pages]
    v_refs = refs[5 + g_pages:5 + 2 * g_pages]
    s_refs = refs[5 + 2 * g_pages:5 + 3 * g_pages]
    o_ref, q_scr, m_scr, l_scr, acc_scr, tail_scr, cn_scr = refs[5 + 3 * g_pages:]
    p = pl.program_id(1)
    bw = heads * 128
    nr = n_new * heads
    first = GROUP - n_new
    n = tail_scr.shape[1]
    row_head = lax.broadcasted_iota(jnp.int32, (nr, 1), 0) & (heads - 1)

    @pl.when(p == 0)
    def _():
        lane_head = lax.broadcasted_iota(jnp.int32, (heads, bw), 1) // 128
        head_mask = lane_head == lax.broadcasted_iota(jnp.int32, (heads, bw), 0)
        qn = qn_ref[0].astype(F32)
        qbd = jnp.concatenate(
            [jnp.where(head_mask, jnp.broadcast_to(qn[first + i:first + i + 1, :], (heads, bw)), 0.0)
             for i in range(n_new)], axis=0)
        q_all = qbd[:, 0:128]
        for h in range(1, heads):
            q_all = q_all + qbd[:, h * 128:(h + 1) * 128]
        q_scr[...] = q_all.astype(BF16)
        cr = cr_ref[0][2 * heads:3 * heads, :]
        cn_col = jnp.concatenate([cr[:, first + i:first + i + 1] for i in range(n_new)], axis=0)
        cn_scr[...] = jnp.broadcast_to(cn_col, cn_scr.shape)
        cn_key = jnp.concatenate([cr] * n_new, axis=0)
        s = _mm(qbd.astype(BF16), knb_ref[0], NT) + cn_col - cn_key
        ri = lax.broadcasted_iota(jnp.int32, (nr, GROUP), 0) // heads
        ci = lax.broadcasted_iota(jnp.int32, (nr, GROUP), 1)
        s = jnp.where((ci >= first) & (ci - first <= ri), s, NEG)
        m0 = jnp.max(s, axis=-1, keepdims=True)
        p0 = jnp.exp(s - m0)
        m_scr[...] = jnp.broadcast_to(m0, m_scr.shape)
        l_scr[...] = jnp.broadcast_to(jnp.sum(p0, axis=-1, keepdims=True), l_scr.shape)
        full = _mm(p0.astype(BF16), vb_ref[0])
        acc0 = jnp.where(row_head == 0, full[:, 0:128], 0.0)
        for h in range(1, heads):
            acc0 = acc0 + jnp.where(row_head == h, full[:, h * 128:(h + 1) * 128], 0.0)
        acc_scr[...] = acc0
        tail_scr[...] = jnp.zeros_like(tail_scr)

    wide = lambda a: jnp.concatenate([a] * (n // 128), axis=1)
    q_all = q_scr[...]
    cn_w = wide(cn_scr[...])
    m = m_scr[...]
    l = l_scr[...]
    acc = acc_scr[...]
    tail = tail_scr[...]
    valid = (lax.broadcasted_iota(jnp.int32, (nr, n), 1) & (heads - 1)) == row_head
    scores = []
    for i in range(g_pages):
        blk = s_refs[i][0]
        bias = blk[:, :n] + tail
        tail = tail + blk[:, n:]
        kf = k_refs[i][0, 0].reshape(n, 128).astype(BF16)
        scores.append(jnp.where(valid, _mm(q_all, kf, NT) + bias + cn_w, NEG))
    s_max = scores[0]
    for s in scores[1:]:
        s_max = jnp.maximum(s_max, s)
    m_new = jnp.maximum(m, jnp.max(s_max, axis=-1, keepdims=True))
    corr = jnp.exp(m - m_new)
    m_w = wide(m_new)
    probs = [jnp.exp(s - m_w) for s in scores]
    p_sum = probs[0]
    for pr in probs[1:]:
        p_sum = p_sum + pr
    pv = _mm(probs[0].astype(BF16), v_refs[0][0, 0].reshape(n, 128).astype(BF16))
    for i in range(1, g_pages):
        pv = pv + _mm(probs[i].astype(BF16), v_refs[i][0, 0].reshape(n, 128).astype(BF16))
    l = l * corr + jnp.sum(p_sum, axis=-1, keepdims=True)
    acc = acc * corr + pv
    m_scr[...] = m_new
    l_scr[...] = l
    acc_scr[...] = acc
    tail_scr[...] = tail

    @pl.when(p == pl.num_programs(1) - 1)
    def _():
        o = jnp.concatenate([acc / l] * heads, axis=1)
        keep = (lax.broadcasted_iota(jnp.int32, (nr, bw), 1) // 128) == row_head
        o = jnp.where(keep, o, 0.0)
        rows = [jnp.zeros((first, bw), F32)]
        for i in range(n_new):
            rows.append(jnp.sum(o[i * heads:(i + 1) * heads, :], axis=0, keepdims=True))
        out = jnp.concatenate(rows, axis=0)
        o_ref[0] = (out * _sigmoid(gate_ref[0])).astype(o_ref.dtype)


def _paged(page_table, qn, knb, vb, cr, proj3, cache_k, cache_v, suffix3, *, heads, n_new):
    bd, _, bw = qn.shape
    n_pages = page_table.shape[1]
    page = cache_k.shape[2]
    n = page * heads
    g_pages = _pick(n_pages, 16, 1)
    rows_t = cr.shape[1]
    nr = n_new * heads
    seq = lambda blk_w, blk: pl.BlockSpec((1, GROUP, blk_w), lambda b, p, pt, blk=blk: (b, 0, blk))
    page_of = lambda b, p, pt, i: pt[b, n_pages - 1 - (p * g_pages + i)]
    kv_spec = lambda i: pl.BlockSpec((1, 1, page, heads, 128),
                                     lambda b, p, pt, i=i: (0, page_of(b, p, pt, i), 0, 0, 0))
    suf_spec = lambda i: pl.BlockSpec((1, 1, 2 * n), lambda b, p, pt, i=i: (page_of(b, p, pt, i), 0, 0))

    in_specs = [seq(bw, 0), seq(bw, 0), seq(bw, 0),
                pl.BlockSpec((1, rows_t, GROUP), lambda b, p, pt: (b, 0, 0)),
                seq(bw, 7)]
    in_specs += [kv_spec(i) for i in range(g_pages)] * 2
    in_specs += [suf_spec(i) for i in range(g_pages)]
    grid_spec = pltpu.PrefetchScalarGridSpec(
        num_scalar_prefetch=1,
        grid=(bd, n_pages // g_pages),
        in_specs=in_specs,
        out_specs=pl.BlockSpec((1, GROUP, bw), lambda b, p, pt: (b, 0, 0)),
        scratch_shapes=[pltpu.VMEM((nr, 128), BF16), pltpu.VMEM((nr, 128), F32), pltpu.VMEM((nr, 128), F32),
                        pltpu.VMEM((nr, 128), F32), pltpu.VMEM((1, n), F32), pltpu.VMEM((nr, 128), F32)],
    )
    return pl.pallas_call(
        functools.partial(_paged_kernel, heads=heads, n_new=n_new, g_pages=g_pages),
        grid_spec=grid_spec,
        out_shape=jax.ShapeDtypeStruct((bd, GROUP, bw), BF16),
        compiler_params=_params(("arbitrary", "arbitrary")),
        name="fox_paged",
    )(page_table, qn, knb, vb, cr, proj3, *([cache_k] * g_pages), *([cache_v] * g_pages), *([suffix3] * g_pages))


def _out_proj_kernel(oa_ref, ob_ref, wa_ref, wb_ref, x_ref, gt_ref, g_ref, o_ref):
    mix = _mm(oa_ref[...], wa_ref[...]) + _mm(ob_ref[...], wb_ref[...])
    normed = mix * lax.rsqrt(jnp.mean(mix * mix, axis=-1, keepdims=True) + EPS) * g_ref[...]
    o_ref[...] = x_ref[...] + gt_ref[0] * normed


def _out_proj(oa, ob, wa, wb, x2d, gt, g, *, tm, tiles_per_b):
    r, d = x2d.shape
    aw, bw = oa.shape[1], ob.shape[1]
    mr = gt.shape[1]
    return pl.pallas_call(
        _out_proj_kernel,
        grid=(r // tm,),
        in_specs=[pl.BlockSpec((tm, aw), lambda i: (i, 0)),
                  pl.BlockSpec((tm, bw), lambda i: (i, 0)),
                  pl.BlockSpec((aw, d), lambda i: (0, 0)),
                  pl.BlockSpec((bw, d), lambda i: (0, 0)),
                  pl.BlockSpec((tm, d), lambda i: (i, 0)),
                  pl.BlockSpec((1, mr, d), lambda i: (i // tiles_per_b, 0, 0)),
                  pl.BlockSpec((1, d), lambda i: (0, 0))],
        out_specs=pl.BlockSpec((tm, d), lambda i: (i, 0)),
        out_shape=jax.ShapeDtypeStruct((r, d), F32),
        compiler_params=_params(("arbitrary",)),
        name="out_proj",
    )(oa, ob, wa, wb, x2d, gt, g.reshape(1, d))


def _ffn_tail_kernel(ug_ref, uv_ref, hg_ref, hv_ref, wcg_ref, wcv_ref, bg_ref, bv_ref, wd_ref, x_ref, gt_ref, g_ref,
                     o_ref, acc_scr, *, tiles_per_b):
    i = pl.program_id(0)
    j = pl.program_id(1)
    first = (i % tiles_per_b) == 0

    def conv(u_ref, halo_ref, wc_ref, b_ref):
        x = u_ref[...]
        halo = jnp.where(first, 0.0, halo_ref[...])
        xe = jnp.concatenate([halo, x], axis=0)
        wc = wc_ref[...]
        return wc[2:3] * x + wc[1:2] * _shift_rows(xe, 1) + wc[0:1] * _shift_rows(xe, 2) + b_ref[...]

    gate = conv(ug_ref, hg_ref, wcg_ref, bg_ref)
    val = conv(uv_ref, hv_ref, wcv_ref, bv_ref)
    act = (gate * _sigmoid(gate) * val).astype(BF16)
    part = _mm(act, wd_ref[...])

    @pl.when(j == 0)
    def _():
        acc_scr[...] = part

    @pl.when(j > 0)
    def _():
        acc_scr[...] += part

    @pl.when(j == pl.num_programs(1) - 1)
    def _():
        y = acc_scr[...]
        normed = y * lax.rsqrt(jnp.mean(y * y, axis=-1, keepdims=True) + EPS) * g_ref[...]
        o_ref[...] = x_ref[...] + gt_ref[0] * normed


def _ffn_tail(up, wc, bc, wd, x2d, gt, g, *, tm, tiles_per_b, tf):
    r, d = x2d.shape
    fp = wd.shape[0]
    nf = fp // tf
    hb = tm // GROUP
    mr = gt.shape[1]
    halo = lambda off: pl.BlockSpec((GROUP, tf), lambda i, j, off=off: (jnp.maximum(i * hb - 1, 0), j + off))
    return pl.pallas_call(
        functools.partial(_ffn_tail_kernel, tiles_per_b=tiles_per_b),
        grid=(r // tm, nf),
        in_specs=[pl.BlockSpec((tm, tf), lambda i, j: (i, j)),
                  pl.BlockSpec((tm, tf), lambda i, j: (i, j + nf)),
                  halo(0), halo(nf),
                  pl.BlockSpec((3, tf), lambda i, j: (0, j)),
                  pl.BlockSpec((3, tf), lambda i, j: (0, j + nf)),
                  pl.BlockSpec((1, tf), lambda i, j: (0, j)),
                  pl.BlockSpec((1, tf), lambda i, j: (0, j + nf)),
                  pl.BlockSpec((tf, d), lambda i, j: (j, 0)),
                  pl.BlockSpec((tm, d), lambda i, j: (i, 0)),
                  pl.BlockSpec((1, mr, d), lambda i, j: (i // tiles_per_b, 0, 0)),
                  pl.BlockSpec((1, d), lambda i, j: (0, 0))],
        out_specs=pl.BlockSpec((tm, d), lambda i, j: (i, 0)),
        out_shape=jax.ShapeDtypeStruct((r, d), F32),
        scratch_shapes=[pltpu.VMEM((tm, d), F32)],
        compiler_params=_params(("arbitrary", "arbitrary")),
        name="ffn_tail",
    )(up, up, up, up, wc, wc, bc, bc, wd, x2d, gt, g.reshape(1, d))


def _ffn_fused_kernel(x_ref, sc_ref, sh_ref, gpre_ref, wg_ref, wv_ref, wcg_ref, wcv_ref, bg_ref, bv_ref, wd_ref,
                      gt_ref, gpost_ref, o_ref, lg_ref, lv_ref, h_scr, cg_scr, cv_scr, xg_scr, xv_scr, act_scr,
                      *, tiles_per_b):
    i = pl.program_id(0)
    j = pl.program_id(1)
    tm = x_ref.shape[0]
    first = (i % tiles_per_b) == 0

    @pl.when(j == 0)
    def _():
        x = x_ref[...]
        y = x * lax.rsqrt(jnp.mean(x * x, axis=-1, keepdims=True) + EPS) * gpre_ref[...]
        h_scr[...] = (y * (1.0 + sc_ref[0]) + sh_ref[0]).astype(BF16)

    h = h_scr[...]
    tf = wd_ref.shape[0]

    def project(w_ref, xe_scr, carry_scr, last_ref):
        xe_scr[0:GROUP, :] = jnp.where(first, 0.0, carry_scr[j])
        xe_scr[GROUP:, :] = _mm(h, w_ref[0])
        last = xe_scr[tm:tm + GROUP, :]
        carry_scr[j] = last
        last_ref[0] = last

    project(wg_ref, xg_scr, cg_scr, lg_ref)
    project(wv_ref, xv_scr, cv_scr, lv_ref)

    def conv(xe_scr, wc_ref, b_ref, r0, c0):
        xe = xe_scr[r0:r0 + FFN_ROWS + GROUP, c0:c0 + FFN_LANES]
        wc = wc_ref[:, c0:c0 + FFN_LANES]
        return (wc[2:3] * xe[GROUP:] + wc[1:2] * _shift_rows(xe, 1) + wc[0:1] * _shift_rows(xe, 2)
                + b_ref[:, c0:c0 + FFN_LANES])

    for r0 in range(0, tm, FFN_ROWS):
        for c0 in range(0, tf, FFN_LANES):
            gate = conv(xg_scr, wcg_ref, bg_ref, r0, c0)
            val = conv(xv_scr, wcv_ref, bv_ref, r0, c0)
            act_scr[r0:r0 + FFN_ROWS, c0:c0 + FFN_LANES] = (gate * _sigmoid(gate) * val).astype(BF16)
    part = _mm(act_scr[...], wd_ref[...])

    @pl.when(j == 0)
    def _():
        o_ref[...] = part

    @pl.when(j > 0)
    def _():
        o_ref[...] += part

    @pl.when(j == pl.num_programs(1) - 1)
    def _():
        y = o_ref[...]
        normed = y * lax.rsqrt(jnp.mean(y * y, axis=-1, keepdims=True) + EPS) * gpost_ref[...]
        o_ref[...] = x_ref[...] + gt_ref[0] * normed


def _ffn_fused(x2d, sc, sh, gt, g_pre, g_post, w_up_t, wc, bc, wd, *, nb, tm):
    r, d = x2d.shape
    fp = wd.shape[0]
    tf = w_up_t.shape[2]
    nf = fp // tf
    tiles_per_b = r // nb // tm
    mod = pl.BlockSpec((1, 1, d), lambda i, j: (i // tiles_per_b, 0, 0))
    vec = pl.BlockSpec((1, d), lambda i, j: (0, 0))
    col = lambda rows, off: pl.BlockSpec((rows, tf), lambda i, j, off=off: (0, j + off))
    wtile = lambda off: pl.BlockSpec((1, d, tf), lambda i, j, off=off: (j + off, 0, 0))
    last = pl.BlockSpec((1, GROUP, tf), lambda i, j: (i, 0, j))
    rows_once = lambda: pl.BlockSpec((tm, d), lambda i, j: (i, 0), pipeline_mode=pl.Buffered(1))
    return pl.pallas_call(
        functools.partial(_ffn_fused_kernel, tiles_per_b=tiles_per_b),
        grid=(r // tm, nf),
        in_specs=[rows_once(), mod, mod, vec,
                  wtile(0), wtile(nf), col(3, 0), col(3, nf), col(1, 0), col(1, nf),
                  pl.BlockSpec((tf, d), lambda i, j: (j, 0)), mod, vec],
        out_specs=[rows_once(), last, last],
        out_shape=[jax.ShapeDtypeStruct((r, d), F32), jax.ShapeDtypeStruct((r // tm, GROUP, fp), F32),
                   jax.ShapeDtypeStruct((r // tm, GROUP, fp), F32)],
        scratch_shapes=[pltpu.VMEM((tm, d), BF16),
                        pltpu.VMEM((nf, GROUP, tf), F32), pltpu.VMEM((nf, GROUP, tf), F32),
                        pltpu.VMEM((tm + GROUP, tf), F32), pltpu.VMEM((tm + GROUP, tf), F32),
                        pltpu.VMEM((tm, tf), BF16)],
        compiler_params=_params(("arbitrary", "arbitrary")),
        name="ffn_fused",
    )(x2d, sc, sh, g_pre.reshape(1, d), w_up_t, w_up_t, wc, wc, bc, bc, wd, gt, g_post.reshape(1, d))


def _pad_cols(a, n):
    return jnp.pad(a, [(0, 0)] * (a.ndim - 1) + [(0, n - a.shape[-1])])


def _splice_rows(a, rows, start):
    k = rows.shape[1]
    padded = jnp.pad(rows, ((0, 0), (start, a.shape[1] - start - k), (0, 0)))
    idx = lax.broadcasted_iota(jnp.int32, (1, a.shape[1], 1), 1)
    return jnp.where((idx >= start) & (idx < start + k), padded, a)


def _split_hi_lo(w):
    hi = w.astype(BF16)
    return hi, (w - hi.astype(F32)).astype(BF16)


def _layer(x3, mods, st_conv, st_gdn, st_ffn, fox, lw, *, heads, front_pad, tm, q_scale):
    (g_pre_mix, g_post_mix, g_pre_ffn, g_post_ffn, w_big, ws, w_conv_qkv, a_row, a_col, gn, qg, kg, brow, bcol,
     w_out_a, w_out_b, w_up, w_conv_ffn, b_conv_ffn, w_down, d_ff, tf) = lw
    sh_m, sc_m, gt_m, sh_f, sc_f, gt_f = mods
    nb, t, d = x3.shape
    aw = heads * 128
    x2d = x3.reshape(nb * t, d)
    tiles_per_b = max(t // tm, 1) if mods[0].shape[1] == 1 else 1

    proj, small = _norm_proj(x2d, sc_m, sh_m, g_pre_mix, w_big, ws, tm=tm, tiles_per_b=tiles_per_b)
    proj3 = proj.reshape(nb, t, 8 * aw)
    if st_conv is not None:
        k = st_conv.shape[1]
        proj3 = _splice_rows(proj3, st_conv, GROUP - (t - front_pad) - k)
    small3 = small.reshape(nb, t, 128)
    rows_t = 3 * GROUP
    smallt3 = jnp.swapaxes(small3[:, :, :rows_t], 1, 2)

    o_a, gdn_new = _gdn(proj3, small3, smallt3, w_conv_qkv, a_row, a_col, gn, st_gdn,
                        heads=heads, front_pad=front_pad)
    qn, kn, knb, vb, logf, cr = _fox_prep(proj3, small3, smallt3, qg, kg, brow, bcol,
                                               heads=heads, front_pad=front_pad, q_scale=q_scale)
    o_b = fox(qn, knb, vb, cr, proj3)

    x1 = _out_proj(o_a.reshape(nb * t, aw), o_b.reshape(nb * t, aw), w_out_a, w_out_b, x2d, gt_m, g_post_mix,
                   tm=min(tm, 512), tiles_per_b=max(t // min(tm, 512), 1) if mods[0].shape[1] == 1 else 1)
    fp = w_down.shape[0]
    unpad = lambda g, v: jnp.concatenate([g[..., :d_ff], v[..., :d_ff]], axis=-1)
    if st_ffn is None:
        w_up_t = jnp.swapaxes(w_up.reshape(d, 2 * fp // tf, tf), 0, 1)
        y, last_g, last_v = _ffn_fused(x1, sc_f, sh_f, gt_f, g_pre_ffn, g_post_ffn, w_up_t, w_conv_ffn, b_conv_ffn,
                                       w_down, nb=nb, tm=tm)
        per_b = last_g.shape[0] // nb
        up_last = unpad(last_g[per_b - 1::per_b], last_v[per_b - 1::per_b])
    else:
        up = _norm_proj(x1, sc_f, sh_f, g_pre_ffn, w_up, tm=tm, tiles_per_b=tiles_per_b)
        up3 = up.reshape(nb, t, 2 * fp)
        k = st_ffn.shape[1]
        up3 = _splice_rows(up3, st_ffn, GROUP - (t - front_pad) - k)
        y = _ffn_tail(up3.reshape(nb * t, 2 * fp), w_conv_ffn, b_conv_ffn, w_down, x1, gt_f, g_post_ffn,
                      tm=min(tm, 512), tiles_per_b=1, tf=tf)
        up_last = unpad(up3[:, t - GROUP:, :fp], up3[:, t - GROUP:, fp:])
    return y.reshape(nb, t, d), proj3, kn, logf, gdn_new, up_last


def kernel(x_prompt, x_sample, cache_k, cache_v, cache_logf, state_gdn, state_conv_qkv, state_ffn_conv, page_table, c_prompt, c_sample, w_ada, b_ada, g_pre_mix, g_post_mix, g_pre_ffn, g_post_ffn, w_in, w_conv_qkv, a_log, dt_bias, g_gdn_norm, q_norm, k_norm, b_forget, w_out, w_up, w_conv_ffn, b_conv_ffn, w_down):
    depth = w_ada.shape[0]
    assert depth == 1, "single-layer step"
    b, t, d = x_prompt.shape
    bd, n_new, _ = x_sample.shape
    heads = state_gdn.shape[2]
    dh = state_gdn.shape[3]
    assert dh == 128 and cache_k.shape[3] == heads and n_new <= GROUP // 2
    aw = heads * dh
    page = cache_k.shape[2]
    n_pool = cache_k.shape[1]
    d_ff = w_down.shape[1]
    conv_a = w_conv_qkv.shape[1]
    ffn_conv = w_conv_ffn.shape[1]
    assert conv_a == 4 and ffn_conv == 3
    layer = 0
    (cache_logf, state_gdn, state_conv_qkv, state_ffn_conv, w_ada, b_ada, g_pre_mix, g_post_mix,
     g_pre_ffn, g_post_ffn, w_in, w_conv_qkv, a_log, dt_bias, g_gdn_norm, q_norm, k_norm, b_forget, w_out, w_up,
     w_conv_ffn, b_conv_ffn, w_down) = [
        (a.reshape(a.shape[1:]),) for a in
        (cache_logf, state_gdn, state_conv_qkv, state_ffn_conv, w_ada, b_ada, g_pre_mix, g_post_mix,
         g_pre_ffn, g_post_ffn, w_in, w_conv_qkv, a_log, dt_bias, g_gdn_norm, q_norm, k_norm, b_forget, w_out, w_up,
         w_conv_ffn, b_conv_ffn, w_down)]

    wi = w_in[layer]
    o1 = 4 * aw
    o2 = o1 + 2 * heads
    o3 = o2 + 4 * aw
    w_big = jnp.concatenate([wi[:, :o1], wi[:, o2:o3]], axis=1).astype(BF16)
    w_small = _pad_cols(jnp.concatenate([wi[:, o1:o2], wi[:, o3:]], axis=1), 128)
    ws = _split_hi_lo(w_small)
    zeros_h = jnp.zeros((heads,), F32)
    a_row = _pad_cols(jnp.stack([jnp.concatenate([zeros_h, a_log[layer]]),
                                 jnp.concatenate([zeros_h, dt_bias[layer]])]), 128)
    rows_t = 3 * GROUP
    a_col = jnp.pad(a_row[:, :rows_t].T, ((0, 0), (0, 0)))
    brow = _pad_cols(jnp.concatenate([zeros_h, zeros_h, b_forget[layer]])[None, :], 128)
    bcol = brow[:, :rows_t].T
    assert heads == GROUP and page == 128
    gn = g_gdn_norm[layer].reshape(1, dh)
    qg = q_norm[layer].reshape(1, dh)
    kg = k_norm[layer].reshape(1, dh)
    wo = w_out[layer].astype(BF16)
    w_out_a, w_out_b = wo[:aw], wo[aw:]
    tf = 512
    fp = -(-d_ff // tf) * tf
    wu = w_up[layer]
    w_up_p =jnp.concatenate([_pad_cols(wu[:, :d_ff], fp), _pad_cols(wu[:, d_ff:], fp)], axis=1).astype(BF16)
    wcf = w_conv_ffn[layer]
    w_conv_ffn_p =jnp.concatenate([_pad_cols(wcf[:, :d_ff], fp), _pad_cols(wcf[:, d_ff:], fp)], axis=1)
    bcf = b_conv_ffn[layer][None, :]
    b_conv_ffn_p = jnp.concatenate([_pad_cols(bcf[:, :d_ff], fp), _pad_cols(bcf[:, d_ff:], fp)], axis=1)
    w_down_p = jnp.pad(w_down[layer], ((0, fp - d_ff), (0, 0))).astype(BF16)
    lw = (g_pre_mix[layer], g_post_mix[layer], g_pre_ffn[layer], g_post_ffn[layer], w_big, ws, w_conv_qkv[layer],
          a_row, a_col, gn, qg, kg, brow, bcol, w_out_a, w_out_b, w_up_p, w_conv_ffn_p, b_conv_ffn_p, w_down_p,
          d_ff, tf)

    n_c = b + bd
    c_all = jnp.pad(jnp.concatenate([c_prompt, c_sample], axis=0), ((0, -n_c % GROUP), (0, 0)))
    mod = _ada(c_all, w_ada[layer], b_ada[layer])
    mods_p = [m[:b].reshape(b, 1, d) for m in jnp.split(mod, 6, axis=-1)]
    mods_s = [jnp.repeat(m[b:n_c], GROUP, axis=0).reshape(1, bd * GROUP, d) for m in jnp.split(mod, 6, axis=-1)]

    tm_p = _pick(t, 1024, 128)
    fox_p = lambda qn, knb, vb, cr, proj3: _flash(
        qn, knb, vb, cr[:, 2 * heads:3 * heads].reshape(b, heads, 1, t), proj3, heads=heads)
    zeros_s0 = jnp.zeros((b, heads, dh, dh), F32)
    y_p, proj_p, kn_p, logf_p, gdn_p, up_p = _layer(
        x_prompt, mods_p, None, zeros_s0, None, fox_p, lw, heads=heads, front_pad=0, tm=tm_p,
        q_scale=dh ** -0.5 * LOG2E)

    front = GROUP - n_new
    x_s = jnp.pad(x_sample, ((0, 0), (front, 0), (0, 0)))
    suffix = _suffix(cache_logf[layer].reshape(n_pool, page * heads), heads=heads)
    suffix3 = suffix.reshape(n_pool, 1, 2 * page * heads)
    fox_s = lambda qn, knb, vb, cr, proj3: _paged(
        page_table, qn, knb, vb, cr, proj3, cache_k, cache_v, suffix3, heads=heads, n_new=n_new)
    st_ffn = state_ffn_conv[layer]
    st_ffn_p =jnp.concatenate([_pad_cols(st_ffn[:, :, :d_ff], fp), _pad_cols(st_ffn[:, :, d_ff:], fp)], axis=-1)
    st_conv = _pad_cols(state_conv_qkv[layer], 8 * aw)
    y_s, proj_s, kn_s, logf_s, gdn_s, up_s = _layer(
        x_s, mods_s, st_conv, state_gdn[layer], st_ffn_p, fox_s, lw, heads=heads, front_pad=front, tm=bd * GROUP,
        q_scale=dh ** -0.5)

    n_pg = t // page
    k_prompt = kn_p.reshape(1, b, n_pg, page, heads, dh)
    v_prompt = proj_p[:, :, 6 * aw:7 * aw].reshape(1, b, n_pg, page, heads, dh)
    logf_prompt = logf_p[:, :, 2 * heads:3 * heads].reshape(1, b, n_pg, page, heads)
    conv_qkv_prompt = proj_p[:, t - (conv_a - 1):, :3 * aw][None]
    ffn_conv_prompt = up_p[:, GROUP - (ffn_conv - 1):, :][None]
    k_sample = kn_s[:, front:].reshape(1, bd, n_new, heads, dh)
    v_sample = proj_s[:, front:, 6 * aw:7 * aw].reshape(1, bd, n_new, heads, dh)
    logf_sample = logf_s[:, front:, 2 * heads:3 * heads][None]
    conv_qkv_sample = proj_s[:, GROUP - (conv_a - 1):, :3 * aw][None]
    ffn_conv_sample = up_s[:, GROUP - (ffn_conv - 1):, :][None]
    return (y_p, y_s[:, front:], k_prompt, v_prompt, logf_prompt, gdn_p[None], conv_qkv_prompt, ffn_conv_prompt,
            k_sample, v_sample, logf_sample, gdn_s.astype(state_gdn[layer].dtype)[None], conv_qkv_sample, ffn_conv_sample)
```

```python
import functools

import jax
import jax.numpy as jnp
from jax import lax
from jax.experimental import pallas as pl
from jax.experimental.pallas import tpu as pltpu

EPS = 1e-6
F32 = jnp.float32
BF16 = jnp.bfloat16
HI = lax.Precision.HIGHEST
NEG = -1e30
LOG2E = 1.4426950408889634
GDN_CHUNK = 64
GROUP = 8
FFN_ROWS, FFN_LANES = 64, 256
FFN_SPLIT = 4
V7X_VMEM_LIMIT = 56 * 1024 * 1024

NN = (((1,), (0,)), ((), ()))
NT = (((1,), (1,)), ((), ()))
TN = (((0,), (0,)), ((), ()))


def _mm(a, b, dims=NN, precision=None):
    return lax.dot_general(a, b, dims, precision=precision, preferred_element_type=F32)


BNN = (((2,), (1,)), ((0,), (0,)))
BNT = (((2,), (2,)), ((0,), (0,)))
BTN = (((1,), (1,)), ((0,), (0,)))


def _bmm(a, b, dims=BNN):
    return lax.dot_general(a, b, dims, preferred_element_type=F32)


def _bmm1(a, b, dims):
    return _bmm(a.astype(BF16), b.astype(BF16), dims)


def _bmm3(a, b, dims):
    free = 2 if dims == BTN else 1
    m = a.shape[free]
    ah = a.astype(BF16).astype(F32)
    bh = b.astype(BF16)
    bl = (b - bh.astype(F32)).astype(BF16)
    stack = jnp.concatenate([ah, a - ah], axis=free).astype(BF16)
    r = _bmm(stack, bh, dims)
    r2 = _bmm(lax.slice_in_dim(stack, 0, m, axis=free), bl, dims)
    return lax.slice_in_dim(r, 0, m, axis=1) + lax.slice_in_dim(r, m, 2 * m, axis=1) + r2


_P_AQ = _P_INV = _P_MRG = _P_UW = _P_WS = _P_O = _P_S = _bmm1


def _pick(n, target, mult):
    best = None
    for d in range(mult, min(n, target) + 1, mult):
        if n % d == 0:
            best = d
    return best if best is not None else n


def _params(sem):
    return pltpu.CompilerParams(dimension_semantics=sem, vmem_limit_bytes=V7X_VMEM_LIMIT)


def _sigmoid(x):
    return 1.0 / (1.0 + jnp.exp(-x))


def _softplus(x):
    return jnp.maximum(x, 0.0) + jnp.log(1.0 + jnp.exp(-jnp.abs(x)))


def _tri(n, kind):
    r = lax.broadcasted_iota(jnp.int32, (n, n), 0)
    c = lax.broadcasted_iota(jnp.int32, (n, n), 1)
    if kind == "lower_incl":
        return (c <= r).astype(F32)
    if kind == "upper_incl":
        return (r <= c).astype(F32)
    raise ValueError(kind)


def _ada_kernel(c_ref, w_ref, b_ref, o_ref):
    c = c_ref[...]
    o_ref[...] = _mm(c * _sigmoid(c), w_ref[...], precision=HI) + b_ref[...]


def _ada(c_all, w_ada, b_ada):
    m, d = c_all.shape
    n = w_ada.shape[1]
    tn = _pick(n, 1024, 128)
    return pl.pallas_call(
        _ada_kernel,
        grid=(n // tn,),
        in_specs=[pl.BlockSpec((m, d), lambda j: (0, 0)),
                  pl.BlockSpec((d, tn), lambda j: (0, j)),
                  pl.BlockSpec((1, tn), lambda j: (0, j))],
        out_specs=pl.BlockSpec((m, tn), lambda j: (0, j)),
        out_shape=jax.ShapeDtypeStruct((m, n), F32),
        compiler_params=_params(("arbitrary",)),
        name="ada",
    )(c_all, w_ada, b_ada.reshape(1, n))


def _norm_proj_kernel(*refs, with_small):
    x_ref, sc_ref, sh_ref, g_ref, w_ref = refs[:5]
    if with_small:
        wsh_ref, wsl_ref, o_ref, os_ref, h_scr = refs[5:]
    else:
        o_ref, h_scr = refs[5:]

    @pl.when(pl.program_id(1) == 0)
    def _():
        x = x_ref[...]
        y = x * lax.rsqrt(jnp.mean(x * x, axis=-1, keepdims=True) + EPS) * g_ref[...]
        h = y * (1.0 + sc_ref[0]) + sh_ref[0]
        hb = h.astype(BF16)
        h_scr[...] = hb
        if with_small:
            hl = (h - hb.astype(F32)).astype(BF16)
            os_ref[...] = _mm(hb, wsh_ref[...]) + _mm(hb, wsl_ref[...]) + _mm(hl, wsh_ref[...])

    o_ref[...] = _mm(h_scr[...], w_ref[0])


def _norm_proj(x2d, sc, sh, g, w, ws=None, *, tm, tiles_per_b):
    r, d = x2d.shape
    n = w.shape[1]
    tn = _pick(n, 1024, 128)
    mr = sc.shape[1]
    mod_spec = pl.BlockSpec((1, mr, d), lambda i, j: (i // tiles_per_b, 0, 0))
    w_t = jnp.swapaxes(w.reshape(d, n // tn, tn), 0, 1)
    in_specs = [pl.BlockSpec((tm, d), lambda i, j: (i, 0)), mod_spec, mod_spec,
                pl.BlockSpec((1, d), lambda i, j: (0, 0)),
                pl.BlockSpec((1, d, tn), lambda i, j: (j, 0, 0))]
    args = [x2d, sc, sh, g.reshape(1, d), w_t]
    out_specs = pl.BlockSpec((tm, tn), lambda i, j: (i, j))
    out_shape = jax.ShapeDtypeStruct((r, n), F32)
    if ws is not None:
        ws_hi, ws_lo = ws
        ns = ws_hi.shape[1]
        in_specs += [pl.BlockSpec((d, ns), lambda i, j: (0, 0))] * 2
        args += [ws_hi, ws_lo]
        out_specs = [out_specs, pl.BlockSpec((tm, ns), lambda i, j: (i, 0))]
        out_shape = [out_shape, jax.ShapeDtypeStruct((r, ns), F32)]
    return pl.pallas_call(
        functools.partial(_norm_proj_kernel, with_small=ws is not None),
        grid=(r // tm, n // tn),
        in_specs=in_specs, out_specs=out_specs, out_shape=out_shape,
        scratch_shapes=[pltpu.VMEM((tm, d), BF16)],
        compiler_params=_params(("arbitrary", "arbitrary")),
        name="norm_proj",
    )(*args)


def _shift_rows(xe, s):
    return pltpu.roll(xe, s, 0)[GROUP:]


def _gdn_kernel(qkv_ref, z_ref, sm_ref, smt_ref, wc_ref, arow_ref, acol_ref, gn_ref, s0_ref,
                o_ref, sout_ref, s_scr, prev_scr, *, heads, chunk, n_chunks, front_pad):
    t = pl.program_id(1)
    tc = chunk * n_chunks
    aw = heads * 128

    @pl.when(t == 0)
    def _():
        s_scr[...] = s0_ref[0]
        prev_scr[...] = jnp.zeros_like(prev_scr)

    row = lax.broadcasted_iota(jnp.int32, (tc, 1), 0)
    col = lax.broadcasted_iota(jnp.int32, (1, tc), 1)
    valid_c = row >= front_pad
    valid_r = col >= front_pad

    sm = sm_ref[0]
    g_tile = jnp.where(valid_c, -jnp.exp(arow_ref[0:1, :]) * _softplus(sm + arow_ref[1:2, :]), 0.0)
    beta_tile = jnp.where(valid_c, _sigmoid(sm), 0.0)
    smt = smt_ref[0]
    gt_all = jnp.where(valid_r, -jnp.exp(acol_ref[:, 0:1]) * _softplus(smt + acol_ref[:, 1:2]), 0.0)

    lo_incl = _tri(chunk, "lower_incl")
    up_incl = _tri(chunk, "upper_incl")
    ri = lax.broadcasted_iota(jnp.int32, (chunk, chunk), 0)
    ci = lax.broadcasted_iota(jnp.int32, (chunk, chunk), 1)
    incl = ci <= ri
    strict = ci < ri
    eye = (ci == ri).astype(F32)

    gcol_tiles = [_mm(lo_incl, g_tile[c * chunk:(c + 1) * chunk], precision=HI) for c in range(n_chunks)]
    grow_tiles = [_mm(gt_all[:, c * chunk:(c + 1) * chunk], up_incl, precision=HI) for c in range(n_chunks)]

    wc = wc_ref[...]
    per_head = []
    for h in range(heads):
        parts = []
        for p in range(3):
            lo = p * aw + h * 128
            x = qkv_ref[0, :, lo:lo + 128]
            xe = jnp.concatenate([prev_scr[:, lo:lo + 128], x], axis=0)
            conv = (wc[3:4, lo:lo + 128] * x + wc[2:3, lo:lo + 128] * _shift_rows(xe, 1)
                    + wc[1:2, lo:lo + 128] * _shift_rows(xe, 2) + wc[0:1, lo:lo + 128] * _shift_rows(xe, 3))
            parts.append(conv * _sigmoid(conv))
        q_all, k_all, v_all = parts
        q_all = q_all * lax.rsqrt(jnp.sum(q_all * q_all, axis=-1, keepdims=True) + EPS) * (128.0 ** -0.5)
        k_all = k_all * lax.rsqrt(jnp.sum(k_all * k_all, axis=-1, keepdims=True) + EPS)
        per_head.append((q_all, jnp.where(valid_c, k_all, 0.0), v_all))

    units = [(c, h) for c in range(n_chunks) for h in range(heads)]
    rows = lambda c: slice(c * chunk, (c + 1) * chunk)
    q = jnp.stack([per_head[h][0][rows(c)] for c, h in units])
    k = jnp.stack([per_head[h][1][rows(c)] for c, h in units])
    v = jnp.stack([per_head[h][2][rows(c)] for c, h in units])
    gcol = jnp.stack([gcol_tiles[c][:, heads + h:heads + h + 1] for c, h in units])
    grow = jnp.stack([grow_tiles[c][heads + h:heads + h + 1, :] for c, h in units])
    bcol = jnp.stack([beta_tile[rows(c), h:h + 1] for c, h in units])

    decay = jnp.where(incl, jnp.exp(jnp.where(incl, gcol - grow, 0.0)), 0.0)
    kb = k * bcol
    vb = v * bcol
    aq = _P_AQ(jnp.concatenate([kb, q], axis=1), k, BNT)
    lower = jnp.where(strict, aq[:, :chunk] * decay, 0.0)
    qk = jnp.where(incl, aq[:, chunk:] * decay, 0.0)
    base = min(GROUP, chunk)
    same_blk = lambda s: (ri >> (s.bit_length() - 1)) == (ci >> (s.bit_length() - 1))
    neg_bd = jnp.where(same_blk(base), -lower, 0.0)
    nm = _P_INV(neg_bd, neg_bd, BNN)
    tinv = eye + neg_bd
    n_base = base.bit_length() - 2
    for lvl in range(n_base):
        if lvl < n_base - 1:
            r = _P_INV(jnp.concatenate([nm, tinv], axis=1), nm, BNN)
            tinv = tinv + r[:, chunk:]
            nm = r[:, :chunk]
        else:
            tinv = tinv + _P_INV(tinv, nm, BNN)
    s = base
    while s < chunk:
        off = jnp.where(same_blk(2 * s) & jnp.logical_not(same_blk(s)), lower, 0.0)
        tinv = tinv - _P_MRG(tinv, _P_MRG(off, tinv, BNN), BNN)
        s *= 2
    eg = jnp.exp(gcol)
    uw = _P_UW(tinv, jnp.concatenate([vb, kb * eg], axis=2), BNN)
    wq = jnp.concatenate([uw[:, :, 128:], q * eg], axis=1)
    g_last = gcol[:, chunk - 1:chunk, :]
    kd = k * jnp.exp(g_last - gcol)
    e_last = jnp.exp(g_last)

    s_all = s_scr[...]
    for c in range(n_chunks):
        us = slice(c * heads, (c + 1) * heads)
        ws = _P_WS(wq[us], s_all, BNN)
        v_new = uw[us, :, :128] - ws[:, :chunk]
        o = ws[:, chunk:] + _P_O(qk[us], v_new, BNN)
        s_all = s_all * e_last[us] + _P_S(kd[us], v_new, BTN)
        on = o * lax.rsqrt(jnp.mean(o * o, axis=-1, keepdims=True) + EPS) * gn_ref[...]
        for h in range(heads):
            z = z_ref[0, rows(c), h * 128:(h + 1) * 128]
            o_ref[0, rows(c), h * 128:(h + 1) * 128] = (on[h] * (z * _sigmoid(z))).astype(o_ref.dtype)
    s_scr[...] = s_all

    prev_scr[...] = qkv_ref[0, tc - GROUP:, :]

    @pl.when(t == pl.num_programs(1) - 1)
    def _():
        sout_ref[0] = s_scr[...]


def _gdn(proj3, small3, smallt3, w_conv, a_row, a_col, gn, s0, *, heads, front_pad):
    b, t, _ = proj3.shape
    aw = heads * 128
    chunk = min(GDN_CHUNK, t)
    tc = t if t <= 128 else 128
    n_chunks = tc // chunk
    rows_t = smallt3.shape[1]
    kern = functools.partial(_gdn_kernel, heads=heads, chunk=chunk, n_chunks=n_chunks, front_pad=front_pad)
    return pl.pallas_call(
        kern,
        grid=(b, t // tc),
        in_specs=[pl.BlockSpec((1, tc, 3 * aw), lambda i, j: (i, j, 0)),
                  pl.BlockSpec((1, tc, aw), lambda i, j: (i, j, 3)),
                  pl.BlockSpec((1, tc, 128), lambda i, j: (i, j, 0)),
                  pl.BlockSpec((1, rows_t, tc), lambda i, j: (i, 0, j)),
                  pl.BlockSpec((4, 3 * aw), lambda i, j: (0, 0)),
                  pl.BlockSpec((2, 128), lambda i, j: (0, 0)),
                  pl.BlockSpec((rows_t, 2), lambda i, j: (0, 0)),
                  pl.BlockSpec((1, 128), lambda i, j: (0, 0)),
                  pl.BlockSpec((1, heads, 128, 128), lambda i, j: (i, 0, 0, 0))],
        out_specs=[pl.BlockSpec((1, tc, aw), lambda i, j: (i, j, 0)),
                   pl.BlockSpec((1, heads, 128, 128), lambda i, j: (i, 0, 0, 0))],
        out_shape=[jax.ShapeDtypeStruct((b, t, aw), BF16),
                   jax.ShapeDtypeStruct((b, heads, 128, 128), F32)],
        scratch_shapes=[pltpu.VMEM((heads, 128, 128), F32), pltpu.VMEM((GROUP, 3 * aw), F32)],
        compiler_params=_params(("arbitrary", "arbitrary")),
        name="gdn",
    )(proj3, proj3, small3, smallt3, w_conv, a_row, a_col, gn, s0)


def _fox_prep_kernel(q_ref, k_ref, v_ref, sm_ref, smt_ref, qg_ref, kg_ref, brow_ref, bcol_ref,
                     qn_ref, kn_ref, knb_ref, vb_ref, logf_ref, cr_ref, carry_r,
                     *, heads, front_pad, q_scale):
    t = pl.program_id(1)
    tr = q_ref.shape[1]

    @pl.when(t == 0)
    def _():
        carry_r[...] = jnp.zeros_like(carry_r)

    for h in range(heads):
        cs = slice(h * 128, (h + 1) * 128)
        q = q_ref[0, :, cs]
        qn = q * lax.rsqrt(jnp.mean(q * q, axis=-1, keepdims=True) + EPS) * qg_ref[...]
        qn_ref[0, :, cs] = (qn * q_scale).astype(BF16)
        k = k_ref[0, :, cs]
        kn = k * lax.rsqrt(jnp.mean(k * k, axis=-1, keepdims=True) + EPS) * kg_ref[...]
        kn_ref[0, :, cs] = kn
        knb_ref[0, :, cs] = kn.astype(BF16)
    vb_ref[0] = v_ref[0].astype(BF16)

    col = lax.broadcasted_iota(jnp.int32, (1, tr), 1)
    logf_ref[0] = -_softplus(-(sm_ref[0] + brow_ref[...]))
    logft = jnp.where(col >= front_pad, -_softplus(-(smt_ref[0] + bcol_ref[...])), 0.0)
    cumt = _mm(logft, _tri(tr, "upper_incl"), precision=HI) + carry_r[:, 0:1]
    cr_ref[0] = cumt
    carry_r[...] = jnp.broadcast_to(cumt[:, tr - 1:tr], carry_r.shape)


def _fox_prep(proj3, small3, smallt3, qg, kg, brow, bcol, *, heads, front_pad, q_scale):
    b, t, _ = proj3.shape
    bw = heads * 128
    tr = _pick(t, 256, 128) if t >= 128 else t
    rows_t = smallt3.shape[1]
    wide = lambda blk: pl.BlockSpec((1, tr, bw), lambda i, j, blk=blk: (i, j, blk))
    out_w = pl.BlockSpec((1, tr, bw), lambda i, j: (i, j, 0))
    out_s = pl.BlockSpec((1, tr, 128), lambda i, j: (i, j, 0))
    return pl.pallas_call(
        functools.partial(_fox_prep_kernel, heads=heads, front_pad=front_pad, q_scale=q_scale),
        grid=(b, t // tr),
        in_specs=[wide(4), wide(5), wide(6),
                  pl.BlockSpec((1, tr, 128), lambda i, j: (i, j, 0)),
                  pl.BlockSpec((1, rows_t, tr), lambda i, j: (i, 0, j)),
                  pl.BlockSpec((1, 128), lambda i, j: (0, 0)),
                  pl.BlockSpec((1, 128), lambda i, j: (0, 0)),
                  pl.BlockSpec((1, 128), lambda i, j: (0, 0)),
                  pl.BlockSpec((rows_t, 1), lambda i, j: (0, 0))],
        out_specs=[out_w, out_w, out_w, out_w, out_s,
                   pl.BlockSpec((1, rows_t, tr), lambda i, j: (i, 0, j))],
        out_shape=[jax.ShapeDtypeStruct((b, t, bw), BF16), jax.ShapeDtypeStruct((b, t, bw), F32),
                   jax.ShapeDtypeStruct((b, t, bw), BF16), jax.ShapeDtypeStruct((b, t, bw), BF16),
                   jax.ShapeDtypeStruct((b, t, 128), F32),
                   jax.ShapeDtypeStruct((b, rows_t, t), F32)],
        scratch_shapes=[pltpu.VMEM((rows_t, 128), F32)],
        compiler_params=_params(("arbitrary", "arbitrary")),
        name="fox_prep",
    )(proj3, proj3, proj3, small3, smallt3, qg, kg, brow, bcol)


def _flash_kernel(q_ref, k_ref, v_ref, cr_ref, gate_ref, o_ref, *, tq, tk, n_sub):
    qi = pl.program_id(2)
    ts = tq // n_sub
    qs = [q_ref[0, a * ts:(a + 1) * ts, :] for a in range(n_sub)]
    q0 = qi * tq
    jd = q0 // tk

    def step(j, carry, masked):
        start = pl.multiple_of(j * tk, tk)
        ks = k_ref[0, :, pl.ds(start, tk)]
        vs = v_ref[0, pl.ds(start, tk), :]
        ck = cr_ref[0, 0, :, pl.ds(start, tk)] * LOG2E
        ss = [_mm(q, ks) - ck for q in qs]
        if masked:
            ci = lax.broadcasted_iota(jnp.int32, (ts, tk), 1) + start
            ss = [jnp.where(ci <= lax.broadcasted_iota(jnp.int32, (ts, tk), 0) + (q0 + a * ts), s, NEG)
                  for a, s in enumerate(ss)]
        m_new = [jnp.maximum(c[0], jnp.max(s, axis=-1, keepdims=True)) for c, s in zip(carry, ss)]
        ps = [jnp.exp2(s - m) for s, m in zip(ss, m_new)]
        pv = [_mm(p.astype(BF16), vs) for p in ps]
        out = []
        for (m, l, acc), mn, p, o in zip(carry, m_new, ps, pv):
            corr = jnp.exp2(m - mn)
            out.append((mn, l * corr + jnp.sum(p, axis=-1, keepdims=True), acc * corr + o))
        return tuple(out)

    init = tuple((jnp.full((ts, 1), NEG, F32), jnp.zeros((ts, 1), F32), jnp.zeros((ts, 128), F32))
                 for _ in range(n_sub))
    carry = lax.fori_loop(0, jd, lambda j, c: step(j, c, False), init)
    carry = step(jd, carry, True)
    for a, (m, l, acc) in enumerate(carry):
        g = gate_ref[0, a * ts:(a + 1) * ts, :]
        o_ref[0, a * ts:(a + 1) * ts, :] = (acc / l * _sigmoid(g)).astype(o_ref.dtype)


def _flash(qn, knb, vb, cr4, proj3, *, heads):
    b, t, bw = qn.shape
    tk = _pick(t, 512, 128)
    tq = tk
    n_sub = 2 if tq % 256 == 0 else 1
    qspec = pl.BlockSpec((1, tq, 128), lambda i, h, j: (i, j, h))
    kvspec = pl.BlockSpec((1, t, 128), lambda i, h, j: (i, 0, h))
    ktspec = pl.BlockSpec((1, 128, t), lambda i, h, j: (i, h, 0))
    knb = jnp.swapaxes(knb, 1, 2)
    return pl.pallas_call(
        functools.partial(_flash_kernel, tq=tq, tk=tk, n_sub=n_sub),
        grid=(b, heads, t // tq),
        in_specs=[qspec, ktspec, kvspec,
                  pl.BlockSpec((1, 1, 1, t), lambda i, h, j: (i, h, 0, 0)),
                  pl.BlockSpec((1, tq, 128), lambda i, h, j: (i, j, 7 * heads + h))],
        out_specs=qspec,
        out_shape=jax.ShapeDtypeStruct((b, t, bw), BF16),
        compiler_params=_params(("arbitrary", "arbitrary", "arbitrary")),
        name="fox_flash",
    )(qn, knb, vb, cr4, proj3)


def _suffix_kernel(lf_ref, o_ref, *, heads):
    x = lf_ref[...]
    n = x.shape[1]
    lane = lax.broadcasted_iota(jnp.int32, (1, n), 1)
    incl = x
    tot = x
    d = heads
    while d < n:
        incl = incl + jnp.where(lane < n - d, pltpu.roll(incl, n - d, 1), 0.0)
        tot = tot + pltpu.roll(tot, d, 1)
        d *= 2
    o_ref[:, :n] = incl - x
    o_ref[:, n:] = tot


def _suffix(logf_flat, *, heads):
    n_pool, n = logf_flat.shape
    gp = _pick(n_pool, 256, GROUP)
    return pl.pallas_call(
        functools.partial(_suffix_kernel, heads=heads),
        grid=(n_pool // gp,),
        in_specs=[pl.BlockSpec((gp, n), lambda i: (i, 0))],
        out_specs=pl.BlockSpec((gp, 2 * n), lambda i: (i, 0)),
        out_shape=jax.ShapeDtypeStruct((n_pool, 2 * n), F32),
        compiler_params=_params(("arbitrary",)),
        name="page_suffix",
    )(logf_flat)


def _paged_kernel(pt_ref, *refs, heads, n_new, g_pages):
    del pt_ref
    qn_ref, knb_ref, vb_ref, cr_ref, gate_ref = refs[:5]
    k_refs = refs[5:5 + g_pages]
    v_refs = refs[5 + g_pages:5 + 2 * g_pages]
    s_refs = refs[5 + 2 * g_pages:5 + 3 * g_pages]
    o_ref, q_scr, m_scr, l_scr, acc_scr, tail_scr, cn_scr = refs[5 + 3 * g_pages:]
    p = pl.program_id(1)
    bw = heads * 128
    nr = n_new * heads
    first = GROUP - n_new
    n = tail_scr.shape[1]
    row_head = lax.broadcasted_iota(jnp.int32, (nr, 1), 0) & (heads - 1)

    @pl.when(p == 0)
    def _():
        lane_head = lax.broadcasted_iota(jnp.int32, (heads, bw), 1) // 128
        head_mask = lane_head == lax.broadcasted_iota(jnp.int32, (heads, bw), 0)
        qn = qn_ref[0].astype(F32)
        qbd = jnp.concatenate(
            [jnp.where(head_mask, jnp.broadcast_to(qn[first + i:first + i + 1, :], (heads, bw)), 0.0)
             for i in range(n_new)], axis=0)
        q_all = qbd[:, 0:128]
        for h in range(1, heads):
            q_all = q_all + qbd[:, h * 128:(h + 1) * 128]
        q_scr[...] = q_all.astype(BF16)
        cr = cr_ref[0][2 * heads:3 * heads, :]
        cn_col = jnp.concatenate([cr[:, first + i:first + i + 1] for i in range(n_new)], axis=0)
        cn_scr[...] = jnp.broadcast_to(cn_col, cn_scr.shape)
        cn_key = jnp.concatenate([cr] * n_new, axis=0)
        s = _mm(qbd.astype(BF16), knb_ref[0], NT) + cn_col - cn_key
        ri = lax.broadcasted_iota(jnp.int32, (nr, GROUP), 0) // heads
        ci = lax.broadcasted_iota(jnp.int32, (nr, GROUP), 1)
        s = jnp.where((ci >= first) & (ci - first <= ri), s, NEG)
        m0 = jnp.max(s, axis=-1, keepdims=True)
        p0 = jnp.exp(s - m0)
        m_scr[...] = jnp.broadcast_to(m0, m_scr.shape)
        l_scr[...] = jnp.broadcast_to(jnp.sum(p0, axis=-1, keepdims=True), l_scr.shape)
        full = _mm(p0.astype(BF16), vb_ref[0])
        acc0 = jnp.where(row_head == 0, full[:, 0:128], 0.0)
        for h in range(1, heads):
            acc0 = acc0 + jnp.where(row_head == h, full[:, h * 128:(h + 1) * 128], 0.0)
        acc_scr[...] = acc0
        tail_scr[...] = jnp.zeros_like(tail_scr)

    wide = lambda a: jnp.concatenate([a] * (n // 128), axis=1)
    q_all = q_scr[...]
    cn_w = wide(cn_scr[...])
    m = m_scr[...]
    l = l_scr[...]
    acc = acc_scr[...]
    tail = tail_scr[...]
    valid = (lax.broadcasted_iota(jnp.int32, (nr, n), 1) & (heads - 1)) == row_head
    scores = []
    for i in range(g_pages):
        blk = s_refs[i][0]
        bias = blk[:, :n] + tail
        tail = tail + blk[:, n:]
        kf = k_refs[i][0, 0].reshape(n, 128).astype(BF16)
        scores.append(jnp.where(valid, _mm(q_all, kf, NT) + bias + cn_w, NEG))
    s_max = scores[0]
    for s in scores[1:]:
        s_max = jnp.maximum(s_max, s)
    m_new = jnp.maximum(m, jnp.max(s_max, axis=-1, keepdims=True))
    corr = jnp.exp(m - m_new)
    m_w = wide(m_new)
    probs = [jnp.exp(s - m_w) for s in scores]
    p_sum = probs[0]
    for pr in probs[1:]:
        p_sum = p_sum + pr
    pv = _mm(probs[0].astype(BF16), v_refs[0][0, 0].reshape(n, 128).astype(BF16))
    for i in range(1, g_pages):
        pv = pv + _mm(probs[i].astype(BF16), v_refs[i][0, 0].reshape(n, 128).astype(BF16))
    l = l * corr + jnp.sum(p_sum, axis=-1, keepdims=True)
    acc = acc * corr + pv
    m_scr[...] = m_new
    l_scr[...] = l
    acc_scr[...] = acc
    tail_scr[...] = tail

    @pl.when(p == pl.num_programs(1) - 1)
    def _():
        o = jnp.concatenate([acc / l] * heads, axis=1)
        keep = (lax.broadcasted_iota(jnp.int32, (nr, bw), 1) // 128) == row_head
        o = jnp.where(keep, o, 0.0)
        rows = [jnp.zeros((first, bw), F32)]
        for i in range(n_new):
            rows.append(jnp.sum(o[i * heads:(i + 1) * heads, :], axis=0, keepdims=True))
        out = jnp.concatenate(rows, axis=0)
        o_ref[0] = (out * _sigmoid(gate_ref[0])).astype(o_ref.dtype)


def _paged(page_table, qn, knb, vb, cr, proj3, cache_k, cache_v, suffix3, *, heads, n_new):
    bd, _, bw = qn.shape
    n_pages = page_table.shape[1]
    page = cache_k.shape[2]
    n = page * heads
    g_pages = _pick(n_pages, 16, 1)
    rows_t = cr.shape[1]
    nr = n_new * heads
    seq = lambda blk_w, blk: pl.BlockSpec((1, GROUP, blk_w), lambda b, p, pt, blk=blk: (b, 0, blk))
    page_of = lambda b, p, pt, i: pt[b, n_pages - 1 - (p * g_pages + i)]
    kv_spec = lambda i: pl.BlockSpec((1, 1, page, heads, 128),
                                     lambda b, p, pt, i=i: (0, page_of(b, p, pt, i), 0, 0, 0))
    suf_spec = lambda i: pl.BlockSpec((1, 1, 2 * n), lambda b, p, pt, i=i: (page_of(b, p, pt, i), 0, 0))

    in_specs = [seq(bw, 0), seq(bw, 0), seq(bw, 0),
                pl.BlockSpec((1, rows_t, GROUP), lambda b, p, pt: (b, 0, 0)),
                seq(bw, 7)]
    in_specs += [kv_spec(i) for i in range(g_pages)] * 2
    in_specs += [suf_spec(i) for i in range(g_pages)]
    grid_spec = pltpu.PrefetchScalarGridSpec(
        num_scalar_prefetch=1,
        grid=(bd, n_pages // g_pages),
        in_specs=in_specs,
        out_specs=pl.BlockSpec((1, GROUP, bw), lambda b, p, pt: (b, 0, 0)),
        scratch_shapes=[pltpu.VMEM((nr, 128), BF16), pltpu.VMEM((nr, 128), F32), pltpu.VMEM((nr, 128), F32),
                        pltpu.VMEM((nr, 128), F32), pltpu.VMEM((1, n), F32), pltpu.VMEM((nr, 128), F32)],
    )
    return pl.pallas_call(
        functools.partial(_paged_kernel, heads=heads, n_new=n_new, g_pages=g_pages),
        grid_spec=grid_spec,
        out_shape=jax.ShapeDtypeStruct((bd, GROUP, bw), BF16),
        compiler_params=_params(("arbitrary", "arbitrary")),
        name="fox_paged",
    )(page_table, qn, knb, vb, cr, proj3, *([cache_k] * g_pages), *([cache_v] * g_pages), *([suffix3] * g_pages))


def _out_proj_kernel(oa_ref, ob_ref, wa_ref, wb_ref, x_ref, gt_ref, g_ref, o_ref):
    mix = _mm(oa_ref[...], wa_ref[...]) + _mm(ob_ref[...], wb_ref[...])
    normed = mix * lax.rsqrt(jnp.mean(mix * mix, axis=-1, keepdims=True) + EPS) * g_ref[...]
    o_ref[...] = x_ref[...] + gt_ref[0] * normed


def _out_proj(oa, ob, wa, wb, x2d, gt, g, *, tm, tiles_per_b):
    r, d = x2d.shape
    aw, bw = oa.shape[1], ob.shape[1]
    mr = gt.shape[1]
    return pl.pallas_call(
        _out_proj_kernel,
        grid=(r // tm,),
        in_specs=[pl.BlockSpec((tm, aw), lambda i: (i, 0)),
                  pl.BlockSpec((tm, bw), lambda i: (i, 0)),
                  pl.BlockSpec((aw, d), lambda i: (0, 0)),
                  pl.BlockSpec((bw, d), lambda i: (0, 0)),
                  pl.BlockSpec((tm, d), lambda i: (i, 0)),
                  pl.BlockSpec((1, mr, d), lambda i: (i // tiles_per_b, 0, 0)),
                  pl.BlockSpec((1, d), lambda i: (0, 0))],
        out_specs=pl.BlockSpec((tm, d), lambda i: (i, 0)),
        out_shape=jax.ShapeDtypeStruct((r, d), F32),
        compiler_params=_params(("arbitrary",)),
        name="out_proj",
    )(oa, ob, wa, wb, x2d, gt, g.reshape(1, d))


def _ffn_tail_kernel(ug_ref, uv_ref, hg_ref, hv_ref, wcg_ref, wcv_ref, bg_ref, bv_ref, wd_ref, x_ref, gt_ref, g_ref,
                     o_ref, acc_scr, *, tiles_per_b):
    i = pl.program_id(0)
    j = pl.program_id(1)
    first = (i % tiles_per_b) == 0

    def conv(u_ref, halo_ref, wc_ref, b_ref):
        x = u_ref[...]
        halo = jnp.where(first, 0.0, halo_ref[...])
        xe = jnp.concatenate([halo, x], axis=0)
        wc = wc_ref[...]
        return wc[2:3] * x + wc[1:2] * _shift_rows(xe, 1) + wc[0:1] * _shift_rows(xe, 2) + b_ref[...]

    gate = conv(ug_ref, hg_ref, wcg_ref, bg_ref)
    val = conv(uv_ref, hv_ref, wcv_ref, bv_ref)
    act = (gate * _sigmoid(gate) * val).astype(BF16)
    part = _mm(act, wd_ref[...])

    @pl.when(j == 0)
    def _():
        acc_scr[...] = part

    @pl.when(j > 0)
    def _():
        acc_scr[...] += part

    @pl.when(j == pl.num_programs(1) - 1)
    def _():
        y = acc_scr[...]
        normed = y * lax.rsqrt(jnp.mean(y * y, axis=-1, keepdims=True) + EPS) * g_ref[...]
        o_ref[...] = x_ref[...] + gt_ref[0] * normed


def _ffn_tail(up, wc, bc, wd, x2d, gt, g, *, tm, tiles_per_b, tf):
    r, d = x2d.shape
    fp = wd.shape[0]
    nf = fp // tf
    hb = tm // GROUP
    mr = gt.shape[1]
    halo = lambda off: pl.BlockSpec((GROUP, tf), lambda i, j, off=off: (jnp.maximum(i * hb - 1, 0), j + off))
    return pl.pallas_call(
        functools.partial(_ffn_tail_kernel, tiles_per_b=tiles_per_b),
        grid=(r // tm, nf),
        in_specs=[pl.BlockSpec((tm, tf), lambda i, j: (i, j)),
                  pl.BlockSpec((tm, tf), lambda i, j: (i, j + nf)),
                  halo(0), halo(nf),
                  pl.BlockSpec((3, tf), lambda i, j: (0, j)),
                  pl.BlockSpec((3, tf), lambda i, j: (0, j + nf)),
                  pl.BlockSpec((1, tf), lambda i, j: (0, j)),
                  pl.BlockSpec((1, tf), lambda i, j: (0, j + nf)),
                  pl.BlockSpec((tf, d), lambda i, j: (j, 0)),
                  pl.BlockSpec((tm, d), lambda i, j: (i, 0)),
                  pl.BlockSpec((1, mr, d), lambda i, j: (i // tiles_per_b, 0, 0)),
                  pl.BlockSpec((1, d), lambda i, j: (0, 0))],
        out_specs=pl.BlockSpec((tm, d), lambda i, j: (i, 0)),
        out_shape=jax.ShapeDtypeStruct((r, d), F32),
        scratch_shapes=[pltpu.VMEM((tm, d), F32)],
        compiler_params=_params(("arbitrary", "arbitrary")),
        name="ffn_tail",
    )(up, up, up, up, wc, wc, bc, bc, wd, x2d, gt, g.reshape(1, d))


def _ffn_fused_kernel(x_ref, sc_ref, sh_ref, gpre_ref, wg_ref, wv_ref, wcg_ref, wcv_ref, bg_ref, bv_ref, wd_ref,
                      gt_ref, gpost_ref, o_ref, lg_ref, lv_ref, h_scr, cg_scr, cv_scr, xg_scr, xv_scr, act_scr,
                      *, tiles_per_b):
    i = pl.program_id(0)
    j = pl.program_id(1)
    tm = x_ref.shape[0]
    first = (i % tiles_per_b) == 0

    @pl.when(j == 0)
    def _():
        x = x_ref[...]
        y = x * lax.rsqrt(jnp.mean(x * x, axis=-1, keepdims=True) + EPS) * gpre_ref[...]
        h_scr[...] = (y * (1.0 + sc_ref[0]) + sh_ref[0]).astype(BF16)
        o_ref[...] = jnp.zeros_like(o_ref)

    tf = wd_ref.shape[0]
    ts = tm // FFN_SPLIT

    def project(s, w_ref, xe_scr):
        xe_scr[GROUP + s * ts:GROUP + (s + 1) * ts, :] = _mm(h_scr[s * ts:(s + 1) * ts, :], w_ref[0])

    xg_scr[0:GROUP, :] = jnp.where(first, 0.0, cg_scr[j])
    xv_scr[0:GROUP, :] = jnp.where(first, 0.0, cv_scr[j])
    for s in range(FFN_SPLIT):
        project(s, wg_ref, xg_scr)
        project(s, wv_ref, xv_scr)

    def conv(xe_scr, wc_ref, b_ref, r0, c0):
        xe = xe_scr[r0:r0 + FFN_ROWS + GROUP, c0:c0 + FFN_LANES]
        wc = wc_ref[:, c0:c0 + FFN_LANES]
        return (wc[2:3] * xe[GROUP:] + wc[1:2] * _shift_rows(xe, 1) + wc[0:1] * _shift_rows(xe, 2)
                + b_ref[:, c0:c0 + FFN_LANES])

    for s in range(FFN_SPLIT):
        for r0 in range(s * ts, (s + 1) * ts, FFN_ROWS):
            for c0 in range(0, tf, FFN_LANES):
                gate = conv(xg_scr, wcg_ref, bg_ref, r0, c0)
                val = conv(xv_scr, wcv_ref, bv_ref, r0, c0)
                act_scr[r0:r0 + FFN_ROWS, c0:c0 + FFN_LANES] = (gate * _sigmoid(gate) * val).astype(BF16)
        o_ref[s * ts:(s + 1) * ts, :] += _mm(act_scr[s * ts:(s + 1) * ts, :], wd_ref[...])

    for xe_scr, carry_scr, last_ref in ((xg_scr, cg_scr, lg_ref), (xv_scr, cv_scr, lv_ref)):
        last = xe_scr[tm:tm + GROUP, :]
        carry_scr[j] = last
        last_ref[0] = last

    @pl.when(j == pl.num_programs(1) - 1)
    def _():
        y = o_ref[...]
        normed = y * lax.rsqrt(jnp.mean(y * y, axis=-1, keepdims=True) + EPS) * gpost_ref[...]
        o_ref[...] = x_ref[...] + gt_ref[0] * normed


def _ffn_fused(x2d, sc, sh, gt, g_pre, g_post, w_up_t, wc, bc, wd, *, nb, tm):
    r, d = x2d.shape
    fp = wd.shape[0]
    tf = w_up_t.shape[2]
    nf = fp // tf
    tiles_per_b = r // nb // tm
    mod = pl.BlockSpec((1, 1, d), lambda i, j: (i // tiles_per_b, 0, 0))
    vec = pl.BlockSpec((1, d), lambda i, j: (0, 0))
    col = lambda rows, off: pl.BlockSpec((rows, tf), lambda i, j, off=off: (0, j + off))
    wtile = lambda off: pl.BlockSpec((1, d, tf), lambda i, j, off=off: (j + off, 0, 0))
    last = pl.BlockSpec((1, GROUP, tf), lambda i, j: (i, 0, j))
    rows_once = lambda: pl.BlockSpec((tm, d), lambda i, j: (i, 0), pipeline_mode=pl.Buffered(1))
    return pl.pallas_call(
        functools.partial(_ffn_fused_kernel, tiles_per_b=tiles_per_b),
        grid=(r // tm, nf),
        in_specs=[rows_once(), mod, mod, vec,
                  wtile(0), wtile(nf), col(3, 0), col(3, nf), col(1, 0), col(1, nf),
                  pl.BlockSpec((tf, d), lambda i, j: (j, 0)), mod, vec],
        out_specs=[rows_once(), last, last],
        out_shape=[jax.ShapeDtypeStruct((r, d), F32), jax.ShapeDtypeStruct((r // tm, GROUP, fp), F32),
                   jax.ShapeDtypeStruct((r // tm, GROUP, fp), F32)],
        scratch_shapes=[pltpu.VMEM((tm, d), BF16),
                        pltpu.VMEM((nf, GROUP, tf), F32), pltpu.VMEM((nf, GROUP, tf), F32),
                        pltpu.VMEM((tm + GROUP, tf), F32), pltpu.VMEM((tm + GROUP, tf), F32),
                        pltpu.VMEM((tm, tf), BF16)],
        compiler_params=_params(("arbitrary", "arbitrary")),
        name="ffn_fused",
    )(x2d, sc, sh, g_pre.reshape(1, d), w_up_t, w_up_t, wc, wc, bc, bc, wd, gt, g_post.reshape(1, d))


def _pad_cols(a, n):
    return jnp.pad(a, [(0, 0)] * (a.ndim - 1) + [(0, n - a.shape[-1])])


def _splice_rows(a, rows, start):
    k = rows.shape[1]
    padded = jnp.pad(rows, ((0, 0), (start, a.shape[1] - start - k), (0, 0)))
    idx = lax.broadcasted_iota(jnp.int32, (1, a.shape[1], 1), 1)
    return jnp.where((idx >= start) & (idx < start + k), padded, a)


def _split_hi_lo(w):
    hi = w.astype(BF16)
    return hi, (w - hi.astype(F32)).astype(BF16)


def _layer(x3, mods, st_conv, st_gdn, st_ffn, fox, lw, *, heads, front_pad, tm, q_scale):
    (g_pre_mix, g_post_mix, g_pre_ffn, g_post_ffn, w_big, ws, w_conv_qkv, a_row, a_col, gn, qg, kg, brow, bcol,
     w_out_a, w_out_b, w_up, w_conv_ffn, b_conv_ffn, w_down, d_ff, tf) = lw
    sh_m, sc_m, gt_m, sh_f, sc_f, gt_f = mods
    nb, t, d = x3.shape
    aw = heads * 128
    x2d = x3.reshape(nb * t, d)
    tiles_per_b = max(t // tm, 1) if mods[0].shape[1] == 1 else 1

    proj, small = _norm_proj(x2d, sc_m, sh_m, g_pre_mix, w_big, ws, tm=tm, tiles_per_b=tiles_per_b)
    proj3 = proj.reshape(nb, t, 8 * aw)
    if st_conv is not None:
        k = st_conv.shape[1]
        proj3 = _splice_rows(proj3, st_conv, GROUP - (t - front_pad) - k)
    small3 = small.reshape(nb, t, 128)
    rows_t = 3 * GROUP
    smallt3 = jnp.swapaxes(small3[:, :, :rows_t], 1, 2)

    o_a, gdn_new = _gdn(proj3, small3, smallt3, w_conv_qkv, a_row, a_col, gn, st_gdn,
                        heads=heads, front_pad=front_pad)
    qn, kn, knb, vb, logf, cr = _fox_prep(proj3, small3, smallt3, qg, kg, brow, bcol,
                                               heads=heads, front_pad=front_pad, q_scale=q_scale)
    o_b = fox(qn, knb, vb, cr, proj3)

    x1 = _out_proj(o_a.reshape(nb * t, aw), o_b.reshape(nb * t, aw), w_out_a, w_out_b, x2d, gt_m, g_post_mix,
                   tm=min(tm, 512), tiles_per_b=max(t // min(tm, 512), 1) if mods[0].shape[1] == 1 else 1)
    fp = w_down.shape[0]
    unpad = lambda g, v: jnp.concatenate([g[..., :d_ff], v[..., :d_ff]], axis=-1)
    if st_ffn is None:
        w_up_t = jnp.swapaxes(w_up.reshape(d, 2 * fp // tf, tf), 0, 1)
        y, last_g, last_v = _ffn_fused(x1, sc_f, sh_f, gt_f, g_pre_ffn, g_post_ffn, w_up_t, w_conv_ffn, b_conv_ffn,
                                       w_down, nb=nb, tm=tm)
        per_b = last_g.shape[0] // nb
        up_last = unpad(last_g[per_b - 1::per_b], last_v[per_b - 1::per_b])
    else:
        up = _norm_proj(x1, sc_f, sh_f, g_pre_ffn, w_up, tm=tm, tiles_per_b=tiles_per_b)
        up3 = up.reshape(nb, t, 2 * fp)
        k = st_ffn.shape[1]
        up3 = _splice_rows(up3, st_ffn, GROUP - (t - front_pad) - k)
        y = _ffn_tail(up3.reshape(nb * t, 2 * fp), w_conv_ffn, b_conv_ffn, w_down, x1, gt_f, g_post_ffn,
                      tm=min(tm, 512), tiles_per_b=1, tf=tf)
        up_last = unpad(up3[:, t - GROUP:, :fp], up3[:, t - GROUP:, fp:])
    return y.reshape(nb, t, d), proj3, kn, logf, gdn_new, up_last


def kernel(x_prompt, x_sample, cache_k, cache_v, cache_logf, state_gdn, state_conv_qkv, state_ffn_conv, page_table, c_prompt, c_sample, w_ada, b_ada, g_pre_mix, g_post_mix, g_pre_ffn, g_post_ffn, w_in, w_conv_qkv, a_log, dt_bias, g_gdn_norm, q_norm, k_norm, b_forget, w_out, w_up, w_conv_ffn, b_conv_ffn, w_down):
    depth = w_ada.shape[0]
    assert depth == 1, "single-layer step"
    b, t, d = x_prompt.shape
    bd, n_new, _ = x_sample.shape
    heads = state_gdn.shape[2]
    dh = state_gdn.shape[3]
    assert dh == 128 and cache_k.shape[3] == heads and n_new <= GROUP // 2
    aw = heads * dh
    page = cache_k.shape[2]
    n_pool = cache_k.shape[1]
    d_ff = w_down.shape[1]
    conv_a = w_conv_qkv.shape[1]
    ffn_conv = w_conv_ffn.shape[1]
    assert conv_a == 4 and ffn_conv == 3
    layer = 0
    (cache_logf, state_gdn, state_conv_qkv, state_ffn_conv, w_ada, b_ada, g_pre_mix, g_post_mix,
     g_pre_ffn, g_post_ffn, w_in, w_conv_qkv, a_log, dt_bias, g_gdn_norm, q_norm, k_norm, b_forget, w_out, w_up,
     w_conv_ffn, b_conv_ffn, w_down) = [
        (a.reshape(a.shape[1:]),) for a in
        (cache_logf, state_gdn, state_conv_qkv, state_ffn_conv, w_ada, b_ada, g_pre_mix, g_post_mix,
         g_pre_ffn, g_post_ffn, w_in, w_conv_qkv, a_log, dt_bias, g_gdn_norm, q_norm, k_norm, b_forget, w_out, w_up,
         w_conv_ffn, b_conv_ffn, w_down)]

    wi = w_in[layer]
    o1 = 4 * aw
    o2 = o1 + 2 * heads
    o3 = o2 + 4 * aw
    w_big = jnp.concatenate([wi[:, :o1], wi[:, o2:o3]], axis=1).astype(BF16)
    w_small = _pad_cols(jnp.concatenate([wi[:, o1:o2], wi[:, o3:]], axis=1), 128)
    ws = _split_hi_lo(w_small)
    zeros_h = jnp.zeros((heads,), F32)
    a_row = _pad_cols(jnp.stack([jnp.concatenate([zeros_h, a_log[layer]]),
                                 jnp.concatenate([zeros_h, dt_bias[layer]])]), 128)
    rows_t = 3 * GROUP
    a_col = jnp.pad(a_row[:, :rows_t].T, ((0, 0), (0, 0)))
    brow = _pad_cols(jnp.concatenate([zeros_h, zeros_h, b_forget[layer]])[None, :], 128)
    bcol = brow[:, :rows_t].T
    assert heads == GROUP and page == 128
    gn = g_gdn_norm[layer].reshape(1, dh)
    qg = q_norm[layer].reshape(1, dh)
    kg = k_norm[layer].reshape(1, dh)
    wo = w_out[layer].astype(BF16)
    w_out_a, w_out_b = wo[:aw], wo[aw:]
    tf = 512
    fp = -(-d_ff // tf) * tf
    wu = w_up[layer]
    w_up_p =jnp.concatenate([_pad_cols(wu[:, :d_ff], fp), _pad_cols(wu[:, d_ff:], fp)], axis=1).astype(BF16)
    wcf = w_conv_ffn[layer]
    w_conv_ffn_p =jnp.concatenate([_pad_cols(wcf[:, :d_ff], fp), _pad_cols(wcf[:, d_ff:], fp)], axis=1)
    bcf = b_conv_ffn[layer][None, :]
    b_conv_ffn_p = jnp.concatenate([_pad_cols(bcf[:, :d_ff], fp), _pad_cols(bcf[:, d_ff:], fp)], axis=1)
    w_down_p = jnp.pad(w_down[layer], ((0, fp - d_ff), (0, 0))).astype(BF16)
    lw = (g_pre_mix[layer], g_post_mix[layer], g_pre_ffn[layer], g_post_ffn[layer], w_big, ws, w_conv_qkv[layer],
          a_row, a_col, gn, qg, kg, brow, bcol, w_out_a, w_out_b, w_up_p, w_conv_ffn_p, b_conv_ffn_p, w_down_p,
          d_ff, tf)

    n_c = b + bd
    c_all = jnp.pad(jnp.concatenate([c_prompt, c_sample], axis=0), ((0, -n_c % GROUP), (0, 0)))
    mod = _ada(c_all, w_ada[layer], b_ada[layer])
    mods_p = [m[:b].reshape(b, 1, d) for m in jnp.split(mod, 6, axis=-1)]
    mods_s = [jnp.repeat(m[b:n_c], GROUP, axis=0).reshape(1, bd * GROUP, d) for m in jnp.split(mod, 6, axis=-1)]

    tm_p = _pick(t, 1024, 128)
    fox_p = lambda qn, knb, vb, cr, proj3: _flash(
        qn, knb, vb, cr[:, 2 * heads:3 * heads].reshape(b, heads, 1, t), proj3, heads=heads)
    zeros_s0 = jnp.zeros((b, heads, dh, dh), F32)
    y_p, proj_p, kn_p, logf_p, gdn_p, up_p = _layer(
        x_prompt, mods_p, None, zeros_s0, None, fox_p, lw, heads=heads, front_pad=0, tm=tm_p,
        q_scale=dh ** -0.5 * LOG2E)

    front = GROUP - n_new
    x_s = jnp.pad(x_sample, ((0, 0), (front, 0), (0, 0)))
    suffix = _suffix(cache_logf[layer].reshape(n_pool, page * heads), heads=heads)
    suffix3 = suffix.reshape(n_pool, 1, 2 * page * heads)
    fox_s = lambda qn, knb, vb, cr, proj3: _paged(
        page_table, qn, knb, vb, cr, proj3, cache_k, cache_v, suffix3, heads=heads, n_new=n_new)
    st_ffn = state_ffn_conv[layer]
    st_ffn_p =jnp.concatenate([_pad_cols(st_ffn[:, :, :d_ff], fp), _pad_cols(st_ffn[:, :, d_ff:], fp)], axis=-1)
    st_conv = _pad_cols(state_conv_qkv[layer], 8 * aw)
    y_s, proj_s, kn_s, logf_s, gdn_s, up_s = _layer(
        x_s, mods_s, st_conv, state_gdn[layer], st_ffn_p, fox_s, lw, heads=heads, front_pad=front, tm=bd * GROUP,
        q_scale=dh ** -0.5)

    n_pg = t // page
    k_prompt = kn_p.reshape(1, b, n_pg, page, heads, dh)
    v_prompt = proj_p[:, :, 6 * aw:7 * aw].reshape(1, b, n_pg, page, heads, dh)
    logf_prompt = logf_p[:, :, 2 * heads:3 * heads].reshape(1, b, n_pg, page, heads)
    conv_qkv_prompt = proj_p[:, t - (conv_a - 1):, :3 * aw][None]
    ffn_conv_prompt = up_p[:, GROUP - (ffn_conv - 1):, :][None]
    k_sample = kn_s[:, front:].reshape(1, bd, n_new, heads, dh)
    v_sample = proj_s[:, front:, 6 * aw:7 * aw].reshape(1, bd, n_new, heads, dh)
    logf_sample = logf_s[:, front:, 2 * heads:3 * heads][None]
    conv_qkv_sample = proj_s[:, GROUP - (conv_a - 1):, :3 * aw][None]
    ffn_conv_sample = up_s[:, GROUP - (ffn_conv - 1):, :][None]
    return (y_p, y_s[:, front:], k_prompt, v_prompt, logf_prompt, gdn_p[None], conv_qkv_prompt, ffn_conv_prompt,
            k_sample, v_sample, logf_sample, gdn_s.astype(state_gdn[layer].dtype)[None], conv_qkv_sample, ffn_conv_sample)
```

```python
import functools

import jax
import jax.numpy as jnp
from jax import lax
from jax.experimental import pallas as pl
from jax.experimental.pallas import tpu as pltpu

EPS = 1e-6
F32 = jnp.float32
BF16 = jnp.bfloat16
HI = lax.Precision.HIGHEST
NEG = -1e30
LOG2E = 1.4426950408889634
GDN_CHUNK = 64
GROUP = 8
FFN_ROWS, FFN_LANES = 64, 128
FFN_SPLIT = 2
V7X_VMEM_LIMIT = 56 * 1024 * 1024

NN = (((1,), (0,)), ((), ()))
NT = (((1,), (1,)), ((), ()))
TN = (((0,), (0,)), ((), ()))


def _mm(a, b, dims=NN, precision=None):
    return lax.dot_general(a, b, dims, precision=precision, preferred_element_type=F32)


BNN = (((2,), (1,)), ((0,), (0,)))
BNT = (((2,), (2,)), ((0,), (0,)))
BTN = (((1,), (1,)), ((0,), (0,)))


def _bmm(a, b, dims=BNN):
    return lax.dot_general(a, b, dims, preferred_element_type=F32)


def _bmm1(a, b, dims):
    return _bmm(a.astype(BF16), b.astype(BF16), dims)


def _bmm3(a, b, dims):
    free = 2 if dims == BTN else 1
    m = a.shape[free]
    ah = a.astype(BF16).astype(F32)
    bh = b.astype(BF16)
    bl = (b - bh.astype(F32)).astype(BF16)
    stack = jnp.concatenate([ah, a - ah], axis=free).astype(BF16)
    r = _bmm(stack, bh, dims)
    r2 = _bmm(lax.slice_in_dim(stack, 0, m, axis=free), bl, dims)
    return lax.slice_in_dim(r, 0, m, axis=1) + lax.slice_in_dim(r, m, 2 * m, axis=1) + r2


_P_AQ = _P_INV = _P_MRG = _P_UW = _P_WS = _P_O = _P_S = _bmm1


def _pick(n, target, mult):
    best = None
    for d in range(mult, min(n, target) + 1, mult):
        if n % d == 0:
            best = d
    return best if best is not None else n


def _params(sem):
    return pltpu.CompilerParams(dimension_semantics=sem, vmem_limit_bytes=V7X_VMEM_LIMIT)


def _sigmoid(x):
    return 1.0 / (1.0 + jnp.exp(-x))


def _softplus(x):
    return jnp.maximum(x, 0.0) + jnp.log(1.0 + jnp.exp(-jnp.abs(x)))


def _tri(n, kind):
    r = lax.broadcasted_iota(jnp.int32, (n, n), 0)
    c = lax.broadcasted_iota(jnp.int32, (n, n), 1)
    if kind == "lower_incl":
        return (c <= r).astype(F32)
    if kind == "upper_incl":
        return (r <= c).astype(F32)
    raise ValueError(kind)


def _ada_kernel(c_ref, w_ref, b_ref, o_ref):
    c = c_ref[...]
    o_ref[...] = _mm(c * _sigmoid(c), w_ref[...], precision=HI) + b_ref[...]


def _ada(c_all, w_ada, b_ada):
    m, d = c_all.shape
    n = w_ada.shape[1]
    tn = _pick(n, 1024, 128)
    return pl.pallas_call(
        _ada_kernel,
        grid=(n // tn,),
        in_specs=[pl.BlockSpec((m, d), lambda j: (0, 0)),
                  pl.BlockSpec((d, tn), lambda j: (0, j)),
                  pl.BlockSpec((1, tn), lambda j: (0, j))],
        out_specs=pl.BlockSpec((m, tn), lambda j: (0, j)),
        out_shape=jax.ShapeDtypeStruct((m, n), F32),
        compiler_params=_params(("arbitrary",)),
        name="ada",
    )(c_all, w_ada, b_ada.reshape(1, n))


def _norm_proj_kernel(*refs, with_small):
    x_ref, sc_ref, sh_ref, g_ref, w_ref = refs[:5]
    if with_small:
        wsh_ref, wsl_ref, o_ref, os_ref, h_scr = refs[5:]
    else:
        o_ref, h_scr = refs[5:]

    @pl.when(pl.program_id(1) == 0)
    def _():
        x = x_ref[...]
        y = x * lax.rsqrt(jnp.mean(x * x, axis=-1, keepdims=True) + EPS) * g_ref[...]
        h = y * (1.0 + sc_ref[0]) + sh_ref[0]
        hb = h.astype(BF16)
        h_scr[...] = hb
        if with_small:
            hl = (h - hb.astype(F32)).astype(BF16)
            os_ref[...] = _mm(hb, wsh_ref[...]) + _mm(hb, wsl_ref[...]) + _mm(hl, wsh_ref[...])

    o_ref[...] = _mm(h_scr[...], w_ref[0])


def _norm_proj(x2d, sc, sh, g, w, ws=None, *, tm, tiles_per_b):
    r, d = x2d.shape
    n = w.shape[1]
    tn = _pick(n, 1024, 128)
    mr = sc.shape[1]
    mod_spec = pl.BlockSpec((1, mr, d), lambda i, j: (i // tiles_per_b, 0, 0))
    w_t = jnp.swapaxes(w.reshape(d, n // tn, tn), 0, 1)
    in_specs = [pl.BlockSpec((tm, d), lambda i, j: (i, 0)), mod_spec, mod_spec,
                pl.BlockSpec((1, d), lambda i, j: (0, 0)),
                pl.BlockSpec((1, d, tn), lambda i, j: (j, 0, 0))]
    args = [x2d, sc, sh, g.reshape(1, d), w_t]
    out_specs = pl.BlockSpec((tm, tn), lambda i, j: (i, j))
    out_shape = jax.ShapeDtypeStruct((r, n), F32)
    if ws is not None:
        ws_hi, ws_lo = ws
        ns = ws_hi.shape[1]
        in_specs += [pl.BlockSpec((d, ns), lambda i, j: (0, 0))] * 2
        args += [ws_hi, ws_lo]
        out_specs = [out_specs, pl.BlockSpec((tm, ns), lambda i, j: (i, 0))]
        out_shape = [out_shape, jax.ShapeDtypeStruct((r, ns), F32)]
    return pl.pallas_call(
        functools.partial(_norm_proj_kernel, with_small=ws is not None),
        grid=(r // tm, n // tn),
        in_specs=in_specs, out_specs=out_specs, out_shape=out_shape,
        scratch_shapes=[pltpu.VMEM((tm, d), BF16)],
        compiler_params=_params(("arbitrary", "arbitrary")),
        name="norm_proj",
    )(*args)


def _shift_rows(xe, s):
    return pltpu.roll(xe, s, 0)[GROUP:]


def _gdn_kernel(qkv_ref, z_ref, sm_ref, smt_ref, wc_ref, arow_ref, acol_ref, gn_ref, s0_ref,
                o_ref, sout_ref, s_scr, prev_scr, *, heads, chunk, n_chunks, front_pad):
    t = pl.program_id(1)
    tc = chunk * n_chunks
    aw = heads * 128

    @pl.when(t == 0)
    def _():
        s_scr[...] = s0_ref[0]
        prev_scr[...] = jnp.zeros_like(prev_scr)

    row = lax.broadcasted_iota(jnp.int32, (tc, 1), 0)
    col = lax.broadcasted_iota(jnp.int32, (1, tc), 1)
    valid_c = row >= front_pad
    valid_r = col >= front_pad

    sm = sm_ref[0]
    g_tile = jnp.where(valid_c, -jnp.exp(arow_ref[0:1, :]) * _softplus(sm + arow_ref[1:2, :]), 0.0)
    beta_tile = jnp.where(valid_c, _sigmoid(sm), 0.0)
    smt = smt_ref[0]
    gt_all = jnp.where(valid_r, -jnp.exp(acol_ref[:, 0:1]) * _softplus(smt + acol_ref[:, 1:2]), 0.0)

    lo_incl = _tri(chunk, "lower_incl")
    up_incl = _tri(chunk, "upper_incl")
    ri = lax.broadcasted_iota(jnp.int32, (chunk, chunk), 0)
    ci = lax.broadcasted_iota(jnp.int32, (chunk, chunk), 1)
    incl = ci <= ri
    strict = ci < ri
    eye = (ci == ri).astype(F32)

    gcol_tiles = [_mm(lo_incl, g_tile[c * chunk:(c + 1) * chunk], precision=HI) for c in range(n_chunks)]
    grow_tiles = [_mm(gt_all[:, c * chunk:(c + 1) * chunk], up_incl, precision=HI) for c in range(n_chunks)]

    wc = wc_ref[...]
    per_head = []
    for h in range(heads):
        parts = []
        for p in range(3):
            lo = p * aw + h * 128
            x = qkv_ref[0, :, lo:lo + 128]
            xe = jnp.concatenate([prev_scr[:, lo:lo + 128], x], axis=0)
            conv = (wc[3:4, lo:lo + 128] * x + wc[2:3, lo:lo + 128] * _shift_rows(xe, 1)
                    + wc[1:2, lo:lo + 128] * _shift_rows(xe, 2) + wc[0:1, lo:lo + 128] * _shift_rows(xe, 3))
            parts.append(conv * _sigmoid(conv))
        q_all, k_all, v_all = parts
        q_all = q_all * lax.rsqrt(jnp.sum(q_all * q_all, axis=-1, keepdims=True) + EPS) * (128.0 ** -0.5)
        k_all = k_all * lax.rsqrt(jnp.sum(k_all * k_all, axis=-1, keepdims=True) + EPS)
        per_head.append((q_all, jnp.where(valid_c, k_all, 0.0), v_all))

    units = [(c, h) for c in range(n_chunks) for h in range(heads)]
    rows = lambda c: slice(c * chunk, (c + 1) * chunk)
    q = jnp.stack([per_head[h][0][rows(c)] for c, h in units])
    k = jnp.stack([per_head[h][1][rows(c)] for c, h in units])
    v = jnp.stack([per_head[h][2][rows(c)] for c, h in units])
    gcol = jnp.stack([gcol_tiles[c][:, heads + h:heads + h + 1] for c, h in units])
    grow = jnp.stack([grow_tiles[c][heads + h:heads + h + 1, :] for c, h in units])
    bcol = jnp.stack([beta_tile[rows(c), h:h + 1] for c, h in units])

    decay = jnp.where(incl, jnp.exp(jnp.where(incl, gcol - grow, 0.0)), 0.0)
    kb = k * bcol
    vb = v * bcol
    aq = _P_AQ(jnp.concatenate([kb, q], axis=1), k, BNT)
    lower = jnp.where(strict, aq[:, :chunk] * decay, 0.0)
    qk = jnp.where(incl, aq[:, chunk:] * decay, 0.0)
    base = min(GROUP, chunk)
    same_blk = lambda s: (ri >> (s.bit_length() - 1)) == (ci >> (s.bit_length() - 1))
    neg_bd = jnp.where(same_blk(base), -lower, 0.0)
    nm = _P_INV(neg_bd, neg_bd, BNN)
    tinv = eye + neg_bd
    n_base = base.bit_length() - 2
    for lvl in range(n_base):
        if lvl < n_base - 1:
            r = _P_INV(jnp.concatenate([nm, tinv], axis=1), nm, BNN)
            tinv = tinv + r[:, chunk:]
            nm = r[:, :chunk]
        else:
            tinv = tinv + _P_INV(tinv, nm, BNN)
    s = base
    while s < chunk:
        off = jnp.where(same_blk(2 * s) & jnp.logical_not(same_blk(s)), lower, 0.0)
        tinv = tinv - _P_MRG(tinv, _P_MRG(off, tinv, BNN), BNN)
        s *= 2
    eg = jnp.exp(gcol)
    uw = _P_UW(tinv, jnp.concatenate([vb, kb * eg], axis=2), BNN)
    wq = jnp.concatenate([uw[:, :, 128:], q * eg], axis=1)
    g_last = gcol[:, chunk - 1:chunk, :]
    kd = k * jnp.exp(g_last - gcol)
    e_last = jnp.exp(g_last)

    s_all = s_scr[...]
    for c in range(n_chunks):
        us = slice(c * heads, (c + 1) * heads)
        ws = _P_WS(wq[us], s_all, BNN)
        v_new = uw[us, :, :128] - ws[:, :chunk]
        o = ws[:, chunk:] + _P_O(qk[us], v_new, BNN)
        s_all = s_all * e_last[us] + _P_S(kd[us], v_new, BTN)
        on = o * lax.rsqrt(jnp.mean(o * o, axis=-1, keepdims=True) + EPS) * gn_ref[...]
        for h in range(heads):
            z = z_ref[0, rows(c), h * 128:(h + 1) * 128]
            o_ref[0, rows(c), h * 128:(h + 1) * 128] = (on[h] * (z * _sigmoid(z))).astype(o_ref.dtype)
    s_scr[...] = s_all

    prev_scr[...] = qkv_ref[0, tc - GROUP:, :]

    @pl.when(t == pl.num_programs(1) - 1)
    def _():
        sout_ref[0] = s_scr[...]


def _gdn(proj3, small3, smallt3, w_conv, a_row, a_col, gn, s0, *, heads, front_pad):
    b, t, _ = proj3.shape
    aw = heads * 128
    chunk = min(GDN_CHUNK, t)
    tc = t if t <= 128 else _pick(t, 256, 128)
    n_chunks = tc // chunk
    rows_t = smallt3.shape[1]
    kern = functools.partial(_gdn_kernel, heads=heads, chunk=chunk, n_chunks=n_chunks, front_pad=front_pad)
    return pl.pallas_call(
        kern,
        grid=(b, t // tc),
        in_specs=[pl.BlockSpec((1, tc, 3 * aw), lambda i, j: (i, j, 0)),
                  pl.BlockSpec((1, tc, aw), lambda i, j: (i, j, 3)),
                  pl.BlockSpec((1, tc, 128), lambda i, j: (i, j, 0)),
                  pl.BlockSpec((1, rows_t, tc), lambda i, j: (i, 0, j)),
                  pl.BlockSpec((4, 3 * aw), lambda i, j: (0, 0)),
                  pl.BlockSpec((2, 128), lambda i, j: (0, 0)),
                  pl.BlockSpec((rows_t, 2), lambda i, j: (0, 0)),
                  pl.BlockSpec((1, 128), lambda i, j: (0, 0)),
                  pl.BlockSpec((1, heads, 128, 128), lambda i, j: (i, 0, 0, 0))],
        out_specs=[pl.BlockSpec((1, tc, aw), lambda i, j: (i, j, 0)),
                   pl.BlockSpec((1, heads, 128, 128), lambda i, j: (i, 0, 0, 0))],
        out_shape=[jax.ShapeDtypeStruct((b, t, aw), BF16),
                   jax.ShapeDtypeStruct((b, heads, 128, 128), F32)],
        scratch_shapes=[pltpu.VMEM((heads, 128, 128), F32), pltpu.VMEM((GROUP, 3 * aw), F32)],
        compiler_params=_params(("arbitrary", "arbitrary")),
        name="gdn",
    )(proj3, proj3, small3, smallt3, w_conv, a_row, a_col, gn, s0)


def _fox_prep_kernel(q_ref, k_ref, v_ref, sm_ref, smt_ref, qg_ref, kg_ref, brow_ref, bcol_ref,
                     qn_ref, kn_ref, knb_ref, vb_ref, logf_ref, cr_ref, carry_r,
                     *, heads, front_pad, q_scale):
    t = pl.program_id(1)
    tr = q_ref.shape[1]

    @pl.when(t == 0)
    def _():
        carry_r[...] = jnp.zeros_like(carry_r)

    for h in range(heads):
        cs = slice(h * 128, (h + 1) * 128)
        q = q_ref[0, :, cs]
        qn = q * lax.rsqrt(jnp.mean(q * q, axis=-1, keepdims=True) + EPS) * qg_ref[...]
        qn_ref[0, :, cs] = (qn * q_scale).astype(BF16)
        k = k_ref[0, :, cs]
        kn = k * lax.rsqrt(jnp.mean(k * k, axis=-1, keepdims=True) + EPS) * kg_ref[...]
        kn_ref[0, :, cs] = kn
        knb_ref[0, :, cs] = kn.astype(BF16)
    vb_ref[0] = v_ref[0].astype(BF16)

    col = lax.broadcasted_iota(jnp.int32, (1, tr), 1)
    logf_ref[0] = -_softplus(-(sm_ref[0] + brow_ref[...]))
    logft = jnp.where(col >= front_pad, -_softplus(-(smt_ref[0] + bcol_ref[...])), 0.0)
    cumt = _mm(logft, _tri(tr, "upper_incl"), precision=HI) + carry_r[:, 0:1]
    cr_ref[0] = cumt
    carry_r[...] = jnp.broadcast_to(cumt[:, tr - 1:tr], carry_r.shape)


def _fox_prep(proj3, small3, smallt3, qg, kg, brow, bcol, *, heads, front_pad, q_scale):
    b, t, _ = proj3.shape
    bw = heads * 128
    tr = _pick(t, 256, 128) if t >= 128 else t
    rows_t = smallt3.shape[1]
    wide = lambda blk: pl.BlockSpec((1, tr, bw), lambda i, j, blk=blk: (i, j, blk))
    out_w = pl.BlockSpec((1, tr, bw), lambda i, j: (i, j, 0))
    out_s = pl.BlockSpec((1, tr, 128), lambda i, j: (i, j, 0))
    return pl.pallas_call(
        functools.partial(_fox_prep_kernel, heads=heads, front_pad=front_pad, q_scale=q_scale),
        grid=(b, t // tr),
        in_specs=[wide(4), wide(5), wide(6),
                  pl.BlockSpec((1, tr, 128), lambda i, j: (i, j, 0)),
                  pl.BlockSpec((1, rows_t, tr), lambda i, j: (i, 0, j)),
                  pl.BlockSpec((1, 128), lambda i, j: (0, 0)),
                  pl.BlockSpec((1, 128), lambda i, j: (0, 0)),
                  pl.BlockSpec((1, 128), lambda i, j: (0, 0)),
                  pl.BlockSpec((rows_t, 1), lambda i, j: (0, 0))],
        out_specs=[out_w, out_w, out_w, out_w, out_s,
                   pl.BlockSpec((1, rows_t, tr), lambda i, j: (i, 0, j))],
        out_shape=[jax.ShapeDtypeStruct((b, t, bw), BF16), jax.ShapeDtypeStruct((b, t, bw), F32),
                   jax.ShapeDtypeStruct((b, t, bw), BF16), jax.ShapeDtypeStruct((b, t, bw), BF16),
                   jax.ShapeDtypeStruct((b, t, 128), F32),
                   jax.ShapeDtypeStruct((b, rows_t, t), F32)],
        scratch_shapes=[pltpu.VMEM((rows_t, 128), F32)],
        compiler_params=_params(("arbitrary", "arbitrary")),
        name="fox_prep",
    )(proj3, proj3, proj3, small3, smallt3, qg, kg, brow, bcol)


def _flash_kernel(q_ref, k_ref, v_ref, cr_ref, gate_ref, o_ref, *, tq, tk, n_sub):
    qi = pl.program_id(2)
    ts = tq // n_sub
    qs = [q_ref[0, a * ts:(a + 1) * ts, :] for a in range(n_sub)]
    q0 = qi * tq
    jd = q0 // tk

    def step(j, carry, masked):
        start = pl.multiple_of(j * tk, tk)
        ks = k_ref[0, :, pl.ds(start, tk)]
        vs = v_ref[0, pl.ds(start, tk), :]
        ck = cr_ref[0, 0, :, pl.ds(start, tk)] * LOG2E
        ss = [_mm(q, ks) - ck for q in qs]
        if masked:
            ci = lax.broadcasted_iota(jnp.int32, (ts, tk), 1) + start
            ss = [jnp.where(ci <= lax.broadcasted_iota(jnp.int32, (ts, tk), 0) + (q0 + a * ts), s, NEG)
                  for a, s in enumerate(ss)]
        m_new = [jnp.maximum(c[0], jnp.max(s, axis=-1, keepdims=True)) for c, s in zip(carry, ss)]
        ps = [jnp.exp2(s - m) for s, m in zip(ss, m_new)]
        pv = [_mm(p.astype(BF16), vs) for p in ps]
        out = []
        for (m, l, acc), mn, p, o in zip(carry, m_new, ps, pv):
            corr = jnp.exp2(m - mn)
            out.append((mn, l * corr + jnp.sum(p, axis=-1, keepdims=True), acc * corr + o))
        return tuple(out)

    init = tuple((jnp.full((ts, 1), NEG, F32), jnp.zeros((ts, 1), F32), jnp.zeros((ts, 128), F32))
                 for _ in range(n_sub))
    carry = lax.fori_loop(0, jd, lambda j, c: step(j, c, False), init)
    carry = step(jd, carry, True)
    for a, (m, l, acc) in enumerate(carry):
        g = gate_ref[0, a * ts:(a + 1) * ts, :]
        o_ref[0, a * ts:(a + 1) * ts, :] = (acc / l * _sigmoid(g)).astype(o_ref.dtype)


def _flash(qn, knb, vb, cr4, proj3, *, heads):
    b, t, bw = qn.shape
    tk = _pick(t, 512, 128)
    tq = tk
    n_sub = 2 if tq % 256 == 0 else 1
    qspec = pl.BlockSpec((1, tq, 128), lambda i, h, j: (i, j, h))
    kvspec = pl.BlockSpec((1, t, 128), lambda i, h, j: (i, 0, h))
    ktspec = pl.BlockSpec((1, 128, t), lambda i, h, j: (i, h, 0))
    knb = jnp.swapaxes(knb, 1, 2)
    return pl.pallas_call(
        functools.partial(_flash_kernel, tq=tq, tk=tk, n_sub=n_sub),
        grid=(b, heads, t // tq),
        in_specs=[qspec, ktspec, kvspec,
                  pl.BlockSpec((1, 1, 1, t), lambda i, h, j: (i, h, 0, 0)),
                  pl.BlockSpec((1, tq, 128), lambda i, h, j: (i, j, 7 * heads + h))],
        out_specs=qspec,
        out_shape=jax.ShapeDtypeStruct((b, t, bw), BF16),
        compiler_params=_params(("arbitrary", "arbitrary", "arbitrary")),
        name="fox_flash",
    )(qn, knb, vb, cr4, proj3)


def _suffix_kernel(lf_ref, o_ref, *, heads):
    x = lf_ref[...]
    n = x.shape[1]
    lane = lax.broadcasted_iota(jnp.int32, (1, n), 1)
    incl = x
    tot = x
    d = heads
    while d < n:
        incl = incl + jnp.where(lane < n - d, pltpu.roll(incl, n - d, 1), 0.0)
        tot = tot + pltpu.roll(tot, d, 1)
        d *= 2
    o_ref[:, :n] = incl - x
    o_ref[:, n:] = tot


def _suffix(logf_flat, *, heads):
    n_pool, n = logf_flat.shape
    gp = _pick(n_pool, 256, GROUP)
    return pl.pallas_call(
        functools.partial(_suffix_kernel, heads=heads),
        grid=(n_pool // gp,),
        in_specs=[pl.BlockSpec((gp, n), lambda i: (i, 0))],
        out_specs=pl.BlockSpec((gp, 2 * n), lambda i: (i, 0)),
        out_shape=jax.ShapeDtypeStruct((n_pool, 2 * n), F32),
        compiler_params=_params(("arbitrary",)),
        name="page_suffix",
    )(logf_flat)


def _paged_kernel(pt_ref, *refs, heads, n_new, g_pages):
    del pt_ref
    qn_ref, knb_ref, vb_ref, cr_ref, gate_ref = refs[:5]
    k_refs = refs[5:5 + g_pages]
    v_refs = refs[5 + g_pages:5 + 2 * g_pages]
    s_refs = refs[5 + 2 * g_pages:5 + 3 * g_pages]
    o_ref, q_scr, m_scr, l_scr, acc_scr, tail_scr, cn_scr = refs[5 + 3 * g_pages:]
    p = pl.program_id(1)
    bw = heads * 128
    nr = n_new * heads
    first = GROUP - n_new
    n = tail_scr.shape[1]
    row_head = lax.broadcasted_iota(jnp.int32, (nr, 1), 0) & (heads - 1)

    @pl.when(p == 0)
    def _():
        lane_head = lax.broadcasted_iota(jnp.int32, (heads, bw), 1) // 128
        head_mask = lane_head == lax.broadcasted_iota(jnp.int32, (heads, bw), 0)
        qn = qn_ref[0].astype(F32)
        qbd = jnp.concatenate(
            [jnp.where(head_mask, jnp.broadcast_to(qn[first + i:first + i + 1, :], (heads, bw)), 0.0)
             for i in range(n_new)], axis=0)
        q_all = qbd[:, 0:128]
        for h in range(1, heads):
            q_all = q_all + qbd[:, h * 128:(h + 1) * 128]
        q_scr[...] = q_all.astype(BF16)
        cr = cr_ref[0][2 * heads:3 * heads, :]
        cn_col = jnp.concatenate([cr[:, first + i:first + i + 1] for i in range(n_new)], axis=0)
        cn_scr[...] = jnp.broadcast_to(cn_col, cn_scr.shape)
        cn_key = jnp.concatenate([cr] * n_new, axis=0)
        s = _mm(qbd.astype(BF16), knb_ref[0], NT) + cn_col - cn_key
        ri = lax.broadcasted_iota(jnp.int32, (nr, GROUP), 0) // heads
        ci = lax.broadcasted_iota(jnp.int32, (nr, GROUP), 1)
        s = jnp.where((ci >= first) & (ci - first <= ri), s, NEG)
        m0 = jnp.max(s, axis=-1, keepdims=True)
        p0 = jnp.exp(s - m0)
        m_scr[...] = jnp.broadcast_to(m0, m_scr.shape)
        l_scr[...] = jnp.broadcast_to(jnp.sum(p0, axis=-1, keepdims=True), l_scr.shape)
        full = _mm(p0.astype(BF16), vb_ref[0])
        acc0 = jnp.where(row_head == 0, full[:, 0:128], 0.0)
        for h in range(1, heads):
            acc0 = acc0 + jnp.where(row_head == h, full[:, h * 128:(h + 1) * 128], 0.0)
        acc_scr[...] = acc0
        tail_scr[...] = jnp.zeros_like(tail_scr)

    wide = lambda a: jnp.concatenate([a] * (n // 128), axis=1)
    q_all = q_scr[...]
    cn_w = wide(cn_scr[...])
    m = m_scr[...]
    l = l_scr[...]
    acc = acc_scr[...]
    tail = tail_scr[...]
    valid = (lax.broadcasted_iota(jnp.int32, (nr, n), 1) & (heads - 1)) == row_head
    scores = []
    for i in range(g_pages):
        blk = s_refs[i][0]
        bias = blk[:, :n] + tail
        tail = tail + blk[:, n:]
        kf = k_refs[i][0, 0].reshape(n, 128).astype(BF16)
        scores.append(jnp.where(valid, _mm(q_all, kf, NT) + bias + cn_w, NEG))
    s_max = scores[0]
    for s in scores[1:]:
        s_max = jnp.maximum(s_max, s)
    m_new = jnp.maximum(m, jnp.max(s_max, axis=-1, keepdims=True))
    corr = jnp.exp(m - m_new)
    m_w = wide(m_new)
    probs = [jnp.exp(s - m_w) for s in scores]
    p_sum = probs[0]
    for pr in probs[1:]:
        p_sum = p_sum + pr
    pv = _mm(probs[0].astype(BF16), v_refs[0][0, 0].reshape(n, 128).astype(BF16))
    for i in range(1, g_pages):
        pv = pv + _mm(probs[i].astype(BF16), v_refs[i][0, 0].reshape(n, 128).astype(BF16))
    l = l * corr + jnp.sum(p_sum, axis=-1, keepdims=True)
    acc = acc * corr + pv
    m_scr[...] = m_new
    l_scr[...] = l
    acc_scr[...] = acc
    tail_scr[...] = tail

    @pl.when(p == pl.num_programs(1) - 1)
    def _():
        o = jnp.concatenate([acc / l] * heads, axis=1)
        keep = (lax.broadcasted_iota(jnp.int32, (nr, bw), 1) // 128) == row_head
        o = jnp.where(keep, o, 0.0)
        rows = [jnp.zeros((first, bw), F32)]
        for i in range(n_new):
            rows.append(jnp.sum(o[i * heads:(i + 1) * heads, :], axis=0, keepdims=True))
        out = jnp.concatenate(rows, axis=0)
        o_ref[0] = (out * _sigmoid(gate_ref[0])).astype(o_ref.dtype)


def _paged(page_table, qn, knb, vb, cr, proj3, cache_k, cache_v, suffix3, *, heads, n_new):
    bd, _, bw = qn.shape
    n_pages = page_table.shape[1]
    page = cache_k.shape[2]
    n = page * heads
    g_pages = _pick(n_pages, 16, 1)
    rows_t = cr.shape[1]
    nr = n_new * heads
    seq = lambda blk_w, blk: pl.BlockSpec((1, GROUP, blk_w), lambda b, p, pt, blk=blk: (b, 0, blk))
    page_of = lambda b, p, pt, i: pt[b, n_pages - 1 - (p * g_pages + i)]
    kv_spec = lambda i: pl.BlockSpec((1, 1, page, heads, 128),
                                     lambda b, p, pt, i=i: (0, page_of(b, p, pt, i), 0, 0, 0))
    suf_spec = lambda i: pl.BlockSpec((1, 1, 2 * n), lambda b, p, pt, i=i: (page_of(b, p, pt, i), 0, 0))

    in_specs = [seq(bw, 0), seq(bw, 0), seq(bw, 0),
                pl.BlockSpec((1, rows_t, GROUP), lambda b, p, pt: (b, 0, 0)),
                seq(bw, 7)]
    in_specs += [kv_spec(i) for i in range(g_pages)] * 2
    in_specs += [suf_spec(i) for i in range(g_pages)]
    grid_spec = pltpu.PrefetchScalarGridSpec(
        num_scalar_prefetch=1,
        grid=(bd, n_pages // g_pages),
        in_specs=in_specs,
        out_specs=pl.BlockSpec((1, GROUP, bw), lambda b, p, pt: (b, 0, 0)),
        scratch_shapes=[pltpu.VMEM((nr, 128), BF16), pltpu.VMEM((nr, 128), F32), pltpu.VMEM((nr, 128), F32),
                        pltpu.VMEM((nr, 128), F32), pltpu.VMEM((1, n), F32), pltpu.VMEM((nr, 128), F32)],
    )
    return pl.pallas_call(
        functools.partial(_paged_kernel, heads=heads, n_new=n_new, g_pages=g_pages),
        grid_spec=grid_spec,
        out_shape=jax.ShapeDtypeStruct((bd, GROUP, bw), BF16),
        compiler_params=_params(("arbitrary", "arbitrary")),
        name="fox_paged",
    )(page_table, qn, knb, vb, cr, proj3, *([cache_k] * g_pages), *([cache_v] * g_pages), *([suffix3] * g_pages))


def _out_proj_kernel(oa_ref, ob_ref, wa_ref, wb_ref, x_ref, gt_ref, g_ref, o_ref):
    mix = _mm(oa_ref[...], wa_ref[...]) + _mm(ob_ref[...], wb_ref[...])
    normed = mix * lax.rsqrt(jnp.mean(mix * mix, axis=-1, keepdims=True) + EPS) * g_ref[...]
    o_ref[...] = x_ref[...] + gt_ref[0] * normed


def _out_proj(oa, ob, wa, wb, x2d, gt, g, *, tm, tiles_per_b):
    r, d = x2d.shape
    aw, bw = oa.shape[1], ob.shape[1]
    mr = gt.shape[1]
    return pl.pallas_call(
        _out_proj_kernel,
        grid=(r // tm,),
        in_specs=[pl.BlockSpec((tm, aw), lambda i: (i, 0)),
                  pl.BlockSpec((tm, bw), lambda i: (i, 0)),
                  pl.BlockSpec((aw, d), lambda i: (0, 0)),
                  pl.BlockSpec((bw, d), lambda i: (0, 0)),
                  pl.BlockSpec((tm, d), lambda i: (i, 0)),
                  pl.BlockSpec((1, mr, d), lambda i: (i // tiles_per_b, 0, 0)),
                  pl.BlockSpec((1, d), lambda i: (0, 0))],
        out_specs=pl.BlockSpec((tm, d), lambda i: (i, 0)),
        out_shape=jax.ShapeDtypeStruct((r, d), F32),
        compiler_params=_params(("arbitrary",)),
        name="out_proj",
    )(oa, ob, wa, wb, x2d, gt, g.reshape(1, d))


def _ffn_tail_kernel(ug_ref, uv_ref, hg_ref, hv_ref, wcg_ref, wcv_ref, bg_ref, bv_ref, wd_ref, x_ref, gt_ref, g_ref,
                     o_ref, acc_scr, *, tiles_per_b):
    i = pl.program_id(0)
    j = pl.program_id(1)
    first = (i % tiles_per_b) == 0

    def conv(u_ref, halo_ref, wc_ref, b_ref):
        x = u_ref[...]
        halo = jnp.where(first, 0.0, halo_ref[...])
        xe = jnp.concatenate([halo, x], axis=0)
        wc = wc_ref[...]
        return wc[2:3] * x + wc[1:2] * _shift_rows(xe, 1) + wc[0:1] * _shift_rows(xe, 2) + b_ref[...]

    gate = conv(ug_ref, hg_ref, wcg_ref, bg_ref)
    val = conv(uv_ref, hv_ref, wcv_ref, bv_ref)
    act = (gate * _sigmoid(gate) * val).astype(BF16)
    part = _mm(act, wd_ref[...])

    @pl.when(j == 0)
    def _():
        acc_scr[...] = part

    @pl.when(j > 0)
    def _():
        acc_scr[...] += part

    @pl.when(j == pl.num_programs(1) - 1)
    def _():
        y = acc_scr[...]
        normed = y * lax.rsqrt(jnp.mean(y * y, axis=-1, keepdims=True) + EPS) * g_ref[...]
        o_ref[...] = x_ref[...] + gt_ref[0] * normed


def _ffn_tail(up, wc, bc, wd, x2d, gt, g, *, tm, tiles_per_b, tf):
    r, d = x2d.shape
    fp = wd.shape[0]
    nf = fp // tf
    hb = tm // GROUP
    mr = gt.shape[1]
    halo = lambda off: pl.BlockSpec((GROUP, tf), lambda i, j, off=off: (jnp.maximum(i * hb - 1, 0), j + off))
    return pl.pallas_call(
        functools.partial(_ffn_tail_kernel, tiles_per_b=tiles_per_b),
        grid=(r // tm, nf),
        in_specs=[pl.BlockSpec((tm, tf), lambda i, j: (i, j)),
                  pl.BlockSpec((tm, tf), lambda i, j: (i, j + nf)),
                  halo(0), halo(nf),
                  pl.BlockSpec((3, tf), lambda i, j: (0, j)),
                  pl.BlockSpec((3, tf), lambda i, j: (0, j + nf)),
                  pl.BlockSpec((1, tf), lambda i, j: (0, j)),
                  pl.BlockSpec((1, tf), lambda i, j: (0, j + nf)),
                  pl.BlockSpec((tf, d), lambda i, j: (j, 0)),
                  pl.BlockSpec((tm, d), lambda i, j: (i, 0)),
                  pl.BlockSpec((1, mr, d), lambda i, j: (i // tiles_per_b, 0, 0)),
                  pl.BlockSpec((1, d), lambda i, j: (0, 0))],
        out_specs=pl.BlockSpec((tm, d), lambda i, j: (i, 0)),
        out_shape=jax.ShapeDtypeStruct((r, d), F32),
        scratch_shapes=[pltpu.VMEM((tm, d), F32)],
        compiler_params=_params(("arbitrary", "arbitrary")),
        name="ffn_tail",
    )(up, up, up, up, wc, wc, bc, bc, wd, x2d, gt, g.reshape(1, d))


def _ffn_fused_kernel(x_ref, sc_ref, sh_ref, gpre_ref, wg_ref, wv_ref, wcg_ref, wcv_ref, bg_ref, bv_ref, wd_ref,
                      gt_ref, gpost_ref, o_ref, lg_ref, lv_ref, h_scr, cg_scr, cv_scr, xg_scr, xv_scr, act_scr,
                      *, tiles_per_b):
    i = pl.program_id(0)
    j = pl.program_id(1)
    tm = x_ref.shape[0]
    first = (i % tiles_per_b) == 0

    @pl.when(j == 0)
    def _():
        x = x_ref[...]
        y = x * lax.rsqrt(jnp.mean(x * x, axis=-1, keepdims=True) + EPS) * gpre_ref[...]
        h_scr[...] = (y * (1.0 + sc_ref[0]) + sh_ref[0]).astype(BF16)
        o_ref[...] = jnp.zeros_like(o_ref)

    tf = wd_ref.shape[0]
    ts = tm // FFN_SPLIT

    def project(s, w_ref, xe_scr):
        xe_scr[GROUP + s * ts:GROUP + (s + 1) * ts, :] = _mm(h_scr[s * ts:(s + 1) * ts, :], w_ref[0])

    xg_scr[0:GROUP, :] = jnp.where(first, 0.0, cg_scr[j])
    xv_scr[0:GROUP, :] = jnp.where(first, 0.0, cv_scr[j])
    for s in range(FFN_SPLIT):
        project(s, wg_ref, xg_scr)
        project(s, wv_ref, xv_scr)

    def conv(xe_scr, wc_ref, b_ref, r0, c0):
        xe = xe_scr[r0:r0 + FFN_ROWS + GROUP, c0:c0 + FFN_LANES]
        wc = wc_ref[:, c0:c0 + FFN_LANES]
        return (wc[2:3] * xe[GROUP:] + wc[1:2] * _shift_rows(xe, 1) + wc[0:1] * _shift_rows(xe, 2)
                + b_ref[:, c0:c0 + FFN_LANES])

    for s in range(FFN_SPLIT):
        for r0 in range(s * ts, (s + 1) * ts, FFN_ROWS):
            for c0 in range(0, tf, FFN_LANES):
                gate = conv(xg_scr, wcg_ref, bg_ref, r0, c0)
                val = conv(xv_scr, wcv_ref, bv_ref, r0, c0)
                act_scr[r0:r0 + FFN_ROWS, c0:c0 + FFN_LANES] = (gate * _sigmoid(gate) * val).astype(BF16)
        o_ref[s * ts:(s + 1) * ts, :] += _mm(act_scr[s * ts:(s + 1) * ts, :], wd_ref[...])

    for xe_scr, carry_scr, last_ref in ((xg_scr, cg_scr, lg_ref), (xv_scr, cv_scr, lv_ref)):
        last = xe_scr[tm:tm + GROUP, :]
        carry_scr[j] = last
        last_ref[0] = last

    @pl.when(j == pl.num_programs(1) - 1)
    def _():
        y = o_ref[...]
        normed = y * lax.rsqrt(jnp.mean(y * y, axis=-1, keepdims=True) + EPS) * gpost_ref[...]
        o_ref[...] = x_ref[...] + gt_ref[0] * normed


def _ffn_fused(x2d, sc, sh, gt, g_pre, g_post, w_up_t, wc, bc, wd, *, nb, tm):
    r, d = x2d.shape
    fp = wd.shape[0]
    tf = w_up_t.shape[2]
    nf = fp // tf
    tiles_per_b = r // nb // tm
    mod = pl.BlockSpec((1, 1, d), lambda i, j: (i // tiles_per_b, 0, 0))
    vec = pl.BlockSpec((1, d), lambda i, j: (0, 0))
    col = lambda rows, off: pl.BlockSpec((rows, tf), lambda i, j, off=off: (0, j + off))
    wtile = lambda off: pl.BlockSpec((1, d, tf), lambda i, j, off=off: (j + off, 0, 0))
    last = pl.BlockSpec((1, GROUP, tf), lambda i, j: (i, 0, j))
    rows_once = lambda: pl.BlockSpec((tm, d), lambda i, j: (i, 0), pipeline_mode=pl.Buffered(1))
    return pl.pallas_call(
        functools.partial(_ffn_fused_kernel, tiles_per_b=tiles_per_b),
        grid=(r // tm, nf),
        in_specs=[rows_once(), mod, mod, vec,
                  wtile(0), wtile(nf), col(3, 0), col(3, nf), col(1, 0), col(1, nf),
                  pl.BlockSpec((tf, d), lambda i, j: (j, 0)), mod, vec],
        out_specs=[rows_once(), last, last],
        out_shape=[jax.ShapeDtypeStruct((r, d), F32), jax.ShapeDtypeStruct((r // tm, GROUP, fp), F32),
                   jax.ShapeDtypeStruct((r // tm, GROUP, fp), F32)],
        scratch_shapes=[pltpu.VMEM((tm, d), BF16),
                        pltpu.VMEM((nf, GROUP, tf), F32), pltpu.VMEM((nf, GROUP, tf), F32),
                        pltpu.VMEM((tm + GROUP, tf), F32), pltpu.VMEM((tm + GROUP, tf), F32),
                        pltpu.VMEM((tm, tf), BF16)],
        compiler_params=_params(("arbitrary", "arbitrary")),
        name="ffn_fused",
    )(x2d, sc, sh, g_pre.reshape(1, d), w_up_t, w_up_t, wc, wc, bc, bc, wd, gt, g_post.reshape(1, d))


def _pad_cols(a, n):
    return jnp.pad(a, [(0, 0)] * (a.ndim - 1) + [(0, n - a.shape[-1])])


def _splice_rows(a, rows, start):
    k = rows.shape[1]
    padded = jnp.pad(rows, ((0, 0), (start, a.shape[1] - start - k), (0, 0)))
    idx = lax.broadcasted_iota(jnp.int32, (1, a.shape[1], 1), 1)
    return jnp.where((idx >= start) & (idx < start + k), padded, a)


def _split_hi_lo(w):
    hi = w.astype(BF16)
    return hi, (w - hi.astype(F32)).astype(BF16)


def _layer(x3, mods, st_conv, st_gdn, st_ffn, fox, lw, *, heads, front_pad, tm, q_scale):
    (g_pre_mix, g_post_mix, g_pre_ffn, g_post_ffn, w_big, ws, w_conv_qkv, a_row, a_col, gn, qg, kg, brow, bcol,
     w_out_a, w_out_b, w_up, w_conv_ffn, b_conv_ffn, w_down, d_ff, tf) = lw
    sh_m, sc_m, gt_m, sh_f, sc_f, gt_f = mods
    nb, t, d = x3.shape
    aw = heads * 128
    x2d = x3.reshape(nb * t, d)
    tiles_per_b = max(t // tm, 1) if mods[0].shape[1] == 1 else 1

    proj, small = _norm_proj(x2d, sc_m, sh_m, g_pre_mix, w_big, ws, tm=tm, tiles_per_b=tiles_per_b)
    proj3 = proj.reshape(nb, t, 8 * aw)
    if st_conv is not None:
        k = st_conv.shape[1]
        proj3 = _splice_rows(proj3, st_conv, GROUP - (t - front_pad) - k)
    small3 = small.reshape(nb, t, 128)
    rows_t = 3 * GROUP
    smallt3 = jnp.swapaxes(small3[:, :, :rows_t], 1, 2)

    o_a, gdn_new = _gdn(proj3, small3, smallt3, w_conv_qkv, a_row, a_col, gn, st_gdn,
                        heads=heads, front_pad=front_pad)
    qn, kn, knb, vb, logf, cr = _fox_prep(proj3, small3, smallt3, qg, kg, brow, bcol,
                                               heads=heads, front_pad=front_pad, q_scale=q_scale)
    o_b = fox(qn, knb, vb, cr, proj3)

    x1 = _out_proj(o_a.reshape(nb * t, aw), o_b.reshape(nb * t, aw), w_out_a, w_out_b, x2d, gt_m, g_post_mix,
                   tm=min(tm, 512), tiles_per_b=max(t // min(tm, 512), 1) if mods[0].shape[1] == 1 else 1)
    fp = w_down.shape[0]
    unpad = lambda g, v: jnp.concatenate([g[..., :d_ff], v[..., :d_ff]], axis=-1)
    if st_ffn is None:
        w_up_t = jnp.swapaxes(w_up.reshape(d, 2 * fp // tf, tf), 0, 1)
        y, last_g, last_v = _ffn_fused(x1, sc_f, sh_f, gt_f, g_pre_ffn, g_post_ffn, w_up_t, w_conv_ffn, b_conv_ffn,
                                       w_down, nb=nb, tm=min(tm, 512))
        per_b = last_g.shape[0] // nb
        up_last = unpad(last_g[per_b - 1::per_b], last_v[per_b - 1::per_b])
    else:
        up = _norm_proj(x1, sc_f, sh_f, g_pre_ffn, w_up, tm=tm, tiles_per_b=tiles_per_b)
        up3 = up.reshape(nb, t, 2 * fp)
        k = st_ffn.shape[1]
        up3 = _splice_rows(up3, st_ffn, GROUP - (t - front_pad) - k)
        y = _ffn_tail(up3.reshape(nb * t, 2 * fp), w_conv_ffn, b_conv_ffn, w_down, x1, gt_f, g_post_ffn,
                      tm=min(tm, 512), tiles_per_b=1, tf=tf)
        up_last = unpad(up3[:, t - GROUP:, :fp], up3[:, t - GROUP:, fp:])
    return y.reshape(nb, t, d), proj3, kn, logf, gdn_new, up_last


def kernel(x_prompt, x_sample, cache_k, cache_v, cache_logf, state_gdn, state_conv_qkv, state_ffn_conv, page_table, c_prompt, c_sample, w_ada, b_ada, g_pre_mix, g_post_mix, g_pre_ffn, g_post_ffn, w_in, w_conv_qkv, a_log, dt_bias, g_gdn_norm, q_norm, k_norm, b_forget, w_out, w_up, w_conv_ffn, b_conv_ffn, w_down):
    depth = w_ada.shape[0]
    assert depth == 1, "single-layer step"
    b, t, d = x_prompt.shape
    bd, n_new, _ = x_sample.shape
    heads = state_gdn.shape[2]
    dh = state_gdn.shape[3]
    assert dh == 128 and cache_k.shape[3] == heads and n_new <= GROUP // 2
    aw = heads * dh
    page = cache_k.shape[2]
    n_pool = cache_k.shape[1]
    d_ff = w_down.shape[1]
    conv_a = w_conv_qkv.shape[1]
    ffn_conv = w_conv_ffn.shape[1]
    assert conv_a == 4 and ffn_conv == 3
    layer = 0
    (cache_logf, state_gdn, state_conv_qkv, state_ffn_conv, w_ada, b_ada, g_pre_mix, g_post_mix,
     g_pre_ffn, g_post_ffn, w_in, w_conv_qkv, a_log, dt_bias, g_gdn_norm, q_norm, k_norm, b_forget, w_out, w_up,
     w_conv_ffn, b_conv_ffn, w_down) = [
        (a.reshape(a.shape[1:]),) for a in
        (cache_logf, state_gdn, state_conv_qkv, state_ffn_conv, w_ada, b_ada, g_pre_mix, g_post_mix,
         g_pre_ffn, g_post_ffn, w_in, w_conv_qkv, a_log, dt_bias, g_gdn_norm, q_norm, k_norm, b_forget, w_out, w_up,
         w_conv_ffn, b_conv_ffn, w_down)]

    wi = w_in[layer]
    o1 = 4 * aw
    o2 = o1 + 2 * heads
    o3 = o2 + 4 * aw
    w_big = jnp.concatenate([wi[:, :o1], wi[:, o2:o3]], axis=1).astype(BF16)
    w_small = _pad_cols(jnp.concatenate([wi[:, o1:o2], wi[:, o3:]], axis=1), 128)
    ws = _split_hi_lo(w_small)
    zeros_h = jnp.zeros((heads,), F32)
    a_row = _pad_cols(jnp.stack([jnp.concatenate([zeros_h, a_log[layer]]),
                                 jnp.concatenate([zeros_h, dt_bias[layer]])]), 128)
    rows_t = 3 * GROUP
    a_col = jnp.pad(a_row[:, :rows_t].T, ((0, 0), (0, 0)))
    brow = _pad_cols(jnp.concatenate([zeros_h, zeros_h, b_forget[layer]])[None, :], 128)
    bcol = brow[:, :rows_t].T
    assert heads == GROUP and page == 128
    gn = g_gdn_norm[layer].reshape(1, dh)
    qg = q_norm[layer].reshape(1, dh)
    kg = k_norm[layer].reshape(1, dh)
    wo = w_out[layer].astype(BF16)
    w_out_a, w_out_b = wo[:aw], wo[aw:]
    fp = -(-d_ff // 512) * 512
    tf = _pick(fp, 1536, 128)
    wu = w_up[layer]
    w_up_p =jnp.concatenate([_pad_cols(wu[:, :d_ff], fp), _pad_cols(wu[:, d_ff:], fp)], axis=1).astype(BF16)
    wcf = w_conv_ffn[layer]
    w_conv_ffn_p =jnp.concatenate([_pad_cols(wcf[:, :d_ff], fp), _pad_cols(wcf[:, d_ff:], fp)], axis=1)
    bcf = b_conv_ffn[layer][None, :]
    b_conv_ffn_p = jnp.concatenate([_pad_cols(bcf[:, :d_ff], fp), _pad_cols(bcf[:, d_ff:], fp)], axis=1)
    w_down_p = jnp.pad(w_down[layer], ((0, fp - d_ff), (0, 0))).astype(BF16)
    lw = (g_pre_mix[layer], g_post_mix[layer], g_pre_ffn[layer], g_post_ffn[layer], w_big, ws, w_conv_qkv[layer],
          a_row, a_col, gn, qg, kg, brow, bcol, w_out_a, w_out_b, w_up_p, w_conv_ffn_p, b_conv_ffn_p, w_down_p,
          d_ff, tf)

    n_c = b + bd
    c_all = jnp.pad(jnp.concatenate([c_prompt, c_sample], axis=0), ((0, -n_c % GROUP), (0, 0)))
    mod = _ada(c_all, w_ada[layer], b_ada[layer])
    mods_p = [m[:b].reshape(b, 1, d) for m in jnp.split(mod, 6, axis=-1)]
    mods_s = [jnp.repeat(m[b:n_c], GROUP, axis=0).reshape(1, bd * GROUP, d) for m in jnp.split(mod, 6, axis=-1)]

    tm_p = _pick(t, 1024, 128)
    fox_p = lambda qn, knb, vb, cr, proj3: _flash(
        qn, knb, vb, cr[:, 2 * heads:3 * heads].reshape(b, heads, 1, t), proj3, heads=heads)
    zeros_s0 = jnp.zeros((b, heads, dh, dh), F32)
    y_p, proj_p, kn_p, logf_p, gdn_p, up_p = _layer(
        x_prompt, mods_p, None, zeros_s0, None, fox_p, lw, heads=heads, front_pad=0, tm=tm_p,
        q_scale=dh ** -0.5 * LOG2E)

    front = GROUP - n_new
    x_s = jnp.pad(x_sample, ((0, 0), (front, 0), (0, 0)))
    suffix = _suffix(cache_logf[layer].reshape(n_pool, page * heads), heads=heads)
    suffix3 = suffix.reshape(n_pool, 1, 2 * page * heads)
    fox_s = lambda qn, knb, vb, cr, proj3: _paged(
        page_table, qn, knb, vb, cr, proj3, cache_k, cache_v, suffix3, heads=heads, n_new=n_new)
    st_ffn = state_ffn_conv[layer]
    st_ffn_p =jnp.concatenate([_pad_cols(st_ffn[:, :, :d_ff], fp), _pad_cols(st_ffn[:, :, d_ff:], fp)], axis=-1)
    st_conv = _pad_cols(state_conv_qkv[layer], 8 * aw)
    y_s, proj_s, kn_s, logf_s, gdn_s, up_s = _layer(
        x_s, mods_s, st_conv, state_gdn[layer], st_ffn_p, fox_s, lw, heads=heads, front_pad=front, tm=bd * GROUP,
        q_scale=dh ** -0.5)

    n_pg = t // page
    k_prompt = kn_p.reshape(1, b, n_pg, page, heads, dh)
    v_prompt = proj_p[:, :, 6 * aw:7 * aw].reshape(1, b, n_pg, page, heads, dh)
    logf_prompt = logf_p[:, :, 2 * heads:3 * heads].reshape(1, b, n_pg, page, heads)
    conv_qkv_prompt = proj_p[:, t - (conv_a - 1):, :3 * aw][None]
    ffn_conv_prompt = up_p[:, GROUP - (ffn_conv - 1):, :][None]
    k_sample = kn_s[:, front:].reshape(1, bd, n_new, heads, dh)
    v_sample = proj_s[:, front:, 6 * aw:7 * aw].reshape(1, bd, n_new, heads, dh)
    logf_sample = logf_s[:, front:, 2 * heads:3 * heads][None]
    conv_qkv_sample = proj_s[:, GROUP - (conv_a - 1):, :3 * aw][None]
    ffn_conv_sample = up_s[:, GROUP - (ffn_conv - 1):, :][None]
    return (y_p, y_s[:, front:], k_prompt, v_prompt, logf_prompt, gdn_p[None], conv_qkv_prompt, ffn_conv_prompt,
            k_sample, v_sample, logf_sample, gdn_s.astype(state_gdn[layer].dtype)[None], conv_qkv_sample, ffn_conv_sample)
```

```python
import functools

import jax
import jax.numpy as jnp
from jax import lax
from jax.experimental import pallas as pl
from jax.experimental.pallas import tpu as pltpu

EPS = 1e-6
F32 = jnp.float32
BF16 = jnp.bfloat16
HI = lax.Precision.HIGHEST
NEG = -1e30
LOG2E = 1.4426950408889634
GDN_CHUNK = 64
GROUP = 8
FFN_ROWS, FFN_LANES = 64, 256
FFN_SPLIT = 4
V7X_VMEM_LIMIT = 56 * 1024 * 1024

NN = (((1,), (0,)), ((), ()))
NT = (((1,), (1,)), ((), ()))
TN = (((0,), (0,)), ((), ()))


def _mm(a, b, dims=NN, precision=None):
    return lax.dot_general(a, b, dims, precision=precision, preferred_element_type=F32)


BNN = (((2,), (1,)), ((0,), (0,)))
BNT = (((2,), (2,)), ((0,), (0,)))
BTN = (((1,), (1,)), ((0,), (0,)))


def _bmm(a, b, dims=BNN):
    return lax.dot_general(a, b, dims, preferred_element_type=F32)


def _bmm1(a, b, dims):
    return _bmm(a.astype(BF16), b.astype(BF16), dims)


def _bmm3(a, b, dims):
    free = 2 if dims == BTN else 1
    m = a.shape[free]
    ah = a.astype(BF16).astype(F32)
    bh = b.astype(BF16)
    bl = (b - bh.astype(F32)).astype(BF16)
    stack = jnp.concatenate([ah, a - ah], axis=free).astype(BF16)
    r = _bmm(stack, bh, dims)
    r2 = _bmm(lax.slice_in_dim(stack, 0, m, axis=free), bl, dims)
    return lax.slice_in_dim(r, 0, m, axis=1) + lax.slice_in_dim(r, m, 2 * m, axis=1) + r2


_P_AQ = _P_INV = _P_MRG = _P_UW = _P_WS = _P_O = _P_S = _bmm1


def _pick(n, target, mult):
    best = None
    for d in range(mult, min(n, target) + 1, mult):
        if n % d == 0:
            best = d
    return best if best is not None else n


def _params(sem):
    return pltpu.CompilerParams(dimension_semantics=sem, vmem_limit_bytes=V7X_VMEM_LIMIT)


def _sigmoid(x):
    return 1.0 / (1.0 + jnp.exp(-x))


def _softplus(x):
    return jnp.maximum(x, 0.0) + jnp.log(1.0 + jnp.exp(-jnp.abs(x)))


def _tri(n, kind):
    r = lax.broadcasted_iota(jnp.int32, (n, n), 0)
    c = lax.broadcasted_iota(jnp.int32, (n, n), 1)
    if kind == "lower_incl":
        return (c <= r).astype(F32)
    if kind == "upper_incl":
        return (r <= c).astype(F32)
    raise ValueError(kind)


def _ada_kernel(c_ref, w_ref, b_ref, o_ref):
    c = c_ref[...]
    o_ref[...] = _mm(c * _sigmoid(c), w_ref[...], precision=HI) + b_ref[...]


def _ada(c_all, w_ada, b_ada):
    m, d = c_all.shape
    n = w_ada.shape[1]
    tn = _pick(n, 1024, 128)
    return pl.pallas_call(
        _ada_kernel,
        grid=(n // tn,),
        in_specs=[pl.BlockSpec((m, d), lambda j: (0, 0)),
                  pl.BlockSpec((d, tn), lambda j: (0, j)),
                  pl.BlockSpec((1, tn), lambda j: (0, j))],
        out_specs=pl.BlockSpec((m, tn), lambda j: (0, j)),
        out_shape=jax.ShapeDtypeStruct((m, n), F32),
        compiler_params=_params(("arbitrary",)),
        name="ada",
    )(c_all, w_ada, b_ada.reshape(1, n))


def _norm_proj_kernel(*refs, with_small):
    x_ref, sc_ref, sh_ref, g_ref, w_ref = refs[:5]
    if with_small:
        wsh_ref, wsl_ref, o_ref, os_ref, h_scr = refs[5:]
    else:
        o_ref, h_scr = refs[5:]

    @pl.when(pl.program_id(1) == 0)
    def _():
        x = x_ref[...]
        y = x * lax.rsqrt(jnp.mean(x * x, axis=-1, keepdims=True) + EPS) * g_ref[...]
        h = y * (1.0 + sc_ref[0]) + sh_ref[0]
        hb = h.astype(BF16)
        h_scr[...] = hb
        if with_small:
            hl = (h - hb.astype(F32)).astype(BF16)
            os_ref[...] = _mm(hb, wsh_ref[...]) + _mm(hb, wsl_ref[...]) + _mm(hl, wsh_ref[...])

    o_ref[...] = _mm(h_scr[...], w_ref[0])


def _norm_proj(x2d, sc, sh, g, w, ws=None, *, tm, tiles_per_b):
    r, d = x2d.shape
    n = w.shape[1]
    tn = _pick(n, 1024, 128)
    mr = sc.shape[1]
    mod_spec = pl.BlockSpec((1, mr, d), lambda i, j: (i // tiles_per_b, 0, 0))
    w_t = jnp.swapaxes(w.reshape(d, n // tn, tn), 0, 1)
    in_specs = [pl.BlockSpec((tm, d), lambda i, j: (i, 0)), mod_spec, mod_spec,
                pl.BlockSpec((1, d), lambda i, j: (0, 0)),
                pl.BlockSpec((1, d, tn), lambda i, j: (j, 0, 0))]
    args = [x2d, sc, sh, g.reshape(1, d), w_t]
    out_specs = pl.BlockSpec((tm, tn), lambda i, j: (i, j))
    out_shape = jax.ShapeDtypeStruct((r, n), F32)
    if ws is not None:
        ws_hi, ws_lo = ws
        ns = ws_hi.shape[1]
        in_specs += [pl.BlockSpec((d, ns), lambda i, j: (0, 0))] * 2
        args += [ws_hi, ws_lo]
        out_specs = [out_specs, pl.BlockSpec((tm, ns), lambda i, j: (i, 0))]
        out_shape = [out_shape, jax.ShapeDtypeStruct((r, ns), F32)]
    return pl.pallas_call(
        functools.partial(_norm_proj_kernel, with_small=ws is not None),
        grid=(r // tm, n // tn),
        in_specs=in_specs, out_specs=out_specs, out_shape=out_shape,
        scratch_shapes=[pltpu.VMEM((tm, d), BF16)],
        compiler_params=_params(("arbitrary", "arbitrary")),
        name="norm_proj",
    )(*args)


def _shift_rows(xe, s):
    return pltpu.roll(xe, s, 0)[GROUP:]


def _gdn_kernel(qkv_ref, z_ref, sm_ref, smt_ref, wc_ref, arow_ref, acol_ref, gn_ref, s0_ref,
                o_ref, sout_ref, s_scr, prev_scr, *, heads, chunk, n_chunks, front_pad):
    t = pl.program_id(1)
    tc = chunk * n_chunks
    aw = heads * 128

    @pl.when(t == 0)
    def _():
        s_scr[...] = s0_ref[0]
        prev_scr[...] = jnp.zeros_like(prev_scr)

    row = lax.broadcasted_iota(jnp.int32, (tc, 1), 0)
    col = lax.broadcasted_iota(jnp.int32, (1, tc), 1)
    valid_c = row >= front_pad
    valid_r = col >= front_pad

    sm = sm_ref[0]
    g_tile = jnp.where(valid_c, -jnp.exp(arow_ref[0:1, :]) * _softplus(sm + arow_ref[1:2, :]), 0.0)
    beta_tile = jnp.where(valid_c, _sigmoid(sm), 0.0)
    smt = smt_ref[0]
    gt_all = jnp.where(valid_r, -jnp.exp(acol_ref[:, 0:1]) * _softplus(smt + acol_ref[:, 1:2]), 0.0)

    lo_incl = _tri(chunk, "lower_incl")
    up_incl = _tri(chunk, "upper_incl")
    ri = lax.broadcasted_iota(jnp.int32, (chunk, chunk), 0)
    ci = lax.broadcasted_iota(jnp.int32, (chunk, chunk), 1)
    incl = ci <= ri
    strict = ci < ri
    eye = (ci == ri).astype(F32)

    gcol_tiles = [_mm(lo_incl, g_tile[c * chunk:(c + 1) * chunk], precision=HI) for c in range(n_chunks)]
    grow_tiles = [_mm(gt_all[:, c * chunk:(c + 1) * chunk], up_incl, precision=HI) for c in range(n_chunks)]

    wc = wc_ref[...]
    per_head = []
    for h in range(heads):
        parts = []
        for p in range(3):
            lo = p * aw + h * 128
            x = qkv_ref[0, :, lo:lo + 128]
            xe = jnp.concatenate([prev_scr[:, lo:lo + 128], x], axis=0)
            conv = (wc[3:4, lo:lo + 128] * x + wc[2:3, lo:lo + 128] * _shift_rows(xe, 1)
                    + wc[1:2, lo:lo + 128] * _shift_rows(xe, 2) + wc[0:1, lo:lo + 128] * _shift_rows(xe, 3))
            parts.append(conv * _sigmoid(conv))
        q_all, k_all, v_all = parts
        q_all = q_all * lax.rsqrt(jnp.sum(q_all * q_all, axis=-1, keepdims=True) + EPS) * (128.0 ** -0.5)
        k_all = k_all * lax.rsqrt(jnp.sum(k_all * k_all, axis=-1, keepdims=True) + EPS)
        per_head.append((q_all, jnp.where(valid_c, k_all, 0.0), v_all))

    units = [(c, h) for c in range(n_chunks) for h in range(heads)]
    rows = lambda c: slice(c * chunk, (c + 1) * chunk)
    q = jnp.stack([per_head[h][0][rows(c)] for c, h in units])
    k = jnp.stack([per_head[h][1][rows(c)] for c, h in units])
    v = jnp.stack([per_head[h][2][rows(c)] for c, h in units])
    gcol = jnp.stack([gcol_tiles[c][:, heads + h:heads + h + 1] for c, h in units])
    grow = jnp.stack([grow_tiles[c][heads + h:heads + h + 1, :] for c, h in units])
    bcol = jnp.stack([beta_tile[rows(c), h:h + 1] for c, h in units])

    decay = jnp.where(incl, jnp.exp(jnp.where(incl, gcol - grow, 0.0)), 0.0)
    kb = k * bcol
    vb = v * bcol
    aq = _P_AQ(jnp.concatenate([kb, q], axis=1), k, BNT)
    lower = jnp.where(strict, aq[:, :chunk] * decay, 0.0)
    qk = jnp.where(incl, aq[:, chunk:] * decay, 0.0)
    base = min(GROUP, chunk)
    same_blk = lambda s: (ri >> (s.bit_length() - 1)) == (ci >> (s.bit_length() - 1))
    neg_bd = jnp.where(same_blk(base), -lower, 0.0)
    nm = _P_INV(neg_bd, neg_bd, BNN)
    tinv = eye + neg_bd
    n_base = base.bit_length() - 2
    for lvl in range(n_base):
        if lvl < n_base - 1:
            r = _P_INV(jnp.concatenate([nm, tinv], axis=1), nm, BNN)
            tinv = tinv + r[:, chunk:]
            nm = r[:, :chunk]
        else:
            tinv = tinv + _P_INV(tinv, nm, BNN)
    s = base
    while s < chunk:
        off = jnp.where(same_blk(2 * s) & jnp.logical_not(same_blk(s)), lower, 0.0)
        tinv = tinv - _P_MRG(tinv, _P_MRG(off, tinv, BNN), BNN)
        s *= 2
    eg = jnp.exp(gcol)
    uw = _P_UW(tinv, jnp.concatenate([vb, kb * eg], axis=2), BNN)
    wq = jnp.concatenate([uw[:, :, 128:], q * eg], axis=1)
    g_last = gcol[:, chunk - 1:chunk, :]
    kd = k * jnp.exp(g_last - gcol)
    e_last = jnp.exp(g_last)

    s_all = s_scr[...]
    for c in range(n_chunks):
        us = slice(c * heads, (c + 1) * heads)
        ws = _P_WS(wq[us], s_all, BNN)
        v_new = uw[us, :, :128] - ws[:, :chunk]
        o = ws[:, chunk:] + _P_O(qk[us], v_new, BNN)
        s_all = s_all * e_last[us] + _P_S(kd[us], v_new, BTN)
        on = o * lax.rsqrt(jnp.mean(o * o, axis=-1, keepdims=True) + EPS) * gn_ref[...]
        for h in range(heads):
            z = z_ref[0, rows(c), h * 128:(h + 1) * 128]
            o_ref[0, rows(c), h * 128:(h + 1) * 128] = (on[h] * (z * _sigmoid(z))).astype(o_ref.dtype)
    s_scr[...] = s_all

    prev_scr[...] = qkv_ref[0, tc - GROUP:, :]

    @pl.when(t == pl.num_programs(1) - 1)
    def _():
        sout_ref[0] = s_scr[...]


def _gdn(proj3, small3, smallt3, w_conv, a_row, a_col, gn, s0, *, heads, front_pad):
    b, t, _ = proj3.shape
    aw = heads * 128
    chunk = min(GDN_CHUNK, t)
    tc = t if t <= 128 else _pick(t, 256, 128)
    n_chunks = tc // chunk
    rows_t = smallt3.shape[1]
    kern = functools.partial(_gdn_kernel, heads=heads, chunk=chunk, n_chunks=n_chunks, front_pad=front_pad)
    return pl.pallas_call(
        kern,
        grid=(b, t // tc),
        in_specs=[pl.BlockSpec((1, tc, 3 * aw), lambda i, j: (i, j, 0)),
                  pl.BlockSpec((1, tc, aw), lambda i, j: (i, j, 3)),
                  pl.BlockSpec((1, tc, 128), lambda i, j: (i, j, 0)),
                  pl.BlockSpec((1, rows_t, tc), lambda i, j: (i, 0, j)),
                  pl.BlockSpec((4, 3 * aw), lambda i, j: (0, 0)),
                  pl.BlockSpec((2, 128), lambda i, j: (0, 0)),
                  pl.BlockSpec((rows_t, 2), lambda i, j: (0, 0)),
                  pl.BlockSpec((1, 128), lambda i, j: (0, 0)),
                  pl.BlockSpec((1, heads, 128, 128), lambda i, j: (i, 0, 0, 0))],
        out_specs=[pl.BlockSpec((1, tc, aw), lambda i, j: (i, j, 0)),
                   pl.BlockSpec((1, heads, 128, 128), lambda i, j: (i, 0, 0, 0))],
        out_shape=[jax.ShapeDtypeStruct((b, t, aw), BF16),
                   jax.ShapeDtypeStruct((b, heads, 128, 128), F32)],
        scratch_shapes=[pltpu.VMEM((heads, 128, 128), F32), pltpu.VMEM((GROUP, 3 * aw), F32)],
        compiler_params=_params(("arbitrary", "arbitrary")),
        name="gdn",
    )(proj3, proj3, small3, smallt3, w_conv, a_row, a_col, gn, s0)


def _fox_prep_kernel(q_ref, k_ref, v_ref, sm_ref, smt_ref, qg_ref, kg_ref, brow_ref, bcol_ref,
                     qn_ref, kn_ref, knb_ref, vb_ref, logf_ref, cr_ref, carry_r,
                     *, heads, front_pad, q_scale):
    t = pl.program_id(1)
    tr = q_ref.shape[1]

    @pl.when(t == 0)
    def _():
        carry_r[...] = jnp.zeros_like(carry_r)

    for h in range(heads):
        cs = slice(h * 128, (h + 1) * 128)
        q = q_ref[0, :, cs]
        qn = q * lax.rsqrt(jnp.mean(q * q, axis=-1, keepdims=True) + EPS) * qg_ref[...]
        qn_ref[0, :, cs] = (qn * q_scale).astype(BF16)
        k = k_ref[0, :, cs]
        kn = k * lax.rsqrt(jnp.mean(k * k, axis=-1, keepdims=True) + EPS) * kg_ref[...]
        kn_ref[0, :, cs] = kn
        knb_ref[0, :, cs] = kn.astype(BF16)
    vb_ref[0] = v_ref[0].astype(BF16)

    col = lax.broadcasted_iota(jnp.int32, (1, tr), 1)
    logf_ref[0] = -_softplus(-(sm_ref[0] + brow_ref[...]))
    logft = jnp.where(col >= front_pad, -_softplus(-(smt_ref[0] + bcol_ref[...])), 0.0)
    cumt = _mm(logft, _tri(tr, "upper_incl"), precision=HI) + carry_r[:, 0:1]
    cr_ref[0] = cumt
    carry_r[...] = jnp.broadcast_to(cumt[:, tr - 1:tr], carry_r.shape)


def _fox_prep(proj3, small3, smallt3, qg, kg, brow, bcol, *, heads, front_pad, q_scale):
    b, t, _ = proj3.shape
    bw = heads * 128
    tr = _pick(t, 256, 128) if t >= 128 else t
    rows_t = smallt3.shape[1]
    wide = lambda blk: pl.BlockSpec((1, tr, bw), lambda i, j, blk=blk: (i, j, blk))
    out_w = pl.BlockSpec((1, tr, bw), lambda i, j: (i, j, 0))
    out_s = pl.BlockSpec((1, tr, 128), lambda i, j: (i, j, 0))
    return pl.pallas_call(
        functools.partial(_fox_prep_kernel, heads=heads, front_pad=front_pad, q_scale=q_scale),
        grid=(b, t // tr),
        in_specs=[wide(4), wide(5), wide(6),
                  pl.BlockSpec((1, tr, 128), lambda i, j: (i, j, 0)),
                  pl.BlockSpec((1, rows_t, tr), lambda i, j: (i, 0, j)),
                  pl.BlockSpec((1, 128), lambda i, j: (0, 0)),
                  pl.BlockSpec((1, 128), lambda i, j: (0, 0)),
                  pl.BlockSpec((1, 128), lambda i, j: (0, 0)),
                  pl.BlockSpec((rows_t, 1), lambda i, j: (0, 0))],
        out_specs=[out_w, out_w, out_w, out_w, out_s,
                   pl.BlockSpec((1, rows_t, tr), lambda i, j: (i, 0, j))],
        out_shape=[jax.ShapeDtypeStruct((b, t, bw), BF16), jax.ShapeDtypeStruct((b, t, bw), F32),
                   jax.ShapeDtypeStruct((b, t, bw), BF16), jax.ShapeDtypeStruct((b, t, bw), BF16),
                   jax.ShapeDtypeStruct((b, t, 128), F32),
                   jax.ShapeDtypeStruct((b, rows_t, t), F32)],
        scratch_shapes=[pltpu.VMEM((rows_t, 128), F32)],
        compiler_params=_params(("arbitrary", "arbitrary")),
        name="fox_prep",
    )(proj3, proj3, proj3, small3, smallt3, qg, kg, brow, bcol)


def _flash_kernel(q_ref, k_ref, v_ref, cr_ref, gate_ref, o_ref, *, tq, tk, n_sub):
    qi = pl.program_id(2)
    ts = tq // n_sub
    qs = [q_ref[0, a * ts:(a + 1) * ts, :] for a in range(n_sub)]
    q0 = qi * tq
    jd = q0 // tk

    def step(j, carry, masked):
        start = pl.multiple_of(j * tk, tk)
        ks = k_ref[0, :, pl.ds(start, tk)]
        vs = v_ref[0, pl.ds(start, tk), :]
        ck = cr_ref[0, 0, :, pl.ds(start, tk)] * LOG2E
        ss = [_mm(q, ks) - ck for q in qs]
        if masked:
            ci = lax.broadcasted_iota(jnp.int32, (ts, tk), 1) + start
            ss = [jnp.where(ci <= lax.broadcasted_iota(jnp.int32, (ts, tk), 0) + (q0 + a * ts), s, NEG)
                  for a, s in enumerate(ss)]
        m_new = [jnp.maximum(c[0], jnp.max(s, axis=-1, keepdims=True)) for c, s in zip(carry, ss)]
        ps = [jnp.exp2(s - m) for s, m in zip(ss, m_new)]
        pv = [_mm(p.astype(BF16), vs) for p in ps]
        out = []
        for (m, l, acc), mn, p, o in zip(carry, m_new, ps, pv):
            corr = jnp.exp2(m - mn)
            out.append((mn, l * corr + jnp.sum(p, axis=-1, keepdims=True), acc * corr + o))
        return tuple(out)

    init = tuple((jnp.full((ts, 1), NEG, F32), jnp.zeros((ts, 1), F32), jnp.zeros((ts, 128), F32))
                 for _ in range(n_sub))
    carry = lax.fori_loop(0, jd, lambda j, c: step(j, c, False), init)
    carry = step(jd, carry, True)
    for a, (m, l, acc) in enumerate(carry):
        g = gate_ref[0, a * ts:(a + 1) * ts, :]
        o_ref[0, a * ts:(a + 1) * ts, :] = (acc / l * _sigmoid(g)).astype(o_ref.dtype)


def _flash(qn, knb, vb, cr4, proj3, *, heads):
    b, t, bw = qn.shape
    tk = _pick(t, 512, 128)
    tq = tk
    n_sub = 2 if tq % 256 == 0 else 1
    qspec = pl.BlockSpec((1, tq, 128), lambda i, h, j: (i, j, h))
    kvspec = pl.BlockSpec((1, t, 128), lambda i, h, j: (i, 0, h))
    ktspec = pl.BlockSpec((1, 128, t), lambda i, h, j: (i, h, 0))
    knb = jnp.swapaxes(knb, 1, 2)
    return pl.pallas_call(
        functools.partial(_flash_kernel, tq=tq, tk=tk, n_sub=n_sub),
        grid=(b, heads, t // tq),
        in_specs=[qspec, ktspec, kvspec,
                  pl.BlockSpec((1, 1, 1, t), lambda i, h, j: (i, h, 0, 0)),
                  pl.BlockSpec((1, tq, 128), lambda i, h, j: (i, j, 7 * heads + h))],
        out_specs=qspec,
        out_shape=jax.ShapeDtypeStruct((b, t, bw), BF16),
        compiler_params=_params(("arbitrary", "arbitrary", "arbitrary")),
        name="fox_flash",
    )(qn, knb, vb, cr4, proj3)


def _suffix_kernel(lf_ref, o_ref, *, heads):
    x = lf_ref[...]
    n = x.shape[1]
    lane = lax.broadcasted_iota(jnp.int32, (1, n), 1)
    incl = x
    tot = x
    d = heads
    while d < n:
        incl = incl + jnp.where(lane < n - d, pltpu.roll(incl, n - d, 1), 0.0)
        tot = tot + pltpu.roll(tot, d, 1)
        d *= 2
    o_ref[:, :n] = incl - x
    o_ref[:, n:] = tot


def _suffix(logf_flat, *, heads):
    n_pool, n = logf_flat.shape
    gp = _pick(n_pool, 256, GROUP)
    return pl.pallas_call(
        functools.partial(_suffix_kernel, heads=heads),
        grid=(n_pool // gp,),
        in_specs=[pl.BlockSpec((gp, n), lambda i: (i, 0))],
        out_specs=pl.BlockSpec((gp, 2 * n), lambda i: (i, 0)),
        out_shape=jax.ShapeDtypeStruct((n_pool, 2 * n), F32),
        compiler_params=_params(("arbitrary",)),
        name="page_suffix",
    )(logf_flat)


def _paged_kernel(pt_ref, *refs, heads, n_new, g_pages):
    del pt_ref
    qn_ref, knb_ref, vb_ref, cr_ref, gate_ref = refs[:5]
    k_refs = refs[5:5 + g_pages]
    v_refs = refs[5 + g_pages:5 + 2 * g_pages]
    s_refs = refs[5 + 2 * g_pages:5 + 3 * g_pages]
    o_ref, q_scr, m_scr, l_scr, acc_scr, tail_scr, cn_scr = refs[5 + 3 * g_pages:]
    p = pl.program_id(1)
    bw = heads * 128
    nr = n_new * heads
    first = GROUP - n_new
    n = tail_scr.shape[1]
    row_head = lax.broadcasted_iota(jnp.int32, (nr, 1), 0) & (heads - 1)

    @pl.when(p == 0)
    def _():
        lane_head = lax.broadcasted_iota(jnp.int32, (heads, bw), 1) // 128
        head_mask = lane_head == lax.broadcasted_iota(jnp.int32, (heads, bw), 0)
        qn = qn_ref[0].astype(F32)
        qbd = jnp.concatenate(
            [jnp.where(head_mask, jnp.broadcast_to(qn[first + i:first + i + 1, :], (heads, bw)), 0.0)
             for i in range(n_new)], axis=0)
        q_all = qbd[:, 0:128]
        for h in range(1, heads):
            q_all = q_all + qbd[:, h * 128:(h + 1) * 128]
        q_scr[...] = q_all.astype(BF16)
        cr = cr_ref[0][2 * heads:3 * heads, :]
        cn_col = jnp.concatenate([cr[:, first + i:first + i + 1] for i in range(n_new)], axis=0)
        cn_scr[...] = jnp.broadcast_to(cn_col, cn_scr.shape)
        cn_key = jnp.concatenate([cr] * n_new, axis=0)
        s = _mm(qbd.astype(BF16), knb_ref[0], NT) + cn_col - cn_key
        ri = lax.broadcasted_iota(jnp.int32, (nr, GROUP), 0) // heads
        ci = lax.broadcasted_iota(jnp.int32, (nr, GROUP), 1)
        s = jnp.where((ci >= first) & (ci - first <= ri), s, NEG)
        m0 = jnp.max(s, axis=-1, keepdims=True)
        p0 = jnp.exp(s - m0)
        m_scr[...] = jnp.broadcast_to(m0, m_scr.shape)
        l_scr[...] = jnp.broadcast_to(jnp.sum(p0, axis=-1, keepdims=True), l_scr.shape)
        full = _mm(p0.astype(BF16), vb_ref[0])
        acc0 = jnp.where(row_head == 0, full[:, 0:128], 0.0)
        for h in range(1, heads):
            acc0 = acc0 + jnp.where(row_head == h, full[:, h * 128:(h + 1) * 128], 0.0)
        acc_scr[...] = acc0
        tail_scr[...] = jnp.zeros_like(tail_scr)

    wide = lambda a: jnp.concatenate([a] * (n // 128), axis=1)
    q_all = q_scr[...]
    cn_w = wide(cn_scr[...])
    m = m_scr[...]
    l = l_scr[...]
    acc = acc_scr[...]
    tail = tail_scr[...]
    valid = (lax.broadcasted_iota(jnp.int32, (nr, n), 1) & (heads - 1)) == row_head
    scores = []
    for i in range(g_pages):
        blk = s_refs[i][0]
        bias = blk[:, :n] + tail
        tail = tail + blk[:, n:]
        kf = k_refs[i][0, 0].reshape(n, 128).astype(BF16)
        scores.append(jnp.where(valid, _mm(q_all, kf, NT) + bias + cn_w, NEG))
    s_max = scores[0]
    for s in scores[1:]:
        s_max = jnp.maximum(s_max, s)
    m_new = jnp.maximum(m, jnp.max(s_max, axis=-1, keepdims=True))
    corr = jnp.exp(m - m_new)
    m_w = wide(m_new)
    probs = [jnp.exp(s - m_w) for s in scores]
    p_sum = probs[0]
    for pr in probs[1:]:
        p_sum = p_sum + pr
    pv = _mm(probs[0].astype(BF16), v_refs[0][0, 0].reshape(n, 128).astype(BF16))
    for i in range(1, g_pages):
        pv = pv + _mm(probs[i].astype(BF16), v_refs[i][0, 0].reshape(n, 128).astype(BF16))
    l = l * corr + jnp.sum(p_sum, axis=-1, keepdims=True)
    acc = acc * corr + pv
    m_scr[...] = m_new
    l_scr[...] = l
    acc_scr[...] = acc
    tail_scr[...] = tail

    @pl.when(p == pl.num_programs(1) - 1)
    def _():
        o = jnp.concatenate([acc / l] * heads, axis=1)
        keep = (lax.broadcasted_iota(jnp.int32, (nr, bw), 1) // 128) == row_head
        o = jnp.where(keep, o, 0.0)
        rows = [jnp.zeros((first, bw), F32)]
        for i in range(n_new):
            rows.append(jnp.sum(o[i * heads:(i + 1) * heads, :], axis=0, keepdims=True))
        out = jnp.concatenate(rows, axis=0)
        o_ref[0] = (out * _sigmoid(gate_ref[0])).astype(o_ref.dtype)


def _paged(page_table, qn, knb, vb, cr, proj3, cache_k, cache_v, suffix3, *, heads, n_new):
    bd, _, bw = qn.shape
    n_pages = page_table.shape[1]
    page = cache_k.shape[2]
    n = page * heads
    g_pages = _pick(n_pages, 16, 1)
    rows_t = cr.shape[1]
    nr = n_new * heads
    seq = lambda blk_w, blk: pl.BlockSpec((1, GROUP, blk_w), lambda b, p, pt, blk=blk: (b, 0, blk))
    page_of = lambda b, p, pt, i: pt[b, n_pages - 1 - (p * g_pages + i)]
    kv_spec = lambda i: pl.BlockSpec((1, 1, page, heads, 128),
                                     lambda b, p, pt, i=i: (0, page_of(b, p, pt, i), 0, 0, 0))
    suf_spec = lambda i: pl.BlockSpec((1, 1, 2 * n), lambda b, p, pt, i=i: (page_of(b, p, pt, i), 0, 0))

    in_specs = [seq(bw, 0), seq(bw, 0), seq(bw, 0),
                pl.BlockSpec((1, rows_t, GROUP), lambda b, p, pt: (b, 0, 0)),
                seq(bw, 7)]
    in_specs += [kv_spec(i) for i in range(g_pages)] * 2
    in_specs += [suf_spec(i) for i in range(g_pages)]
    grid_spec = pltpu.PrefetchScalarGridSpec(
        num_scalar_prefetch=1,
        grid=(bd, n_pages // g_pages),
        in_specs=in_specs,
        out_specs=pl.BlockSpec((1, GROUP, bw), lambda b, p, pt: (b, 0, 0)),
        scratch_shapes=[pltpu.VMEM((nr, 128), BF16), pltpu.VMEM((nr, 128), F32), pltpu.VMEM((nr, 128), F32),
                        pltpu.VMEM((nr, 128), F32), pltpu.VMEM((1, n), F32), pltpu.VMEM((nr, 128), F32)],
    )
    return pl.pallas_call(
        functools.partial(_paged_kernel, heads=heads, n_new=n_new, g_pages=g_pages),
        grid_spec=grid_spec,
        out_shape=jax.ShapeDtypeStruct((bd, GROUP, bw), BF16),
        compiler_params=_params(("arbitrary", "arbitrary")),
        name="fox_paged",
    )(page_table, qn, knb, vb, cr, proj3, *([cache_k] * g_pages), *([cache_v] * g_pages), *([suffix3] * g_pages))


def _out_proj_kernel(oa_ref, ob_ref, wa_ref, wb_ref, x_ref, gt_ref, g_ref, o_ref):
    mix = _mm(oa_ref[...], wa_ref[...]) + _mm(ob_ref[...], wb_ref[...])
    normed = mix * lax.rsqrt(jnp.mean(mix * mix, axis=-1, keepdims=True) + EPS) * g_ref[...]
    o_ref[...] = x_ref[...] + gt_ref[0] * normed


def _out_proj(oa, ob, wa, wb, x2d, gt, g, *, tm, tiles_per_b):
    r, d = x2d.shape
    aw, bw = oa.shape[1], ob.shape[1]
    mr = gt.shape[1]
    return pl.pallas_call(
        _out_proj_kernel,
        grid=(r // tm,),
        in_specs=[pl.BlockSpec((tm, aw), lambda i: (i, 0)),
                  pl.BlockSpec((tm, bw), lambda i: (i, 0)),
                  pl.BlockSpec((aw, d), lambda i: (0, 0)),
                  pl.BlockSpec((bw, d), lambda i: (0, 0)),
                  pl.BlockSpec((tm, d), lambda i: (i, 0)),
                  pl.BlockSpec((1, mr, d), lambda i: (i // tiles_per_b, 0, 0)),
                  pl.BlockSpec((1, d), lambda i: (0, 0))],
        out_specs=pl.BlockSpec((tm, d), lambda i: (i, 0)),
        out_shape=jax.ShapeDtypeStruct((r, d), F32),
        compiler_params=_params(("arbitrary",)),
        name="out_proj",
    )(oa, ob, wa, wb, x2d, gt, g.reshape(1, d))


def _ffn_tail_kernel(ug_ref, uv_ref, hg_ref, hv_ref, wcg_ref, wcv_ref, bg_ref, bv_ref, wd_ref, x_ref, gt_ref, g_ref,
                     o_ref, acc_scr, *, tiles_per_b):
    i = pl.program_id(0)
    j = pl.program_id(1)
    first = (i % tiles_per_b) == 0

    def conv(u_ref, halo_ref, wc_ref, b_ref):
        x = u_ref[...]
        halo = jnp.where(first, 0.0, halo_ref[...])
        xe = jnp.concatenate([halo, x], axis=0)
        wc = wc_ref[...]
        return wc[2:3] * x + wc[1:2] * _shift_rows(xe, 1) + wc[0:1] * _shift_rows(xe, 2) + b_ref[...]

    gate = conv(ug_ref, hg_ref, wcg_ref, bg_ref)
    val = conv(uv_ref, hv_ref, wcv_ref, bv_ref)
    act = (gate * _sigmoid(gate) * val).astype(BF16)
    part = _mm(act, wd_ref[...])

    @pl.when(j == 0)
    def _():
        acc_scr[...] = part

    @pl.when(j > 0)
    def _():
        acc_scr[...] += part

    @pl.when(j == pl.num_programs(1) - 1)
    def _():
        y = acc_scr[...]
        normed = y * lax.rsqrt(jnp.mean(y * y, axis=-1, keepdims=True) + EPS) * g_ref[...]
        o_ref[...] = x_ref[...] + gt_ref[0] * normed


def _ffn_tail(up, wc, bc, wd, x2d, gt, g, *, tm, tiles_per_b, tf):
    r, d = x2d.shape
    fp = wd.shape[0]
    nf = fp // tf
    hb = tm // GROUP
    mr = gt.shape[1]
    halo = lambda off: pl.BlockSpec((GROUP, tf), lambda i, j, off=off: (jnp.maximum(i * hb - 1, 0), j + off))
    return pl.pallas_call(
        functools.partial(_ffn_tail_kernel, tiles_per_b=tiles_per_b),
        grid=(r // tm, nf),
        in_specs=[pl.BlockSpec((tm, tf), lambda i, j: (i, j)),
                  pl.BlockSpec((tm, tf), lambda i, j: (i, j + nf)),
                  halo(0), halo(nf),
                  pl.BlockSpec((3, tf), lambda i, j: (0, j)),
                  pl.BlockSpec((3, tf), lambda i, j: (0, j + nf)),
                  pl.BlockSpec((1, tf), lambda i, j: (0, j)),
                  pl.BlockSpec((1, tf), lambda i, j: (0, j + nf)),
                  pl.BlockSpec((tf, d), lambda i, j: (j, 0)),
                  pl.BlockSpec((tm, d), lambda i, j: (i, 0)),
                  pl.BlockSpec((1, mr, d), lambda i, j: (i // tiles_per_b, 0, 0)),
                  pl.BlockSpec((1, d), lambda i, j: (0, 0))],
        out_specs=pl.BlockSpec((tm, d), lambda i, j: (i, 0)),
        out_shape=jax.ShapeDtypeStruct((r, d), F32),
        scratch_shapes=[pltpu.VMEM((tm, d), F32)],
        compiler_params=_params(("arbitrary", "arbitrary")),
        name="ffn_tail",
    )(up, up, up, up, wc, wc, bc, bc, wd, x2d, gt, g.reshape(1, d))


def _ffn_fused_kernel(x_ref, sc_ref, sh_ref, gpre_ref, wg_ref, wv_ref, wcg_ref, wcv_ref, bg_ref, bv_ref, wd_ref,
                      gt_ref, gpost_ref, o_ref, lg_ref, lv_ref, h_scr, cg_scr, cv_scr, xg_scr, xv_scr, act_scr,
                      *, tiles_per_b):
    i = pl.program_id(0)
    j = pl.program_id(1)
    tm = x_ref.shape[0]
    first = (i % tiles_per_b) == 0

    @pl.when(j == 0)
    def _():
        x = x_ref[...]
        y = x * lax.rsqrt(jnp.mean(x * x, axis=-1, keepdims=True) + EPS) * gpre_ref[...]
        h_scr[...] = (y * (1.0 + sc_ref[0]) + sh_ref[0]).astype(BF16)
        o_ref[...] = jnp.zeros_like(o_ref)

    tf = wd_ref.shape[0]
    ts = tm // FFN_SPLIT

    def project(s, w_ref, xe_scr):
        xe_scr[GROUP + s * ts:GROUP + (s + 1) * ts, :] = _mm(h_scr[s * ts:(s + 1) * ts, :], w_ref[0])

    xg_scr[0:GROUP, :] = jnp.where(first, 0.0, cg_scr[j])
    xv_scr[0:GROUP, :] = jnp.where(first, 0.0, cv_scr[j])
    for s in range(FFN_SPLIT):
        project(s, wg_ref, xg_scr)
        project(s, wv_ref, xv_scr)

    def conv(xe_scr, wc_ref, b_ref, r0, c0):
        xe = xe_scr[r0:r0 + FFN_ROWS + GROUP, c0:c0 + FFN_LANES]
        wc = wc_ref[:, c0:c0 + FFN_LANES]
        return (wc[2:3] * xe[GROUP:] + wc[1:2] * _shift_rows(xe, 1) + wc[0:1] * _shift_rows(xe, 2)
                + b_ref[:, c0:c0 + FFN_LANES])

    for s in range(FFN_SPLIT):
        for r0 in range(s * ts, (s + 1) * ts, FFN_ROWS):
            for c0 in range(0, tf, FFN_LANES):
                gate = conv(xg_scr, wcg_ref, bg_ref, r0, c0)
                val = conv(xv_scr, wcv_ref, bv_ref, r0, c0)
                act_scr[r0:r0 + FFN_ROWS, c0:c0 + FFN_LANES] = (gate * _sigmoid(gate) * val).astype(BF16)
        o_ref[s * ts:(s + 1) * ts, :] += _mm(act_scr[s * ts:(s + 1) * ts, :], wd_ref[...])

    for xe_scr, carry_scr, last_ref in ((xg_scr, cg_scr, lg_ref), (xv_scr, cv_scr, lv_ref)):
        last = xe_scr[tm:tm + GROUP, :]
        carry_scr[j] = last
        last_ref[0] = last

    @pl.when(j == pl.num_programs(1) - 1)
    def _():
        y = o_ref[...]
        normed = y * lax.rsqrt(jnp.mean(y * y, axis=-1, keepdims=True) + EPS) * gpost_ref[...]
        o_ref[...] = x_ref[...] + gt_ref[0] * normed


def _ffn_fused(x2d, sc, sh, gt, g_pre, g_post, w_up_t, wc, bc, wd, *, nb, tm):
    r, d = x2d.shape
    fp = wd.shape[0]
    tf = w_up_t.shape[2]
    nf = fp // tf
    tiles_per_b = r // nb // tm
    mod = pl.BlockSpec((1, 1, d), lambda i, j: (i // tiles_per_b, 0, 0))
    vec = pl.BlockSpec((1, d), lambda i, j: (0, 0))
    col = lambda rows, off: pl.BlockSpec((rows, tf), lambda i, j, off=off: (0, j + off))
    wtile = lambda off: pl.BlockSpec((1, d, tf), lambda i, j, off=off: (j + off, 0, 0))
    last = pl.BlockSpec((1, GROUP, tf), lambda i, j: (i, 0, j))
    rows_once = lambda: pl.BlockSpec((tm, d), lambda i, j: (i, 0), pipeline_mode=pl.Buffered(1))
    return pl.pallas_call(
        functools.partial(_ffn_fused_kernel, tiles_per_b=tiles_per_b),
        grid=(r // tm, nf),
        in_specs=[rows_once(), mod, mod, vec,
                  wtile(0), wtile(nf), col(3, 0), col(3, nf), col(1, 0), col(1, nf),
                  pl.BlockSpec((tf, d), lambda i, j: (j, 0)), mod, vec],
        out_specs=[rows_once(), last, last],
        out_shape=[jax.ShapeDtypeStruct((r, d), F32), jax.ShapeDtypeStruct((r // tm, GROUP, fp), F32),
                   jax.ShapeDtypeStruct((r // tm, GROUP, fp), F32)],
        scratch_shapes=[pltpu.VMEM((tm, d), BF16),
                        pltpu.VMEM((nf, GROUP, tf), F32), pltpu.VMEM((nf, GROUP, tf), F32),
                        pltpu.VMEM((tm + GROUP, tf), F32), pltpu.VMEM((tm + GROUP, tf), F32),
                        pltpu.VMEM((tm, tf), BF16)],
        compiler_params=_params(("arbitrary", "arbitrary")),
        name="ffn_fused",
    )(x2d, sc, sh, g_pre.reshape(1, d), w_up_t, w_up_t, wc, wc, bc, bc, wd, gt, g_post.reshape(1, d))


def _pad_cols(a, n):
    return jnp.pad(a, [(0, 0)] * (a.ndim - 1) + [(0, n - a.shape[-1])])


def _splice_rows(a, rows, start):
    k = rows.shape[1]
    padded = jnp.pad(rows, ((0, 0), (start, a.shape[1] - start - k), (0, 0)))
    idx = lax.broadcasted_iota(jnp.int32, (1, a.shape[1], 1), 1)
    return jnp.where((idx >= start) & (idx < start + k), padded, a)


def _split_hi_lo(w):
    hi = w.astype(BF16)
    return hi, (w - hi.astype(F32)).astype(BF16)


def _layer(x3, mods, st_conv, st_gdn, st_ffn, fox, lw, *, heads, front_pad, tm, q_scale):
    (g_pre_mix, g_post_mix, g_pre_ffn, g_post_ffn, w_big, ws, w_conv_qkv, a_row, a_col, gn, qg, kg, brow, bcol,
     w_out_a, w_out_b, w_up, w_conv_ffn, b_conv_ffn, w_down, d_ff, tf) = lw
    sh_m, sc_m, gt_m, sh_f, sc_f, gt_f = mods
    nb, t, d = x3.shape
    aw = heads * 128
    x2d = x3.reshape(nb * t, d)
    tiles_per_b = max(t // tm, 1) if mods[0].shape[1] == 1 else 1

    proj, small = _norm_proj(x2d, sc_m, sh_m, g_pre_mix, w_big, ws, tm=tm, tiles_per_b=tiles_per_b)
    proj3 = proj.reshape(nb, t, 8 * aw)
    if st_conv is not None:
        k = st_conv.shape[1]
        proj3 = _splice_rows(proj3, st_conv, GROUP - (t - front_pad) - k)
    small3 = small.reshape(nb, t, 128)
    rows_t = 3 * GROUP
    smallt3 = jnp.swapaxes(small3[:, :, :rows_t], 1, 2)

    o_a, gdn_new = _gdn(proj3, small3, smallt3, w_conv_qkv, a_row, a_col, gn, st_gdn,
                        heads=heads, front_pad=front_pad)
    qn, kn, knb, vb, logf, cr = _fox_prep(proj3, small3, smallt3, qg, kg, brow, bcol,
                                               heads=heads, front_pad=front_pad, q_scale=q_scale)
    o_b = fox(qn, knb, vb, cr, proj3)

    x1 = _out_proj(o_a.reshape(nb * t, aw), o_b.reshape(nb * t, aw), w_out_a, w_out_b, x2d, gt_m, g_post_mix,
                   tm=min(tm, 512), tiles_per_b=max(t // min(tm, 512), 1) if mods[0].shape[1] == 1 else 1)
    fp = w_down.shape[0]
    unpad = lambda g, v: jnp.concatenate([g[..., :d_ff], v[..., :d_ff]], axis=-1)
    if st_ffn is None:
        w_up_t = jnp.swapaxes(w_up.reshape(d, 2 * fp // tf, tf), 0, 1)
        y, last_g, last_v = _ffn_fused(x1, sc_f, sh_f, gt_f, g_pre_ffn, g_post_ffn, w_up_t, w_conv_ffn, b_conv_ffn,
                                       w_down, nb=nb, tm=tm)
        per_b = last_g.shape[0] // nb
        up_last = unpad(last_g[per_b - 1::per_b], last_v[per_b - 1::per_b])
    else:
        up = _norm_proj(x1, sc_f, sh_f, g_pre_ffn, w_up, tm=tm, tiles_per_b=tiles_per_b)
        up3 = up.reshape(nb, t, 2 * fp)
        k = st_ffn.shape[1]
        up3 = _splice_rows(up3, st_ffn, GROUP - (t - front_pad) - k)
        y = _ffn_tail(up3.reshape(nb * t, 2 * fp), w_conv_ffn, b_conv_ffn, w_down, x1, gt_f, g_post_ffn,
                      tm=min(tm, 512), tiles_per_b=1, tf=tf)
        up_last = unpad(up3[:, t - GROUP:, :fp], up3[:, t - GROUP:, fp:])
    return y.reshape(nb, t, d), proj3, kn, logf, gdn_new, up_last


def kernel(x_prompt, x_sample, cache_k, cache_v, cache_logf, state_gdn, state_conv_qkv, state_ffn_conv, page_table, c_prompt, c_sample, w_ada, b_ada, g_pre_mix, g_post_mix, g_pre_ffn, g_post_ffn, w_in, w_conv_qkv, a_log, dt_bias, g_gdn_norm, q_norm, k_norm, b_forget, w_out, w_up, w_conv_ffn, b_conv_ffn, w_down):
    depth = w_ada.shape[0]
    assert depth == 1, "single-layer step"
    b, t, d = x_prompt.shape
    bd, n_new, _ = x_sample.shape
    heads = state_gdn.shape[2]
    dh = state_gdn.shape[3]
    assert dh == 128 and cache_k.shape[3] == heads and n_new <= GROUP // 2
    aw = heads * dh
    page = cache_k.shape[2]
    n_pool = cache_k.shape[1]
    d_ff = w_down.shape[1]
    conv_a = w_conv_qkv.shape[1]
    ffn_conv = w_conv_ffn.shape[1]
    assert conv_a == 4 and ffn_conv == 3
    layer = 0
    (cache_logf, state_gdn, state_conv_qkv, state_ffn_conv, w_ada, b_ada, g_pre_mix, g_post_mix,
     g_pre_ffn, g_post_ffn, w_in, w_conv_qkv, a_log, dt_bias, g_gdn_norm, q_norm, k_norm, b_forget, w_out, w_up,
     w_conv_ffn, b_conv_ffn, w_down) = [
        (a.reshape(a.shape[1:]),) for a in
        (cache_logf, state_gdn, state_conv_qkv, state_ffn_conv, w_ada, b_ada, g_pre_mix, g_post_mix,
         g_pre_ffn, g_post_ffn, w_in, w_conv_qkv, a_log, dt_bias, g_gdn_norm, q_norm, k_norm, b_forget, w_out, w_up,
         w_conv_ffn, b_conv_ffn, w_down)]

    wi = w_in[layer]
    o1 = 4 * aw
    o2 = o1 + 2 * heads
    o3 = o2 + 4 * aw
    w_big = jnp.concatenate([wi[:, :o1], wi[:, o2:o3]], axis=1).astype(BF16)
    w_small = _pad_cols(jnp.concatenate([wi[:, o1:o2], wi[:, o3:]], axis=1), 128)
    ws = _split_hi_lo(w_small)
    zeros_h = jnp.zeros((heads,), F32)
    a_row = _pad_cols(jnp.stack([jnp.concatenate([zeros_h, a_log[layer]]),
                                 jnp.concatenate([zeros_h, dt_bias[layer]])]), 128)
    rows_t = 3 * GROUP
    a_col = jnp.pad(a_row[:, :rows_t].T, ((0, 0), (0, 0)))
    brow = _pad_cols(jnp.concatenate([zeros_h, zeros_h, b_forget[layer]])[None, :], 128)
    bcol = brow[:, :rows_t].T
    assert heads == GROUP and page == 128
    gn = g_gdn_norm[layer].reshape(1, dh)
    qg = q_norm[layer].reshape(1, dh)
    kg = k_norm[layer].reshape(1, dh)
    wo = w_out[layer].astype(BF16)
    w_out_a, w_out_b = wo[:aw], wo[aw:]
    tf = 512
    fp = -(-d_ff // tf) * tf
    wu = w_up[layer]
    w_up_p =jnp.concatenate([_pad_cols(wu[:, :d_ff], fp), _pad_cols(wu[:, d_ff:], fp)], axis=1).astype(BF16)
    wcf = w_conv_ffn[layer]
    w_conv_ffn_p =jnp.concatenate([_pad_cols(wcf[:, :d_ff], fp), _pad_cols(wcf[:, d_ff:], fp)], axis=1)
    bcf = b_conv_ffn[layer][None, :]
    b_conv_ffn_p = jnp.concatenate([_pad_cols(bcf[:, :d_ff], fp), _pad_cols(bcf[:, d_ff:], fp)], axis=1)
    w_down_p = jnp.pad(w_down[layer], ((0, fp - d_ff), (0, 0))).astype(BF16)
    lw = (g_pre_mix[layer], g_post_mix[layer], g_pre_ffn[layer], g_post_ffn[layer], w_big, ws, w_conv_qkv[layer],
          a_row, a_col, gn, qg, kg, brow, bcol, w_out_a, w_out_b, w_up_p, w_conv_ffn_p, b_conv_ffn_p, w_down_p,
          d_ff, tf)

    n_c = b + bd
    c_all = jnp.pad(jnp.concatenate([c_prompt, c_sample], axis=0), ((0, -n_c % GROUP), (0, 0)))
    mod = _ada(c_all, w_ada[layer], b_ada[layer])
    mods_p = [m[:b].reshape(b, 1, d) for m in jnp.split(mod, 6, axis=-1)]
    mods_s = [jnp.repeat(m[b:n_c], GROUP, axis=0).reshape(1, bd * GROUP, d) for m in jnp.split(mod, 6, axis=-1)]

    tm_p = _pick(t, 1024, 128)
    fox_p = lambda qn, knb, vb, cr, proj3: _flash(
        qn, knb, vb, cr[:, 2 * heads:3 * heads].reshape(b, heads, 1, t), proj3, heads=heads)
    zeros_s0 = jnp.zeros((b, heads, dh, dh), F32)
    y_p, proj_p, kn_p, logf_p, gdn_p, up_p = _layer(
        x_prompt, mods_p, None, zeros_s0, None, fox_p, lw, heads=heads, front_pad=0, tm=tm_p,
        q_scale=dh ** -0.5 * LOG2E)

    front = GROUP - n_new
    x_s = jnp.pad(x_sample, ((0, 0), (front, 0), (0, 0)))
    suffix = _suffix(cache_logf[layer].reshape(n_pool, page * heads), heads=heads)
    suffix3 = suffix.reshape(n_pool, 1, 2 * page * heads)
    fox_s = lambda qn, knb, vb, cr, proj3: _paged(
        page_table, qn, knb, vb, cr, proj3, cache_k, cache_v, suffix3, heads=heads, n_new=n_new)
    st_ffn = state_ffn_conv[layer]
    st_ffn_p =jnp.concatenate([_pad_cols(st_ffn[:, :, :d_ff], fp), _pad_cols(st_ffn[:, :, d_ff:], fp)], axis=-1)
    st_conv = _pad_cols(state_conv_qkv[layer], 8 * aw)
    y_s, proj_s, kn_s, logf_s, gdn_s, up_s = _layer(
        x_s, mods_s, st_conv, state_gdn[layer], st_ffn_p, fox_s, lw, heads=heads, front_pad=front, tm=bd * GROUP,
        q_scale=dh ** -0.5)

    n_pg = t // page
    k_prompt = kn_p.reshape(1, b, n_pg, page, heads, dh)
    v_prompt = proj_p[:, :, 6 * aw:7 * aw].reshape(1, b, n_pg, page, heads, dh)
    logf_prompt = logf_p[:, :, 2 * heads:3 * heads].reshape(1, b, n_pg, page, heads)
    conv_qkv_prompt = proj_p[:, t - (conv_a - 1):, :3 * aw][None]
    ffn_conv_prompt = up_p[:, GROUP - (ffn_conv - 1):, :][None]
    k_sample = kn_s[:, front:].reshape(1, bd, n_new, heads, dh)
    v_sample = proj_s[:, front:, 6 * aw:7 * aw].reshape(1, bd, n_new, heads, dh)
    logf_sample = logf_s[:, front:, 2 * heads:3 * heads][None]
    conv_qkv_sample = proj_s[:, GROUP - (conv_a - 1):, :3 * aw][None]
    ffn_conv_sample = up_s[:, GROUP - (ffn_conv - 1):, :][None]
    return (y_p, y_s[:, front:], k_prompt, v_prompt, logf_prompt, gdn_p[None], conv_qkv_prompt, ffn_conv_prompt,
            k_sample, v_sample, logf_sample, gdn_s.astype(state_gdn[layer].dtype)[None], conv_qkv_sample, ffn_conv_sample)
```
